```python
import math, functools
import jax, jax.numpy as jnp
from jax import lax
import numpy as np

D_MODEL = 2048
BATCH = 8
SEQ = 2048
DEPTH = 2
DEC_BATCH = 8
DEC_SEQ = 64
PAST_LEN = 2048

CHUNK = 64
Q_BLOCK = 128
N_MEM = 256
N_EVEN = (DEPTH + 1) // 2
N_ODD = DEPTH // 2
EPS = 1e-6
HA = 4
DA = D_MODEL // 8
DVA = D_MODEL // 8
ROPE_BASE = 10000.0
HB = 4
DB = D_MODEL // 16
DVB = D_MODEL // 8
HC = 4
DKC = D_MODEL // 8
DVC = D_MODEL // 4
GATE_RANK = 16
GATE_TAU = 16.0
HX = 4
DX = D_MODEL // HX
D_FF = 11 * D_MODEL // 4
N_EXPERTS = 8
TOP_K = 2
D_FF_EXPERT = 11 * D_MODEL // 4
MOE_BLOCK = 128
EVEN_SIZES = (HA * DA, HA * DA, HA * DVA, HA * DVA, HB * 2 * DB, HB * 2 * DB, HB * DVB)
ODD_SIZES = (HC * DKC, HC * DKC, HC * DVC, HC * DVC, GATE_RANK)

kernel_name = 'hybrid_streaming_encoder_step'


def _split_points(sizes):
    return np.cumsum(sizes)[:-1].tolist()


def rms_norm(x, g):
    xf = x.astype(jnp.float32)
    y = xf * lax.rsqrt(jnp.mean(xf * xf, axis=-1, keepdims=True) + EPS) * g.astype(jnp.float32)
    return y.astype(x.dtype)


def head_norm(x, center):
    x = x.astype(jnp.float32)
    if center:
        x = x - jnp.mean(x, axis=-1, keepdims=True)
    return x * lax.rsqrt(jnp.mean(x * x, axis=-1, keepdims=True) + EPS)


def rotary(x, pos):
    half = x.shape[-1] // 2
    inv = 1.0 / (ROPE_BASE ** jnp.linspace(0.0, 1.0, half, dtype=jnp.float32))
    ang = pos.astype(jnp.float32)[:, None] * inv[None, :]
    cos = jnp.cos(ang)[None, :, None, :]
    sin = jnp.sin(ang)[None, :, None, :]
    x1, x2 = x[..., :half], x[..., half:]
    return jnp.concatenate([x1 * cos - x2 * sin, x1 * sin + x2 * cos], axis=-1)


def chunk_scan(step, state0, seqs):
    t = seqs[0].shape[1]
    c = CHUNK if t % CHUNK == 0 else t
    nc = t // c

    def to_chunks(a):
        return jnp.swapaxes(a.reshape(a.shape[0], nc, c, *a.shape[2:]), 0, 1)

    state, ys = lax.scan(step, state0, tuple(to_chunks(a) for a in seqs))
    ys = jnp.swapaxes(ys, 0, 1)
    return ys.reshape(ys.shape[0], t, *ys.shape[3:]), state


def retention_chunk(state, qkv, log_gamma):
    q, k, v = qkv
    L = q.shape[1]
    n = jnp.arange(L, dtype=jnp.float32)
    diff = n[:, None] - n[None, :]
    decay = jnp.where(diff >= 0.0, jnp.exp(jnp.maximum(diff, 0.0)[None] * log_gamma[:, None, None]), 0.0)
    scores = jnp.einsum('blhd,bmhd->bhlm', q, k) * decay[None]
    inner = jnp.einsum('bhlm,bmhe->blhe', scores, v)
    cross = jnp.einsum('blhd,bhde->blhe', q, state) * jnp.exp((n[:, None] + 1.0) * log_gamma[None, :])[None, :, :, None]
    k_dec = k * jnp.exp((L - 1.0 - n)[:, None] * log_gamma[None, :])[None, :, :, None]
    new_state = state * jnp.exp(L * log_gamma)[None, :, None, None] + jnp.einsum('blhd,blhe->bhde', k_dec, v)
    return new_state, inner + cross


def gla_chunk(state, xs):
    q, k, v, log_a = xs
    L = q.shape[1]
    b = jnp.cumsum(log_a, axis=1)
    ref = b[:, L // 2][:, None]
    q_in = q * jnp.exp(b - ref)
    k_in = k * jnp.exp(ref - b)
    causal = jnp.tril(jnp.ones((L, L), dtype=bool))
    att = jnp.where(causal[None, None], jnp.einsum('blhd,bmhd->bhlm', q_in, k_in), 0.0)
    inner = jnp.einsum('bhlm,bmhe->blhe', att, v)
    cross = jnp.einsum('blhd,bhde->blhe', q * jnp.exp(b), state)
    b_last = b[:, -1]
    new_state = jnp.exp(b_last)[..., None] * state + jnp.einsum('blhd,blhe->bhde', k * jnp.exp(b_last[:, None] - b), v)
    return new_state, inner + cross


def diff_softmax_attention(q, k, v, q_pos, k_pos, lam, lam_init):
    s = jnp.einsum('bqhcd,bkhcd->bchqk', q.astype(jnp.float32), k.astype(jnp.float32)) * DB ** -0.5
    mask = (k_pos // CHUNK)[None, :] <= (q_pos // CHUNK)[:, None]
    p = jax.nn.softmax(jnp.where(mask[None, None, None], s, -jnp.inf), axis=-1)
    w = p[:, 0] - lam * p[:, 1]
    o = jnp.einsum('bhqk,bkhe->bqhe', w, v.astype(jnp.float32))
    return head_norm(o, False) * (1.0 - lam_init)


def diff_attention(q, k, v, q_pos, k_pos, lam, lam_init):
    bsz, lq = q.shape[0], q.shape[1]
    if lq <= Q_BLOCK or lq % Q_BLOCK:
        return diff_softmax_attention(q, k, v, q_pos, k_pos, lam, lam_init)
    nb = lq // Q_BLOCK
    qb = jnp.swapaxes(q.reshape(bsz, nb, Q_BLOCK, *q.shape[2:]), 0, 1)
    pb = q_pos.reshape(nb, Q_BLOCK)
    ob = lax.map(lambda a: diff_softmax_attention(a[0], k, v, a[1], k_pos, lam, lam_init), (qb, pb))
    return jnp.swapaxes(ob, 0, 1).reshape(bsz, lq, *ob.shape[3:])


def even_mixer(h, pos, key_pos, ret0, past_k, past_v, w_in, w_out, lq1, lk1, lq2, lk2, lam_init):
    bsz, t, _ = h.shape
    f32 = jnp.float32
    rq, rk, rv, rg, dq, dk, dv = jnp.split(h @ w_in, _split_points(EVEN_SIZES), axis=-1)
    rq = rotary(rq.reshape(bsz, t, HA, DA).astype(f32), pos)
    rk = rotary(rk.reshape(bsz, t, HA, DA).astype(f32), pos) * DA ** -0.5
    rv = rv.reshape(bsz, t, HA, DVA).astype(f32)
    log_gamma = jnp.log1p(-jnp.power(2.0, -5.0 - jnp.arange(HA, dtype=f32)))
    step = functools.partial(retention_chunk, log_gamma=log_gamma)
    ret, ret_state = chunk_scan(step, ret0.astype(f32), (rq, rk, rv))
    a_out = jax.nn.silu(rg) * head_norm(ret, True).reshape(bsz, t, HA * DVA).astype(h.dtype)
    dq = dq.reshape(bsz, t, HB, 2, DB)
    dk = dk.reshape(bsz, t, HB, 2, DB)
    dv = dv.reshape(bsz, t, HB, DVB)
    k_all = dk if past_k is None else jnp.concatenate([past_k.astype(dk.dtype), dk], axis=1)
    v_all = dv if past_v is None else jnp.concatenate([past_v.astype(dv.dtype), dv], axis=1)
    lam = (jnp.exp(jnp.sum(lq1.astype(f32) * lk1.astype(f32)))
           - jnp.exp(jnp.sum(lq2.astype(f32) * lk2.astype(f32))) + lam_init)
    b_out = diff_attention(dq, k_all, v_all, pos, key_pos, lam, lam_init).reshape(bsz, t, HB * DVB).astype(h.dtype)
    out = jnp.concatenate([a_out, b_out], axis=-1) @ w_out
    return out, ret_state.astype(h.dtype), dk, dv


def odd_mixer(h, gla0, w_in, w_a2, b_a, w_out):
    bsz, t, _ = h.shape
    f32 = jnp.float32
    q, k, v, g, a_lr = jnp.split(h @ w_in, _split_points(ODD_SIZES), axis=-1)
    log_a = jax.nn.log_sigmoid((a_lr @ w_a2 + b_a).astype(f32)) / GATE_TAU
    q = q.reshape(bsz, t, HC, DKC).astype(f32) * DKC ** -0.5
    k = k.reshape(bsz, t, HC, DKC).astype(f32)
    v = v.reshape(bsz, t, HC, DVC).astype(f32)
    log_a = log_a.reshape(bsz, t, HC, DKC)
    o, state = chunk_scan(gla_chunk, gla0.astype(f32), (q, k, v, log_a))
    o = head_norm(o, False).reshape(bsz, t, HC * DVC).astype(h.dtype)
    return (jax.nn.silu(g) * o) @ w_out, state.astype(h.dtype)


def memory_kv(mem, g, w_kv):
    bsz, m, _ = mem.shape
    k, v = jnp.split(rms_norm(mem, g) @ w_kv, 2, axis=-1)
    return k.reshape(bsz, m, HX, DX), v.reshape(bsz, m, HX, DX)


def cross_attention(h, mk, mv, w_q, w_o):
    bsz, t, _ = h.shape
    q = (h @ w_q).reshape(bsz, t, HX, DX).astype(jnp.float32)
    s = jnp.einsum('bqhd,bkhd->bhqk', q, mk.astype(jnp.float32)) * DX ** -0.5
    p = jax.nn.softmax(s, axis=-1)
    o = jnp.einsum('bhqk,bkhd->bqhd', p, mv.astype(jnp.float32)).reshape(bsz, t, HX * DX).astype(h.dtype)
    return o @ w_o


def swiglu(h, w_gu, w_dn):
    g, u = jnp.split(h @ w_gu, 2, axis=-1)
    return (jax.nn.silu(g) * u) @ w_dn


def moe_swiglu(h, w_router, w_gu, w_dn):
    n_tok, d = h.shape
    n_rows = n_tok * TOP_K
    logits = (h @ w_router).astype(jnp.float32)
    top_logit, top_e = lax.top_k(logits, TOP_K)
    gate = jax.nn.softmax(top_logit, axis=-1).reshape(-1)
    flat_e = top_e.reshape(-1).astype(jnp.int32)
    order = jnp.argsort(flat_e)
    e_sorted = flat_e[order]
    tok_sorted = (order // TOP_K).astype(jnp.int32)
    gate_sorted = gate[order]
    counts = jnp.zeros((N_EXPERTS,), jnp.int32).at[flat_e].add(1)
    padded = (counts + MOE_BLOCK - 1) // MOE_BLOCK * MOE_BLOCK
    pad_end = jnp.cumsum(padded)
    pad_start = pad_end - padded
    grp_start = jnp.cumsum(counts) - counts
    dest = pad_start[e_sorted] + jnp.arange(n_rows, dtype=jnp.int32) - grp_start[e_sorted]
    n_blocks = (n_rows + N_EXPERTS * (MOE_BLOCK - 1) + MOE_BLOCK - 1) // MOE_BLOCK
    row_tok = jnp.full((n_blocks * MOE_BLOCK,), n_tok, jnp.int32).at[dest].set(tok_sorted)
    h_ext = jnp.concatenate([h, jnp.zeros((1, d), h.dtype)], axis=0)
    x_blocks = h_ext[row_tok].reshape(n_blocks, MOE_BLOCK, d)
    block_start = jnp.arange(n_blocks, dtype=jnp.int32) * MOE_BLOCK
    block_expert = jnp.minimum(jnp.searchsorted(pad_end, block_start, side='right'), N_EXPERTS - 1)

    def expert_block(args):
        xb, e = args
        return swiglu(xb, w_gu[e], w_dn[e])

    y_rows = lax.map(expert_block, (x_blocks, block_expert)).reshape(n_blocks * MOE_BLOCK, d)
    contrib = y_rows[dest] * gate_sorted[:, None].astype(h.dtype)
    return jnp.zeros_like(h).at[tok_sorted].add(contrib)


def setup_inputs(seed: int = 0) -> dict:
    key = jax.random.key(seed)
    ks = iter(jax.random.split(key, 48))
    f32 = jnp.float32

    def nrm(shape, scale):
        return jax.random.normal(next(ks), shape, f32) * scale

    def gain(shape):
        return 1.0 + nrm(shape, 0.01)

    d = D_MODEL
    return {
        'x_prompt': nrm((BATCH, SEQ, d), 1.0),
        'x_sample': nrm((DEC_BATCH, DEC_SEQ, d), 1.0),
        'mem_prompt': nrm((BATCH, N_MEM, d), 1.0),
        'cache_diff_k': nrm((N_EVEN, DEC_BATCH, PAST_LEN, HB, 2, DB), 1.0),
        'cache_diff_v': nrm((N_EVEN, DEC_BATCH, PAST_LEN, HB, DVB), 1.0),
        'state_ret': nrm((N_EVEN, DEC_BATCH, HA, DA, DVA), 0.5),
        'state_gla': nrm((N_ODD, DEC_BATCH, HC, DKC, DVC), 1.0),
        'cache_mem_k': nrm((DEPTH, DEC_BATCH, N_MEM, HX, DX), 1.0),
        'cache_mem_v': nrm((DEPTH, DEC_BATCH, N_MEM, HX, DX), 1.0),
        'g_mix': gain((DEPTH, d)),
        'g_xattn': gain((DEPTH, d)),
        'g_mem': gain((DEPTH, d)),
        'g_ffn': gain((DEPTH, d)),
        'g_final': gain((d,)),
        'w_in_even': nrm((N_EVEN, d, sum(EVEN_SIZES)), d ** -0.5),
        'w_out_even': nrm((N_EVEN, HA * DVA + HB * DVB, d), (HA * DVA + HB * DVB) ** -0.5),
        'lambda_q1': nrm((N_EVEN, DB), 0.1),
        'lambda_k1': nrm((N_EVEN, DB), 0.1),
        'lambda_q2': nrm((N_EVEN, DB), 0.1),
        'lambda_k2': nrm((N_EVEN, DB), 0.1),
        'w_in_odd': nrm((N_ODD, d, sum(ODD_SIZES)), d ** -0.5),
        'w_gate_lr': nrm((N_ODD, GATE_RANK, HC * DKC), GATE_RANK ** -0.5),
        'b_gate_lr': nrm((N_ODD, HC * DKC), 0.1),
        'w_out_odd': nrm((N_ODD, HC * DVC, d), (HC * DVC) ** -0.5),
        'w_xq': nrm((DEPTH, d, HX * DX), d ** -0.5),
        'w_xkv': nrm((DEPTH, d, 2 * HX * DX), d ** -0.5),
        'w_xo': nrm((DEPTH, HX * DX, d), (HX * DX) ** -0.5),
        'w_ffn_gu': nrm((N_EVEN, d, 2 * D_FF), d ** -0.5),
        'w_ffn_dn': nrm((N_EVEN, D_FF, d), D_FF ** -0.5),
        'w_router': nrm((N_ODD, d, N_EXPERTS), d ** -0.5),
        'w_moe_gu': nrm((N_ODD, N_EXPERTS, d, 2 * D_FF_EXPERT), d ** -0.5),
        'w_moe_dn': nrm((N_ODD, N_EXPERTS, D_FF_EXPERT, d), D_FF_EXPERT ** -0.5),
    }


def reference(x_prompt, x_sample, mem_prompt, cache_diff_k, cache_diff_v, state_ret, state_gla,
              cache_mem_k, cache_mem_v, g_mix, g_xattn, g_mem, g_ffn, g_final,
              w_in_even, w_out_even, lambda_q1, lambda_k1, lambda_q2, lambda_k2,
              w_in_odd, w_gate_lr, b_gate_lr, w_out_odd, w_xq, w_xkv, w_xo,
              w_ffn_gu, w_ffn_dn, w_router, w_moe_gu, w_moe_dn):

    def trunk(x, pos, key_pos, mem_k, mem_v, ret0, gla0, past_k, past_v):
        bsz = x.shape[0]
        ret_s, gla_s, k_rows, v_rows = [], [], [], []
        for i in range(DEPTH):
            j = i // 2
            h = rms_norm(x, g_mix[i])
            if i % 2 == 0:
                r0 = jnp.zeros((bsz, HA, DA, DVA), jnp.float32) if ret0 is None else ret0[j]
                pk = None if past_k is None else past_k[j]
                pv = None if past_v is None else past_v[j]
                lam_init = 0.8 - 0.6 * math.exp(-0.3 * i)
                out, s, kr, vr = even_mixer(h, pos, key_pos, r0, pk, pv, w_in_even[j], w_out_even[j],
                                            lambda_q1[j], lambda_k1[j], lambda_q2[j], lambda_k2[j], lam_init)
                ret_s.append(s)
                k_rows.append(kr)
                v_rows.append(vr)
            else:
                s0 = jnp.zeros((bsz, HC, DKC, DVC), jnp.float32) if gla0 is None else gla0[j]
                out, s = odd_mixer(h, s0, w_in_odd[j], w_gate_lr[j], b_gate_lr[j], w_out_odd[j])
                gla_s.append(s)
            x = x + out
            x = x + cross_attention(rms_norm(x, g_xattn[i]), mem_k[i], mem_v[i], w_xq[i], w_xo[i])
            h = rms_norm(x, g_ffn[i])
            if i % 2 == 0:
                ffn = swiglu(h, w_ffn_gu[j], w_ffn_dn[j])
            else:
                ffn = moe_swiglu(h.reshape(-1, D_MODEL), w_router[j], w_moe_gu[j], w_moe_dn[j]).reshape(h.shape)
            x = x + ffn
        return rms_norm(x, g_final), jnp.stack(k_rows), jnp.stack(v_rows), jnp.stack(ret_s), jnp.stack(gla_s)

    t_p = x_prompt.shape[1]
    pos_p = jnp.arange(t_p, dtype=jnp.int32)
    mem_kv_p = [memory_kv(mem_prompt, g_mem[i], w_xkv[i]) for i in range(DEPTH)]
    mem_k_p = [kv[0] for kv in mem_kv_p]
    mem_v_p = [kv[1] for kv in mem_kv_p]
    y_prompt, dk_p, dv_p, ret_p, gla_p = trunk(x_prompt, pos_p, pos_p, mem_k_p, mem_v_p, None, None, None, None)

    t_s = x_sample.shape[1]
    past = cache_diff_k.shape[2]
    pos_s = past + jnp.arange(t_s, dtype=jnp.int32)
    key_pos_s = jnp.arange(past + t_s, dtype=jnp.int32)
    mem_k_s = [cache_mem_k[i] for i in range(DEPTH)]
    mem_v_s = [cache_mem_v[i] for i in range(DEPTH)]
    y_sample, dk_s, dv_s, ret_s, gla_s = trunk(x_sample, pos_s, key_pos_s, mem_k_s, mem_v_s,
                                               state_ret, state_gla, cache_diff_k, cache_diff_v)

    return (y_prompt, y_sample, dk_p, dv_p, ret_p, gla_p, jnp.stack(mem_k_p), jnp.stack(mem_v_p),
            dk_s, dv_s, ret_s, gla_s)
```

```python
import functools
import math

import jax
import jax.numpy as jnp
import numpy as np
from jax import lax
from jax.experimental import pallas as pl
from jax.experimental.pallas import tpu as pltpu

F32 = jnp.float32
BF16 = jnp.bfloat16

D_MODEL = 2048
CHUNK = 64
N_MEM = 256
EPS = 1e-6
HA, DA, DVA = 4, 256, 256
ROPE_BASE = 10000.0
HB, DB, DVB = 4, 128, 256
HC, DKC, DVC = 4, 256, 512
GATE_RANK = 16
GATE_TAU = 16.0
HX, DX = 4, 512
D_FF = 5632
N_EXPERTS = 8
TOP_K = 2
NEG_BIG = -1e30

VMEM_LIMIT_BYTES = 56 * 1024 * 1024
LANES = 128

ROW_TILE = 512
FF_TILE = 512
COMBINE_TILE = 256
ATTN_TILE = 256
RET_CHUNK = 256
GLA_BLOCK = 512


def _params(*sem):
    return pltpu.CompilerParams(dimension_semantics=sem, vmem_limit_bytes=VMEM_LIMIT_BYTES)


def _rms_bf16(x, g):
    ms = jnp.mean(x * x, axis=-1, keepdims=True)
    return (x * lax.rsqrt(ms + EPS) * g).astype(BF16)


def _silu(x):
    return x / (1.0 + jnp.exp(-x))


def _dot(a, b):
    return jnp.dot(a, b, preferred_element_type=F32)


def _dot_nt(a, b):
    return lax.dot_general(a, b, (((1,), (1,)), ((), ())), preferred_element_type=F32)


def _dot_tn(a, b):
    return lax.dot_general(a, b, (((0,), (0,)), ((), ())), preferred_element_type=F32)


def _norm_mm_kernel(x_ref, g_ref, w_ref, o_ref, h_ref):
    @pl.when(pl.program_id(1) == 0)
    def _():
        h_ref[...] = _rms_bf16(x_ref[...], g_ref[...])

    o_ref[...] = _dot(h_ref[...], w_ref[...]).astype(o_ref.dtype)


def norm_matmul(x, g, w, *, col0, ncols, tn, out_dtype, tm=ROW_TILE):
    m, d = x.shape
    assert m % tm == 0 and ncols % tn == 0 and col0 % tn == 0
    cb0 = col0 // tn
    return pl.pallas_call(
        _norm_mm_kernel,
        out_shape=jax.ShapeDtypeStruct((m, ncols), out_dtype),
        grid=(m // tm, ncols // tn),
        in_specs=[
            pl.BlockSpec((tm, d), lambda i, j: (i, 0)),
            pl.BlockSpec((1, d), lambda i, j: (0, 0)),
            pl.BlockSpec((d, tn), lambda i, j: (0, cb0 + j)),
        ],
        out_specs=pl.BlockSpec((tm, tn), lambda i, j: (i, j)),
        scratch_shapes=[pltpu.VMEM((tm, d), BF16)],
        compiler_params=_params("parallel", "arbitrary"),
        name="norm_matmul",
    )(x, g, w)


def _mm_res_kernel(*refs, n_lhs, n_head_blocks):
    head_refs = refs[:n_lhs]
    tail_refs = refs[n_lhs:2 * n_lhs]
    w_refs = refs[2 * n_lhs:3 * n_lhs]
    r_ref = refs[3 * n_lhs]
    o_ref = refs[3 * n_lhs + 1]

    def compute(a_refs):
        acc = r_ref[...]
        for a_ref, w_ref in zip(a_refs, w_refs):
            acc = acc + _dot(a_ref[...], w_ref[...])
        o_ref[...] = acc

    is_head = pl.program_id(0) < n_head_blocks

    @pl.when(is_head)
    def _():
        compute(head_refs)

    @pl.when(jnp.logical_not(is_head))
    def _():
        compute(tail_refs)


def matmul_residual(lhs_pairs, w, res, *, tn=1024, tm=ROW_TILE):
    m, n = res.shape
    n_lhs = len(lhs_pairs)
    nhb = lhs_pairs[0][0].shape[0] // tm
    in_specs = []
    for head, tail in lhs_pairs:
        assert head.shape[0] == nhb * tm and tail.shape[0] == tm and (nhb + 1) * tm == m
        in_specs.append(pl.BlockSpec((tm, head.shape[1]), lambda i, j: (jnp.minimum(i, nhb - 1), 0)))
    for head, tail in lhs_pairs:
        in_specs.append(pl.BlockSpec((tm, tail.shape[1]), lambda i, j: (0, 0)))
    row = 0
    for head, _ in lhs_pairs:
        kk = head.shape[1]
        assert row % kk == 0
        rb = row // kk
        in_specs.append(pl.BlockSpec((kk, tn), lambda i, j, rb=rb: (rb, j)))
        row += kk
    in_specs.append(pl.BlockSpec((tm, tn), lambda i, j: (i, j)))
    heads = [p[0] for p in lhs_pairs]
    tails = [p[1] for p in lhs_pairs]
    return pl.pallas_call(
        functools.partial(_mm_res_kernel, n_lhs=n_lhs, n_head_blocks=nhb),
        out_shape=jax.ShapeDtypeStruct((m, n), F32),
        grid=(m // tm, n // tn),
        in_specs=in_specs,
        out_specs=pl.BlockSpec((tm, tn), lambda i, j: (i, j)),
        compiler_params=_params("parallel", "arbitrary"),
        name="matmul_residual",
    )(*heads, *tails, *([w] * n_lhs), res)


def _ffn_kernel(be_ref, nv_ref, x_ref, g_ref, wg_ref, wu_ref, wd_ref, o_ref, h_ref, *, residual):
    i = pl.program_id(0)
    f = pl.program_id(1)
    valid = i < nv_ref[0]

    @pl.when(jnp.logical_and(valid, f == 0))
    def _():
        h_ref[...] = _rms_bf16(x_ref[...], g_ref[...])

    @pl.when(jnp.logical_and(jnp.logical_not(valid), f == 0))
    def _():
        o_ref[...] = jnp.zeros_like(o_ref)

    @pl.when(valid)
    def _():
        h = h_ref[...]
        a = _dot(h, wg_ref[0])
        u = _dot(h, wu_ref[0])
        act = (_silu(a) * u).astype(BF16)
        part = _dot(act, wd_ref[0])

        @pl.when(f == 0)
        def _():
            if residual:
                o_ref[...] = x_ref[...] + part
            else:
                o_ref[...] = part

        @pl.when(f > 0)
        def _():
            o_ref[...] += part


def swiglu_blocks(x, g, w_gu, w_dn, block_expert, n_valid, *, residual, tm=ROW_TILE, tf=FF_TILE):
    m, d = x.shape
    ff = w_dn.shape[1]
    nf = ff // tf
    nb = m // tm

    def _x_map(i, f, be, nv):
        return (jnp.minimum(i, nv[0] - 1), 0)

    def _f_eff(i, f, nv):
        return jnp.where(i < nv[0], f, nf - 1)

    return pl.pallas_call(
        functools.partial(_ffn_kernel, residual=residual),
        out_shape=jax.ShapeDtypeStruct((m, d), F32),
        grid_spec=pltpu.PrefetchScalarGridSpec(
            num_scalar_prefetch=2,
            grid=(nb, nf),
            in_specs=[
                pl.BlockSpec((tm, d), _x_map),
                pl.BlockSpec((1, d), lambda i, f, be, nv: (0, 0)),
                pl.BlockSpec((1, d, tf), lambda i, f, be, nv: (be[i], 0, _f_eff(i, f, nv))),
                pl.BlockSpec((1, d, tf), lambda i, f, be, nv: (be[i], 0, nf + _f_eff(i, f, nv))),
                pl.BlockSpec((1, tf, d), lambda i, f, be, nv: (be[i], _f_eff(i, f, nv), 0)),
            ],
            out_specs=pl.BlockSpec((tm, d), lambda i, f, be, nv: (i, 0)),
            scratch_shapes=[pltpu.VMEM((tm, d), BF16)],
        ),
        compiler_params=_params("parallel", "arbitrary"),
        name="swiglu_blocks",
    )(block_expert, n_valid, x, g, w_gu, w_gu, w_dn)


def _router_kernel(x_ref, g_ref, w_ref, o_ref):
    h = _rms_bf16(x_ref[...], g_ref[...])
    logits = _dot(h, w_ref[...])
    lane = lax.broadcasted_iota(jnp.int32, logits.shape, 1).astype(F32)
    l1 = jnp.where(lane < N_EXPERTS, logits, NEG_BIG)
    m1 = jnp.max(l1, axis=-1, keepdims=True)
    i1 = jnp.min(jnp.where(l1 == m1, lane, float(LANES)), axis=-1, keepdims=True)
    l2 = jnp.where(lane == i1, NEG_BIG, l1)
    m2 = jnp.max(l2, axis=-1, keepdims=True)
    i2 = jnp.min(jnp.where(l2 == m2, lane, float(LANES)), axis=-1, keepdims=True)
    e = jnp.exp(m2 - m1)
    g1 = 1.0 / (1.0 + e)
    g2 = e / (1.0 + e)
    out = jnp.where(lane == 0.0, i1,
                    jnp.where(lane == 1.0, i2,
                              jnp.where(lane == 2.0, g1, jnp.where(lane == 3.0, g2, 0.0))))
    o_ref[...] = out


def router(x, g, w_pad, *, tm=ROW_TILE):
    m, d = x.shape
    return pl.pallas_call(
        _router_kernel,
        out_shape=jax.ShapeDtypeStruct((m, LANES), F32),
        grid=(m // tm,),
        in_specs=[
            pl.BlockSpec((tm, d), lambda i: (i, 0)),
            pl.BlockSpec((1, d), lambda i: (0, 0)),
            pl.BlockSpec((d, LANES), lambda i: (0, 0)),
        ],
        out_specs=pl.BlockSpec((tm, LANES), lambda i: (i, 0)),
        compiler_params=_params("parallel"),
        name="router",
    )(x, g, w_pad)


def _row_copy(src_hbm, dst_vmem, src_row, dst_row, sem):
    return pltpu.make_async_copy(src_hbm.at[pl.ds(src_row, 1)], dst_vmem.at[pl.ds(dst_row, 1)], sem)


def _gather_kernel(idx_ref, x_hbm, o_ref, sem, *, tm):
    base = pl.program_id(0) * tm

    def start(r, c):
        _row_copy(x_hbm, o_ref, idx_ref[base + r], r, sem).start()
        return c

    lax.fori_loop(0, tm, start, 0)

    def wait(r, c):
        _row_copy(x_hbm, o_ref, 0, r, sem).wait()
        return c

    lax.fori_loop(0, tm, wait, 0)


def gather_rows(x, row_idx, *, tm=ROW_TILE):
    r_total = row_idx.shape[0]
    d = x.shape[1]
    return pl.pallas_call(
        functools.partial(_gather_kernel, tm=tm),
        out_shape=jax.ShapeDtypeStruct((r_total, d), F32),
        grid_spec=pltpu.PrefetchScalarGridSpec(
            num_scalar_prefetch=1,
            grid=(r_total // tm,),
            in_specs=[pl.BlockSpec(memory_space=pl.ANY)],
            out_specs=pl.BlockSpec((tm, d), lambda i, idx: (i, 0)),
            scratch_shapes=[pltpu.SemaphoreType.DMA],
        ),
        compiler_params=_params("arbitrary"),
        name="moe_gather",
    )(row_idx, x)


def _combine_kernel(d0_ref, d1_ref, x_ref, r_ref, g_ref, y_hbm, o_ref, buf, sem, *, tm, row0):
    base = row0 + pl.program_id(0) * tm

    def start(r, c):
        _row_copy(y_hbm, buf.at[0], d0_ref[base + r], r, sem).start()
        _row_copy(y_hbm, buf.at[1], d1_ref[base + r], r, sem).start()
        return c

    lax.fori_loop(0, tm, start, 0)

    def wait(r, c):
        _row_copy(y_hbm, buf.at[0], 0, r, sem).wait()
        _row_copy(y_hbm, buf.at[1], 0, r, sem).wait()
        return c

    lax.fori_loop(0, tm, wait, 0)
    rt = r_ref[...]
    g0 = rt[:, 2:3]
    g1 = rt[:, 3:4]
    x = x_ref[...] + (buf[0] * g0 + buf[1] * g1)
    ms = jnp.mean(x * x, axis=-1, keepdims=True)
    o_ref[...] = x * lax.rsqrt(ms + EPS) * g_ref[...]


def moe_combine_norm(x, route, g, y_rows, d0, d1, *, row0, nrows, tm=COMBINE_TILE):
    d = x.shape[1]
    rb0 = row0 // tm
    return pl.pallas_call(
        functools.partial(_combine_kernel, tm=tm, row0=row0),
        out_shape=jax.ShapeDtypeStruct((nrows, d), F32),
        grid_spec=pltpu.PrefetchScalarGridSpec(
            num_scalar_prefetch=2,
            grid=(nrows // tm,),
            in_specs=[
                pl.BlockSpec((tm, d), lambda i, a, b: (rb0 + i, 0)),
                pl.BlockSpec((tm, LANES), lambda i, a, b: (rb0 + i, 0)),
                pl.BlockSpec((1, d), lambda i, a, b: (0, 0)),
                pl.BlockSpec(memory_space=pl.ANY),
            ],
            out_specs=pl.BlockSpec((tm, d), lambda i, a, b: (i, 0)),
            scratch_shapes=[pltpu.VMEM((2, tm, d), F32), pltpu.SemaphoreType.DMA],
        ),
        compiler_params=_params("arbitrary"),
        name="moe_combine",
    )(d0, d1, x, route, g, y_rows)


def _gate_kernel(x_ref, g_ref, wlr_ref, wa2_ref, b_ref, o_ref):
    h = _rms_bf16(x_ref[...], g_ref[...])
    a_lr = _dot(h, wlr_ref[...])
    z = _dot(a_lr.astype(BF16), wa2_ref[...]) + b_ref[...]
    o_ref[...] = (jnp.minimum(z, 0.0) - jnp.log(1.0 + jnp.exp(-jnp.abs(z)))) * (1.0 / GATE_TAU)


def gla_gate(x, g, w_lr_pad, w_a2_pad, b_a, *, tm=ROW_TILE):
    m, d = x.shape
    n = w_a2_pad.shape[1]
    return pl.pallas_call(
        _gate_kernel,
        out_shape=jax.ShapeDtypeStruct((m, n), F32),
        grid=(m // tm,),
        in_specs=[
            pl.BlockSpec((tm, d), lambda i: (i, 0)),
            pl.BlockSpec((1, d), lambda i: (0, 0)),
            pl.BlockSpec((d, LANES), lambda i: (0, 0)),
            pl.BlockSpec((LANES, n), lambda i: (0, 0)),
            pl.BlockSpec((1, n), lambda i: (0, 0)),
        ],
        out_specs=pl.BlockSpec((tm, n), lambda i: (i, 0)),
        compiler_params=_params("parallel"),
        name="gla_gate",
    )(x, g, w_lr_pad, w_a2_pad, b_a)


def _rotary(x, cos, sin):
    half = x.shape[-1] // 2
    x1, x2 = x[:, :half], x[:, half:]
    return jnp.concatenate([x1 * cos - x2 * sin, x1 * sin + x2 * cos], axis=-1)


def _retention_kernel(lg_ref, *refs, L, has_state):
    if has_state:
        q_ref, k_ref, v_ref, g_ref, cos_ref, sin_ref, s0_ref, o_ref, sout_ref, s_ref = refs
    else:
        q_ref, k_ref, v_ref, g_ref, cos_ref, sin_ref, o_ref, sout_ref, s_ref = refs
    h = pl.program_id(1)
    c = pl.program_id(2)

    @pl.when(c == 0)
    def _():
        if has_state:
            s_ref[...] = s0_ref[0, 0]
        else:
            s_ref[...] = jnp.zeros_like(s_ref)

    lg = lg_ref[h]
    cos = cos_ref[...]
    sin = sin_ref[...]
    qr = _rotary(q_ref[...], cos, sin)
    kr = _rotary(k_ref[...], cos, sin) * (DA ** -0.5)
    vb = v_ref[...].astype(BF16)
    n_col = lax.broadcasted_iota(jnp.int32, (L, 1), 0).astype(F32)
    n_row = lax.broadcasted_iota(jnp.int32, (1, L), 1).astype(F32)
    diff = n_col - n_row
    decay = jnp.where(diff >= 0.0, jnp.exp(jnp.maximum(diff, 0.0) * lg), 0.0)
    qb = qr.astype(BF16)
    scores = _dot_nt(qb, kr.astype(BF16)) * decay
    inner = _dot(scores.astype(BF16), vb)
    state = s_ref[...]
    cross = _dot(qb, state.astype(BF16)) * jnp.exp((n_col + 1.0) * lg)
    k_dec = (kr * jnp.exp((L - 1.0 - n_col) * lg)).astype(BF16)
    new_state = state * jnp.exp(jnp.zeros((1, 1), F32) + L * lg) + _dot_tn(k_dec, vb)
    s_ref[...] = new_state
    ret = inner + cross
    ret = ret - jnp.mean(ret, axis=-1, keepdims=True)
    ret = ret * lax.rsqrt(jnp.mean(ret * ret, axis=-1, keepdims=True) + EPS)
    o_ref[...] = (_silu(g_ref[...]) * ret).astype(o_ref.dtype)

    @pl.when(c == pl.num_programs(2) - 1)
    def _():
        sout_ref[0, 0] = new_state


def retention(proj, cos, sin, log_gamma, *, nbatch, seq, row0, pos0, L, state0=None):
    nc = seq // L
    rb0 = row0 // L
    pb0 = pos0 // L
    has_state = state0 is not None

    def col(off):
        return lambda b, h, c, lg: (rb0 + b * nc + c, off + h)

    in_specs = [
        pl.BlockSpec((L, DA), col(0)),
        pl.BlockSpec((L, DA), col(HA)),
        pl.BlockSpec((L, DVA), col(2 * HA)),
        pl.BlockSpec((L, DVA), col(3 * HA)),
        pl.BlockSpec((L, DA // 2), lambda b, h, c, lg: (pb0 + c, 0)),
        pl.BlockSpec((L, DA // 2), lambda b, h, c, lg: (pb0 + c, 0)),
    ]
    args = [proj, proj, proj, proj, cos, sin]
    if has_state:
        in_specs.append(pl.BlockSpec((1, 1, DA, DVA), lambda b, h, c, lg: (b, h, 0, 0)))
        args.append(state0)
    return pl.pallas_call(
        functools.partial(_retention_kernel, L=L, has_state=has_state),
        out_shape=(jax.ShapeDtypeStruct((nbatch * seq, HA * DVA), BF16),
                   jax.ShapeDtypeStruct((nbatch, HA, DA, DVA), F32)),
        grid_spec=pltpu.PrefetchScalarGridSpec(
            num_scalar_prefetch=1,
            grid=(nbatch, HA, nc),
            in_specs=in_specs,
            out_specs=(pl.BlockSpec((L, DVA), lambda b, h, c, lg: (b * nc + c, h)),
                       pl.BlockSpec((1, 1, DA, DVA), lambda b, h, c, lg: (b, h, 0, 0))),
            scratch_shapes=[pltpu.VMEM((DA, DVA), F32)],
        ),
        compiler_params=_params("parallel", "parallel", "arbitrary"),
        name="retention",
    )(log_gamma, *args)


def _lambda_value(l_ref, lam_init):
    lv = l_ref[...]
    a = jnp.sum(lv[0:1] * lv[1:2], axis=-1, keepdims=True)
    b = jnp.sum(lv[2:3] * lv[3:4], axis=-1, keepdims=True)
    return jnp.exp(a) - jnp.exp(b) + lam_init


def _head_norm_scale(o, scale):
    return o * lax.rsqrt(jnp.mean(o * o, axis=-1, keepdims=True) + EPS) * scale


def _dattn_prompt_kernel(q_ref, k_ref, v_ref, l_ref, o_ref, m_ref, s_ref, acc_ref, *, tq, lam_init):
    i = pl.program_id(2)
    q = q_ref[...] * (DB ** -0.5)
    qc = (q[:, :DB].astype(BF16), q[:, DB:].astype(BF16))
    m_ref[...] = jnp.full_like(m_ref, NEG_BIG)
    s_ref[...] = jnp.zeros_like(s_ref)
    acc_ref[...] = jnp.zeros_like(acc_ref)

    def block(kb, masked):
        start = pl.multiple_of(kb * tq, tq)
        kblk = k_ref[pl.ds(start, tq), :].astype(BF16)
        vblk = v_ref[pl.ds(start, tq), :].astype(BF16)
        if masked:
            r_chunk = lax.broadcasted_iota(jnp.int32, (tq, tq), 0) // CHUNK
            c_chunk = lax.broadcasted_iota(jnp.int32, (tq, tq), 1) // CHUNK
            mask = c_chunk <= r_chunk
        for c in range(2):
            s = _dot_nt(qc[c], kblk[:, c * DB:(c + 1) * DB])
            if masked:
                s = jnp.where(mask, s, NEG_BIG)
            m_old = m_ref[c]
            m_new = jnp.maximum(m_old, jnp.max(s, axis=-1, keepdims=True))
            p = jnp.exp(s - m_new)
            alpha = jnp.exp(m_old - m_new)
            s_ref[c] = alpha * s_ref[c] + jnp.sum(p, axis=-1, keepdims=True)
            acc_ref[c] = alpha * acc_ref[c] + _dot(p.astype(BF16), vblk)
            m_ref[c] = m_new

    def body(kb, carry):
        block(kb, False)
        return carry

    lax.fori_loop(0, i, body, 0)
    block(i, True)
    lam = _lambda_value(l_ref, lam_init)
    o = acc_ref[0] / s_ref[0] - lam * (acc_ref[1] / s_ref[1])
    o_ref[...] = _head_norm_scale(o, 1.0 - lam_init).astype(o_ref.dtype)


def diff_attention_prompt(proj, lam_params, *, nbatch, seq, lam_init, tq=ATTN_TILE):
    nq = seq // tq
    assert tq % CHUNK == 0
    return pl.pallas_call(
        functools.partial(_dattn_prompt_kernel, tq=tq, lam_init=lam_init),
        out_shape=jax.ShapeDtypeStruct((nbatch * seq, HB * DVB), BF16),
        grid=(nbatch, HB, nq),
        in_specs=[
            pl.BlockSpec((tq, 2 * DB), lambda b, h, i: (b * nq + i, 4 * HA + h)),
            pl.BlockSpec((seq, 2 * DB), lambda b, h, i: (b, 4 * HA + HB + h)),
            pl.BlockSpec((seq, DVB), lambda b, h, i: (b, 4 * HA + 2 * HB + h)),
            pl.BlockSpec((4, DB), lambda b, h, i: (0, 0)),
        ],
        out_specs=pl.BlockSpec((tq, DVB), lambda b, h, i: (b * nq + i, h)),
        scratch_shapes=[pltpu.VMEM((2, tq, 1), F32), pltpu.VMEM((2, tq, 1), F32),
                        pltpu.VMEM((2, tq, DVB), F32)],
        compiler_params=_params("parallel", "parallel", "arbitrary"),
        name="diff_attention_prompt",
    )(proj, proj, proj, lam_params)


def _dattn_sample_kernel(q_ref, kn_ref, vn_ref, kc_ref, vc_ref, l_ref, o_ref, *, lam_init):
    q = q_ref[...] * (DB ** -0.5)
    kn = kn_ref[...].astype(BF16)
    kc = kc_ref[0].astype(BF16)
    lam = _lambda_value(l_ref, lam_init)
    w_c = None
    w_n = None
    for c in range(2):
        qc = q[:, c * DB:(c + 1) * DB].astype(BF16)
        s_c = _dot_nt(qc, kc[:, c * DB:(c + 1) * DB])
        s_n = _dot_nt(qc, kn[:, c * DB:(c + 1) * DB])
        m = jnp.maximum(jnp.max(s_c, axis=-1, keepdims=True), jnp.max(s_n, axis=-1, keepdims=True))
        p_c = jnp.exp(s_c - m)
        p_n = jnp.exp(s_n - m)
        inv = 1.0 / (jnp.sum(p_c, axis=-1, keepdims=True) + jnp.sum(p_n, axis=-1, keepdims=True))
        if c == 0:
            w_c, w_n = p_c * inv, p_n * inv
        else:
            w_c, w_n = w_c - lam * (p_c * inv), w_n - lam * (p_n * inv)
    o = _dot(w_c.astype(BF16), vc_ref[0].astype(BF16)) + _dot(w_n.astype(BF16), vn_ref[...].astype(BF16))
    o_ref[...] = _head_norm_scale(o, 1.0 - lam_init).astype(o_ref.dtype)


def diff_attention_sample(proj, cache_k, cache_v, lam_params, *, nbatch, seq, row0, lam_init):
    past = cache_k.shape[1]
    assert seq == CHUNK and past % CHUNK == 0 and row0 % seq == 0
    rb0 = row0 // seq
    return pl.pallas_call(
        functools.partial(_dattn_sample_kernel, lam_init=lam_init),
        out_shape=jax.ShapeDtypeStruct((nbatch * seq, HB * DVB), BF16),
        grid=(nbatch, HB),
        in_specs=[
            pl.BlockSpec((seq, 2 * DB), lambda b, h: (rb0 + b, 4 * HA + h)),
            pl.BlockSpec((seq, 2 * DB), lambda b, h: (rb0 + b, 4 * HA + HB + h)),
            pl.BlockSpec((seq, DVB), lambda b, h: (rb0 + b, 4 * HA + 2 * HB + h)),
            pl.BlockSpec((1, past, 2 * DB), lambda b, h: (b, 0, h)),
            pl.BlockSpec((1, past, DVB), lambda b, h: (b, 0, h)),
            pl.BlockSpec((4, DB), lambda b, h: (0, 0)),
        ],
        out_specs=pl.BlockSpec((seq, DVB), lambda b, h: (b, h)),
        compiler_params=_params("parallel", "parallel"),
        name="diff_attention_sample",
    )(proj, proj, proj, cache_k, cache_v, lam_params)


def _split3_bf16(x):
    hi = x.astype(BF16)
    r1 = x - hi.astype(F32)
    mid = r1.astype(BF16)
    lo = (r1 - mid.astype(F32)).astype(BF16)
    return hi, mid, lo


def _gla_kernel(*refs, nsub, has_state):
    if has_state:
        q_ref, k_ref, v_ref, g_ref, a_ref, s0_ref, o_ref, sout_ref, s_ref = refs
    else:
        q_ref, k_ref, v_ref, g_ref, a_ref, o_ref, sout_ref, s_ref = refs
    c = pl.program_id(2)
    L = CHUNK

    @pl.when(c == 0)
    def _():
        if has_state:
            s_ref[...] = s0_ref[0, 0].T
        else:
            s_ref[...] = jnp.zeros_like(s_ref)

    row = lax.broadcasted_iota(jnp.int32, (L, L), 0)
    colm = lax.broadcasted_iota(jnp.int32, (L, L), 1)
    causal = colm <= row
    tril = jnp.where(causal, 1.0, 0.0).astype(BF16)

    def chunk(j, carry):
        r0 = pl.multiple_of(j * L, L)
        q = q_ref[pl.ds(r0, L), :] * (DKC ** -0.5)
        k = k_ref[pl.ds(r0, L), :]
        vb = v_ref[pl.ds(r0, L), :].astype(BF16)
        la = a_ref[pl.ds(r0, L), :]
        hi, mid, lo = _split3_bf16(la)
        b = _dot(tril, hi) + _dot(tril, mid) + _dot(tril, lo)
        ref = b[L // 2:L // 2 + 1, :]
        b_last = b[L - 1:L, :]
        q_in = (q * jnp.exp(b - ref)).astype(BF16)
        k_in = (k * jnp.exp(ref - b)).astype(BF16)
        att = jnp.where(causal, _dot_nt(q_in, k_in), 0.0)
        inner = _dot(att.astype(BF16), vb)
        state_t = s_ref[...]
        cross = _dot_nt((q * jnp.exp(b)).astype(BF16), state_t.astype(BF16))
        k_out = (k * jnp.exp(b_last - b)).astype(BF16)
        s_ref[...] = jnp.exp(b_last) * state_t + _dot_tn(vb, k_out)
        o = inner + cross
        o = o * lax.rsqrt(jnp.mean(o * o, axis=-1, keepdims=True) + EPS)
        o_ref[pl.ds(r0, L), :] = (_silu(g_ref[pl.ds(r0, L), :]) * o).astype(o_ref.dtype)
        return carry

    lax.fori_loop(0, nsub, chunk, 0)

    @pl.when(c == pl.num_programs(2) - 1)
    def _():
        sout_ref[0, 0] = s_ref[...].T


def gla(proj, log_a, *, nbatch, seq, row0, rows_per_step, state0=None):
    lb = rows_per_step
    nc = seq // lb
    rb0 = row0 // lb
    has_state = state0 is not None
    kq = HC
    v_off = 2 * HC * DKC // DVC

    def rows(b, c):
        return rb0 + b * nc + c

    in_specs = [
        pl.BlockSpec((lb, DKC), lambda b, h, c: (rows(b, c), h)),
        pl.BlockSpec((lb, DKC), lambda b, h, c: (rows(b, c), kq + h)),
        pl.BlockSpec((lb, DVC), lambda b, h, c: (rows(b, c), v_off + h)),
        pl.BlockSpec((lb, DVC), lambda b, h, c: (rows(b, c), v_off + HC + h)),
        pl.BlockSpec((lb, DKC), lambda b, h, c: (rows(b, c), h)),
    ]
    args = [proj, proj, proj, proj, log_a]
    if has_state:
        in_specs.append(pl.BlockSpec((1, 1, DKC, DVC), lambda b, h, c: (b, h, 0, 0)))
        args.append(state0)
    return pl.pallas_call(
        functools.partial(_gla_kernel, nsub=lb // CHUNK, has_state=has_state),
        out_shape=(jax.ShapeDtypeStruct((nbatch * seq, HC * DVC), BF16),
                   jax.ShapeDtypeStruct((nbatch, HC, DKC, DVC), F32)),
        grid=(nbatch, HC, nc),
        in_specs=in_specs,
        out_specs=(pl.BlockSpec((lb, DVC), lambda b, h, c: (b * nc + c, h)),
                   pl.BlockSpec((1, 1, DKC, DVC), lambda b, h, c: (b, h, 0, 0))),
        scratch_shapes=[pltpu.VMEM((DVC, DKC), F32)],
        compiler_params=_params("parallel", "parallel", "arbitrary"),
        name="gla",
    )(*args)


def _xattn_kernel(q_ref, k_ref, v_ref, o_ref):
    for h in range(HX):
        sl = slice(h * DX, (h + 1) * DX)
        q = q_ref[:, sl]
        k = k_ref[0, :, sl].astype(BF16)
        v = v_ref[0, :, sl].astype(BF16)
        s = _dot_nt(q, k) * (DX ** -0.5)
        m = jnp.max(s, axis=-1, keepdims=True)
        p = jnp.exp(s - m)
        p = p / jnp.sum(p, axis=-1, keepdims=True)
        o_ref[:, sl] = _dot(p.astype(BF16), v).astype(o_ref.dtype)


def cross_attention(q, mem_k, mem_v, *, nbatch, seq, row0, tq):
    d = q.shape[1]
    nq = seq // tq
    rb0 = row0 // tq
    return pl.pallas_call(
        _xattn_kernel,
        out_shape=jax.ShapeDtypeStruct((nbatch * seq, d), BF16),
        grid=(nbatch, nq),
        in_specs=[
            pl.BlockSpec((tq, d), lambda b, i: (rb0 + b * nq + i, 0)),
            pl.BlockSpec((1, N_MEM, d), lambda b, i: (b, 0, 0)),
            pl.BlockSpec((1, N_MEM, d), lambda b, i: (b, 0, 0)),
        ],
        out_specs=pl.BlockSpec((tq, d), lambda b, i: (b * nq + i, 0)),
        compiler_params=_params("parallel", "arbitrary"),
        name="cross_attention",
    )(q, mem_k, mem_v)


def _routing_tables(route, tm):
    n_tok = route.shape[0]
    n_rows = n_tok * TOP_K
    nb = (n_rows + N_EXPERTS * (tm - 1) + tm - 1) // tm
    flat_e = route[:, :TOP_K].astype(jnp.int32).reshape(-1)
    onehot = (flat_e[:, None] == jnp.arange(N_EXPERTS, dtype=jnp.int32)[None, :]).astype(jnp.int32)
    csum = jnp.cumsum(onehot, axis=0)
    rank = jnp.sum(csum * onehot, axis=1) - 1
    counts = csum[-1]
    padded = (counts + tm - 1) // tm * tm
    pad_end = jnp.cumsum(padded)
    pad_start = pad_end - padded
    dest = pad_start[flat_e] + rank
    tok = jnp.arange(n_rows, dtype=jnp.int32) // TOP_K
    row_tok = jnp.zeros((nb * tm,), jnp.int32).at[dest].set(tok)
    n_valid = (pad_end[-1] // tm).astype(jnp.int32)
    blk = jnp.minimum(jnp.arange(nb, dtype=jnp.int32), n_valid - 1) * tm
    block_expert = jnp.minimum(jnp.searchsorted(pad_end, blk, side='right'), N_EXPERTS - 1).astype(jnp.int32)
    dest2 = dest.reshape(n_tok, TOP_K).astype(jnp.int32)
    return row_tok, block_expert, n_valid.reshape(1), dest2[:, 0], dest2[:, 1]


def kernel(x_prompt, x_sample, mem_prompt, cache_diff_k, cache_diff_v, state_ret, state_gla, cache_mem_k, cache_mem_v, g_mix, g_xattn, g_mem, g_ffn, g_final, w_in_even, w_out_even, lambda_q1, lambda_k1, lambda_q2, lambda_k2, w_in_odd, w_gate_lr, b_gate_lr, w_out_odd, w_xq, w_xkv, w_xo, w_ffn_gu, w_ffn_dn, w_router, w_moe_gu, w_moe_dn):
    d = D_MODEL
    bp, tp, _ = x_prompt.shape
    bs, ts, _ = x_sample.shape
    past = cache_diff_k.shape[2]
    np_tok = bp * tp
    ns_tok = bs * ts
    n_tok = np_tok + ns_tok
    depth = g_mix.shape[0]

    x = jnp.concatenate([x_prompt.reshape(np_tok, d), x_sample.reshape(ns_tok, d)], axis=0)

    half = DA // 2
    inv = 1.0 / (ROPE_BASE ** jnp.linspace(0.0, 1.0, half, dtype=F32))
    pos = jnp.arange(max(tp, past + ts), dtype=jnp.int32).astype(F32)
    ang = pos[:, None] * inv[None, :]
    cos_t, sin_t = jnp.cos(ang), jnp.sin(ang)
    log_gamma = jnp.log1p(-jnp.power(2.0, -5.0 - jnp.arange(HA, dtype=F32)))

    ones_blocks = jnp.zeros((n_tok // ROW_TILE,), jnp.int32)
    all_valid = jnp.full((1,), n_tok // ROW_TILE, jnp.int32)

    ret_p, ret_s, gla_p, gla_s = [], [], [], []
    dk_p, dv_p, dk_s, dv_s = [], [], [], []
    mk_p, mv_p = [], []
    y_rows = route = d0 = d1 = None

    for i in range(depth):
        j = i // 2
        g_i = g_mix[i].reshape(1, d)
        if i % 2 == 0:
            w_in = w_in_even[j].astype(BF16)
            proj = norm_matmul(x, g_i, w_in, col0=0, ncols=w_in.shape[1], tn=1024, out_dtype=F32)
            lam_init = 0.8 - 0.6 * math.exp(-0.3 * i)
            lam_params = jnp.stack([lambda_q1[j], lambda_k1[j], lambda_q2[j], lambda_k2[j]]).astype(F32)
            a_p, s_p = retention(proj, cos_t, sin_t, log_gamma, nbatch=bp, seq=tp, row0=0, pos0=0, L=RET_CHUNK)
            a_s, s_s = retention(proj, cos_t, sin_t, log_gamma, nbatch=bs, seq=ts, row0=np_tok, pos0=past,
                                 L=CHUNK, state0=state_ret[j])
            ret_p.append(s_p)
            ret_s.append(s_s)
            b_p = diff_attention_prompt(proj, lam_params, nbatch=bp, seq=tp, lam_init=lam_init)
            b_s = diff_attention_sample(
                proj, cache_diff_k[j].reshape(bs, past, HB * 2 * DB), cache_diff_v[j].reshape(bs, past, HB * DVB),
                lam_params, nbatch=bs, seq=ts, row0=np_tok, lam_init=lam_init)
            c_dk = 4 * HA * DA + HB * 2 * DB
            c_dv = c_dk + HB * 2 * DB
            dk = proj[:, c_dk:c_dv]
            dv = proj[:, c_dv:c_dv + HB * DVB]
            dk_p.append(dk[:np_tok].reshape(bp, tp, HB, 2, DB))
            dk_s.append(dk[np_tok:].reshape(bs, ts, HB, 2, DB))
            dv_p.append(dv[:np_tok].reshape(bp, tp, HB, DVB))
            dv_s.append(dv[np_tok:].reshape(bs, ts, HB, DVB))
            x = matmul_residual([(a_p, a_s), (b_p, b_s)], w_out_even[j].astype(BF16), x)
        else:
            n_main = 2 * HC * DKC + 2 * HC * DVC
            w_in = w_in_odd[j]
            proj = norm_matmul(x, g_i, w_in[:, :n_main].astype(BF16), col0=0, ncols=n_main, tn=1024, out_dtype=F32)
            w_lr = jnp.zeros((d, LANES), BF16).at[:, :GATE_RANK].set(w_in[:, n_main:].astype(BF16))
            w_a2 = jnp.zeros((LANES, HC * DKC), BF16).at[:GATE_RANK].set(w_gate_lr[j].astype(BF16))
            log_a = gla_gate(x, g_i, w_lr, w_a2, b_gate_lr[j].reshape(1, -1).astype(F32))
            o_p, s_p = gla(proj, log_a, nbatch=bp, seq=tp, row0=0, rows_per_step=GLA_BLOCK)
            o_s, s_s = gla(proj, log_a, nbatch=bs, seq=ts, row0=np_tok, rows_per_step=ts, state0=state_gla[j])
            gla_p.append(s_p)
            gla_s.append(s_s)
            x = matmul_residual([(o_p, o_s)], w_out_odd[j].astype(BF16), x)

        q = norm_matmul(x, g_xattn[i].reshape(1, d), w_xq[i].astype(BF16), col0=0, ncols=HX * DX, tn=1024,
                        out_dtype=BF16)
        w_kv = w_xkv[i].astype(BF16)
        mem2d = mem_prompt.reshape(bp * N_MEM, d)
        g_m = g_mem[i].reshape(1, d)
        mk = norm_matmul(mem2d, g_m, w_kv, col0=0, ncols=HX * DX, tn=1024, out_dtype=F32)
        mv = norm_matmul(mem2d, g_m, w_kv, col0=HX * DX, ncols=HX * DX, tn=1024, out_dtype=F32)
        mk_p.append(mk.reshape(bp, N_MEM, HX, DX))
        mv_p.append(mv.reshape(bp, N_MEM, HX, DX))
        o_p = cross_attention(q, mk.reshape(bp, N_MEM, HX * DX), mv.reshape(bp, N_MEM, HX * DX),
                              nbatch=bp, seq=tp, row0=0, tq=512)
        o_s = cross_attention(q, cache_mem_k[i].reshape(bs, N_MEM, HX * DX),
                              cache_mem_v[i].reshape(bs, N_MEM, HX * DX), nbatch=bs, seq=ts, row0=np_tok, tq=ts)
        x = matmul_residual([(o_p, o_s)], w_xo[i].astype(BF16), x)

        g_f = g_ffn[i].reshape(1, d)
        if i % 2 == 0:
            x = swiglu_blocks(x, g_f, w_ffn_gu[j].astype(BF16)[None], w_ffn_dn[j].astype(BF16)[None],
                              ones_blocks, all_valid, residual=True)
        else:
            w_r = jnp.zeros((d, LANES), BF16).at[:, :N_EXPERTS].set(w_router[j].astype(BF16))
            route = router(x, g_f, w_r)
            row_tok, block_expert, n_valid, d0, d1 = _routing_tables(route, ROW_TILE)
            xs = gather_rows(x, row_tok)
            y_rows = swiglu_blocks(xs, g_f, w_moe_gu[j].astype(BF16), w_moe_dn[j].astype(BF16),
                                   block_expert, n_valid, residual=False)
            if i != depth - 1:
                raise NotImplementedError("MoE layer must be the last layer")

    g_fin = g_final.reshape(1, d)
    y_p = moe_combine_norm(x, route, g_fin, y_rows, d0, d1, row0=0, nrows=np_tok)
    y_s = moe_combine_norm(x, route, g_fin, y_rows, d0, d1, row0=np_tok, nrows=ns_tok)

    return (y_p.reshape(bp, tp, d), y_s.reshape(bs, ts, d),
            jnp.stack(dk_p), jnp.stack(dv_p), jnp.stack(ret_p), jnp.stack(gla_p),
            jnp.stack(mk_p), jnp.stack(mv_p),
            jnp.stack(dk_s), jnp.stack(dv_s), jnp.stack(ret_s), jnp.stack(gla_s))
```

```python
import functools
import math

import jax
import jax.numpy as jnp
import numpy as np
from jax import lax
from jax.experimental import pallas as pl
from jax.experimental.pallas import tpu as pltpu

F32 = jnp.float32
BF16 = jnp.bfloat16

D_MODEL = 2048
CHUNK = 64
N_MEM = 256
EPS = 1e-6
HA, DA, DVA = 4, 256, 256
ROPE_BASE = 10000.0
HB, DB, DVB = 4, 128, 256
HC, DKC, DVC = 4, 256, 512
GATE_RANK = 16
GATE_TAU = 16.0
HX, DX = 4, 512
D_FF = 5632
N_EXPERTS = 8
TOP_K = 2
NEG_BIG = -1e30

VMEM_LIMIT_BYTES = 56 * 1024 * 1024
LANES = 128

ROW_TILE = 512
PROJ_ROW_TILE = 768
MEM_ROW_TILE = 256
FF_TILE = 512
COMBINE_TILE = 256
ATTN_TILE = 256
RET_CHUNK = 256
GLA_BLOCK = 256


def _params(*sem):
    return pltpu.CompilerParams(dimension_semantics=sem, vmem_limit_bytes=VMEM_LIMIT_BYTES)


def _resident_spec(block_shape, index_map):
    return pl.BlockSpec(block_shape, index_map, pipeline_mode=pl.Buffered(1))


def _rms_bf16(x, g):
    ms = jnp.mean(x * x, axis=-1, keepdims=True)
    return (x * lax.rsqrt(ms + EPS) * g).astype(BF16)


def _silu(x):
    return x / (1.0 + jnp.exp(-x))


def _dot(a, b):
    return jnp.dot(a, b, preferred_element_type=F32)


def _dot_nt(a, b):
    return lax.dot_general(a, b, (((1,), (1,)), ((), ())), preferred_element_type=F32)


def _dot_tn(a, b):
    return lax.dot_general(a, b, (((0,), (0,)), ((), ())), preferred_element_type=F32)


def _norm_mm_kernel(x_ref, g_ref, w_ref, o_ref, h_ref):
    @pl.when(pl.program_id(1) == 0)
    def _():
        h_ref[...] = _rms_bf16(x_ref[...], g_ref[...])

    o_ref[...] = _dot(h_ref[...], w_ref[...]).astype(o_ref.dtype)


def norm_matmul(x, g, w, *, col0, ncols, tn, out_dtype, tm=ROW_TILE):
    m, d = x.shape
    assert m % tm == 0 and ncols % tn == 0 and col0 % tn == 0
    cb0 = col0 // tn
    return pl.pallas_call(
        _norm_mm_kernel,
        out_shape=jax.ShapeDtypeStruct((m, ncols), out_dtype),
        grid=(m // tm, ncols // tn),
        in_specs=[
            pl.BlockSpec((tm, d), lambda i, j: (i, 0)),
            pl.BlockSpec((1, d), lambda i, j: (0, 0)),
            pl.BlockSpec((d, tn), lambda i, j: (0, cb0 + j)),
        ],
        out_specs=pl.BlockSpec((tm, tn), lambda i, j: (i, j)),
        scratch_shapes=[pltpu.VMEM((tm, d), BF16)],
        compiler_params=_params("parallel", "arbitrary"),
        name="norm_matmul",
    )(x, g, w)


def _norm_mm_groups_kernel(*refs, n_groups):
    x_ref, g_ref = refs[:2]
    w_refs = refs[2:2 + n_groups]
    o_refs = refs[2 + n_groups:]
    h = _rms_bf16(x_ref[...], g_ref[...])
    outs_per_group = len(o_refs) // n_groups
    for k, w_ref in enumerate(w_refs):
        y = _dot(h, w_ref[...])
        for o_ref in o_refs[k * outs_per_group:(k + 1) * outs_per_group]:
            o_ref[...] = y.astype(o_ref.dtype)


def norm_matmul_groups(x, g, w, *, row0, nrows, col0, group_cols, n_groups, out_dtypes, tm=ROW_TILE):
    d = x.shape[1]
    assert nrows % tm == 0 and row0 % tm == 0 and col0 % group_cols == 0
    rb0 = row0 // tm
    cb0 = col0 // group_cols
    in_specs = [pl.BlockSpec((tm, d), lambda i: (rb0 + i, 0)), pl.BlockSpec((1, d), lambda i: (0, 0))]
    for k in range(n_groups):
        in_specs.append(_resident_spec((d, group_cols), lambda i, k=k: (0, cb0 + k)))
    out_shape, out_specs = [], []
    for k in range(n_groups):
        for dt in out_dtypes:
            out_shape.append(jax.ShapeDtypeStruct((nrows, group_cols), dt))
            out_specs.append(pl.BlockSpec((tm, group_cols), lambda i: (i, 0)))
    return pl.pallas_call(
        functools.partial(_norm_mm_groups_kernel, n_groups=n_groups),
        out_shape=out_shape,
        grid=(nrows // tm,),
        in_specs=in_specs,
        out_specs=out_specs,
        compiler_params=_params("parallel"),
        name="norm_matmul_groups",
    )(x, g, *([w] * n_groups))


def _mm_res_kernel(*refs, n_lhs, n_head_blocks):
    head_refs = refs[:n_lhs]
    tail_refs = refs[n_lhs:2 * n_lhs]
    w_refs = refs[2 * n_lhs:3 * n_lhs]
    r_ref = refs[3 * n_lhs]
    o_ref = refs[3 * n_lhs + 1]

    def compute(a_refs):
        acc = r_ref[...]
        for a_ref, w_ref in zip(a_refs, w_refs):
            acc = acc + _dot(a_ref[...], w_ref[...])
        o_ref[...] = acc

    is_head = pl.program_id(0) < n_head_blocks

    @pl.when(is_head)
    def _():
        compute(head_refs)

    @pl.when(jnp.logical_not(is_head))
    def _():
        compute(tail_refs)


def matmul_residual(lhs_pairs, w, res, *, tm=ROW_TILE):
    m, n = res.shape
    n_lhs = len(lhs_pairs)
    nhb = lhs_pairs[0][0].shape[0] // tm
    in_specs = []
    for head, tail in lhs_pairs:
        assert head.shape[0] == nhb * tm and tail.shape[0] == tm and (nhb + 1) * tm == m
        in_specs.append(pl.BlockSpec((tm, head.shape[1]), lambda i: (jnp.minimum(i, nhb - 1), 0)))
    for head, tail in lhs_pairs:
        in_specs.append(pl.BlockSpec((tm, tail.shape[1]), lambda i: (0, 0)))
    row = 0
    for head, _ in lhs_pairs:
        kk = head.shape[1]
        assert row % kk == 0
        rb = row // kk
        in_specs.append(_resident_spec((kk, n), lambda i, rb=rb: (rb, 0)))
        row += kk
    in_specs.append(pl.BlockSpec((tm, n), lambda i: (i, 0)))
    heads = [p[0] for p in lhs_pairs]
    tails = [p[1] for p in lhs_pairs]
    return pl.pallas_call(
        functools.partial(_mm_res_kernel, n_lhs=n_lhs, n_head_blocks=nhb),
        out_shape=jax.ShapeDtypeStruct((m, n), F32),
        grid=(m // tm,),
        in_specs=in_specs,
        out_specs=pl.BlockSpec((tm, n), lambda i: (i, 0)),
        compiler_params=_params("parallel"),
        name="matmul_residual",
    )(*heads, *tails, *([w] * n_lhs), res)


def _ffn_kernel(be_ref, nv_ref, x_ref, g_ref, wg_ref, wu_ref, wd_ref, o_ref, h_ref, *, residual):
    i = pl.program_id(0)
    f = pl.program_id(1)
    valid = i < nv_ref[0]

    @pl.when(jnp.logical_and(valid, f == 0))
    def _():
        h_ref[...] = _rms_bf16(x_ref[...], g_ref[...])

    @pl.when(jnp.logical_and(jnp.logical_not(valid), f == 0))
    def _():
        o_ref[...] = jnp.zeros_like(o_ref)

    @pl.when(valid)
    def _():
        h = h_ref[...]
        a = _dot(h, wg_ref[0])
        u = _dot(h, wu_ref[0])
        act = (_silu(a) * u).astype(BF16)
        part = _dot(act, wd_ref[0])

        @pl.when(f == 0)
        def _():
            if residual:
                o_ref[...] = x_ref[...] + part
            else:
                o_ref[...] = part

        @pl.when(f > 0)
        def _():
            o_ref[...] += part


def swiglu_blocks(x, g, w_gu, w_dn, block_expert, n_valid, *, residual, tm=ROW_TILE, tf=FF_TILE):
    m, d = x.shape
    ff = w_dn.shape[1]
    nf = ff // tf
    nb = m // tm

    def _x_map(i, f, be, nv):
        return (jnp.minimum(i, nv[0] - 1), 0)

    def _f_eff(i, f, nv):
        return jnp.where(i < nv[0], f, nf - 1)

    return pl.pallas_call(
        functools.partial(_ffn_kernel, residual=residual),
        out_shape=jax.ShapeDtypeStruct((m, d), F32),
        grid_spec=pltpu.PrefetchScalarGridSpec(
            num_scalar_prefetch=2,
            grid=(nb, nf),
            in_specs=[
                pl.BlockSpec((tm, d), _x_map),
                pl.BlockSpec((1, d), lambda i, f, be, nv: (0, 0)),
                pl.BlockSpec((1, d, tf), lambda i, f, be, nv: (be[i], 0, _f_eff(i, f, nv))),
                pl.BlockSpec((1, d, tf), lambda i, f, be, nv: (be[i], 0, nf + _f_eff(i, f, nv))),
                pl.BlockSpec((1, tf, d), lambda i, f, be, nv: (be[i], _f_eff(i, f, nv), 0)),
            ],
            out_specs=pl.BlockSpec((tm, d), lambda i, f, be, nv: (i, 0)),
            scratch_shapes=[pltpu.VMEM((tm, d), BF16)],
        ),
        compiler_params=_params("parallel", "arbitrary"),
        name="swiglu_blocks",
    )(block_expert, n_valid, x, g, w_gu, w_gu, w_dn)


def _router_kernel(x_ref, g_ref, w_ref, o_ref):
    h = _rms_bf16(x_ref[...], g_ref[...])
    logits = _dot(h, w_ref[...])
    lane = lax.broadcasted_iota(jnp.int32, logits.shape, 1).astype(F32)
    l1 = jnp.where(lane < N_EXPERTS, logits, NEG_BIG)
    m1 = jnp.max(l1, axis=-1, keepdims=True)
    i1 = jnp.min(jnp.where(l1 == m1, lane, float(LANES)), axis=-1, keepdims=True)
    l2 = jnp.where(lane == i1, NEG_BIG, l1)
    m2 = jnp.max(l2, axis=-1, keepdims=True)
    i2 = jnp.min(jnp.where(l2 == m2, lane, float(LANES)), axis=-1, keepdims=True)
    e = jnp.exp(m2 - m1)
    g1 = 1.0 / (1.0 + e)
    g2 = e / (1.0 + e)
    out = jnp.where(lane == 0.0, i1,
                    jnp.where(lane == 1.0, i2,
                              jnp.where(lane == 2.0, g1, jnp.where(lane == 3.0, g2, 0.0))))
    o_ref[...] = out


def router(x, g, w_pad, *, tm=ROW_TILE):
    m, d = x.shape
    return pl.pallas_call(
        _router_kernel,
        out_shape=jax.ShapeDtypeStruct((m, LANES), F32),
        grid=(m // tm,),
        in_specs=[
            pl.BlockSpec((tm, d), lambda i: (i, 0)),
            pl.BlockSpec((1, d), lambda i: (0, 0)),
            pl.BlockSpec((d, LANES), lambda i: (0, 0)),
        ],
        out_specs=pl.BlockSpec((tm, LANES), lambda i: (i, 0)),
        compiler_params=_params("parallel"),
        name="router",
    )(x, g, w_pad)


def _row_copy(src_hbm, dst_vmem, src_row, dst_row, sem):
    return pltpu.make_async_copy(src_hbm.at[pl.ds(src_row, 1)], dst_vmem.at[pl.ds(dst_row, 1)], sem)


def _gather_kernel(idx_ref, x_hbm, o_ref, sem, *, tm):
    base = pl.program_id(0) * tm

    def start(r, c):
        _row_copy(x_hbm, o_ref, idx_ref[base + r], r, sem).start()
        return c

    lax.fori_loop(0, tm, start, 0)

    def wait(r, c):
        _row_copy(x_hbm, o_ref, 0, r, sem).wait()
        return c

    lax.fori_loop(0, tm, wait, 0)


def gather_rows(x, row_idx, *, tm=ROW_TILE):
    r_total = row_idx.shape[0]
    d = x.shape[1]
    return pl.pallas_call(
        functools.partial(_gather_kernel, tm=tm),
        out_shape=jax.ShapeDtypeStruct((r_total, d), F32),
        grid_spec=pltpu.PrefetchScalarGridSpec(
            num_scalar_prefetch=1,
            grid=(r_total // tm,),
            in_specs=[pl.BlockSpec(memory_space=pl.ANY)],
            out_specs=pl.BlockSpec((tm, d), lambda i, idx: (i, 0)),
            scratch_shapes=[pltpu.SemaphoreType.DMA],
        ),
        compiler_params=_params("arbitrary"),
        name="moe_gather",
    )(row_idx, x)


def _combine_kernel(d0_ref, d1_ref, x_ref, r_ref, g_ref, y_hbm, o_ref, buf, sem, *, tm, row0):
    base = row0 + pl.program_id(0) * tm

    def start(r, c):
        _row_copy(y_hbm, buf.at[0], d0_ref[base + r], r, sem).start()
        _row_copy(y_hbm, buf.at[1], d1_ref[base + r], r, sem).start()
        return c

    lax.fori_loop(0, tm, start, 0)

    def wait(r, c):
        _row_copy(y_hbm, buf.at[0], 0, r, sem).wait()
        _row_copy(y_hbm, buf.at[1], 0, r, sem).wait()
        return c

    lax.fori_loop(0, tm, wait, 0)
    rt = r_ref[...]
    g0 = rt[:, 2:3]
    g1 = rt[:, 3:4]
    x = x_ref[...] + (buf[0] * g0 + buf[1] * g1)
    ms = jnp.mean(x * x, axis=-1, keepdims=True)
    o_ref[...] = x * lax.rsqrt(ms + EPS) * g_ref[...]


def moe_combine_norm(x, route, g, y_rows, d0, d1, *, row0, nrows, tm=COMBINE_TILE):
    d = x.shape[1]
    rb0 = row0 // tm
    return pl.pallas_call(
        functools.partial(_combine_kernel, tm=tm, row0=row0),
        out_shape=jax.ShapeDtypeStruct((nrows, d), F32),
        grid_spec=pltpu.PrefetchScalarGridSpec(
            num_scalar_prefetch=2,
            grid=(nrows // tm,),
            in_specs=[
                pl.BlockSpec((tm, d), lambda i, a, b: (rb0 + i, 0)),
                pl.BlockSpec((tm, LANES), lambda i, a, b: (rb0 + i, 0)),
                pl.BlockSpec((1, d), lambda i, a, b: (0, 0)),
                pl.BlockSpec(memory_space=pl.ANY),
            ],
            out_specs=pl.BlockSpec((tm, d), lambda i, a, b: (i, 0)),
            scratch_shapes=[pltpu.VMEM((2, tm, d), F32), pltpu.SemaphoreType.DMA],
        ),
        compiler_params=_params("arbitrary"),
        name="moe_combine",
    )(d0, d1, x, route, g, y_rows)


def _gate_kernel(x_ref, g_ref, wlr_ref, wa2_ref, b_ref, o_ref):
    h = _rms_bf16(x_ref[...], g_ref[...])
    a_lr = _dot(h, wlr_ref[...])
    z = _dot(a_lr.astype(BF16), wa2_ref[...]) + b_ref[...]
    o_ref[...] = (jnp.minimum(z, 0.0) - jnp.log(1.0 + jnp.exp(-jnp.abs(z)))) * (1.0 / GATE_TAU)


def gla_gate(x, g, w_lr_pad, w_a2_pad, b_a, *, tm=ROW_TILE):
    m, d = x.shape
    n = w_a2_pad.shape[1]
    return pl.pallas_call(
        _gate_kernel,
        out_shape=jax.ShapeDtypeStruct((m, n), F32),
        grid=(m // tm,),
        in_specs=[
            pl.BlockSpec((tm, d), lambda i: (i, 0)),
            pl.BlockSpec((1, d), lambda i: (0, 0)),
            pl.BlockSpec((d, LANES), lambda i: (0, 0)),
            pl.BlockSpec((LANES, n), lambda i: (0, 0)),
            pl.BlockSpec((1, n), lambda i: (0, 0)),
        ],
        out_specs=pl.BlockSpec((tm, n), lambda i: (i, 0)),
        compiler_params=_params("parallel"),
        name="gla_gate",
    )(x, g, w_lr_pad, w_a2_pad, b_a)


def _rotary(x, cos, sin):
    half = x.shape[-1] // 2
    x1, x2 = x[:, :half], x[:, half:]
    return jnp.concatenate([x1 * cos - x2 * sin, x1 * sin + x2 * cos], axis=-1)


def _retention_kernel(lg_ref, *refs, L, has_state):
    if has_state:
        q_ref, k_ref, v_ref, g_ref, cos_ref, sin_ref, s0_ref, o_ref, sout_ref, s_ref = refs
    else:
        q_ref, k_ref, v_ref, g_ref, cos_ref, sin_ref, o_ref, sout_ref, s_ref = refs
    c = pl.program_id(1)

    @pl.when(c == 0)
    def _():
        if has_state:
            s_ref[...] = s0_ref[0]
        else:
            s_ref[...] = jnp.zeros_like(s_ref)

    cos = cos_ref[...]
    sin = sin_ref[...]
    n_col = lax.broadcasted_iota(jnp.int32, (L, 1), 0).astype(F32)
    n_row = lax.broadcasted_iota(jnp.int32, (1, L), 1).astype(F32)
    diff = n_col - n_row
    for h in range(HA):
        lg = lg_ref[h]
        qs = slice(h * DA, (h + 1) * DA)
        vs = slice(h * DVA, (h + 1) * DVA)
        qr = _rotary(q_ref[:, qs], cos, sin)
        kr = _rotary(k_ref[:, qs], cos, sin) * (DA ** -0.5)
        vb = v_ref[:, vs].astype(BF16)
        decay = jnp.where(diff >= 0.0, jnp.exp(jnp.maximum(diff, 0.0) * lg), 0.0)
        qb = qr.astype(BF16)
        scores = _dot_nt(qb, kr.astype(BF16)) * decay
        inner = _dot(scores.astype(BF16), vb)
        state = s_ref[h]
        cross = _dot(qb, state.astype(BF16)) * jnp.exp((n_col + 1.0) * lg)
        k_dec = (kr * jnp.exp((L - 1.0 - n_col) * lg)).astype(BF16)
        s_ref[h] = state * jnp.exp(jnp.zeros((1, 1), F32) + L * lg) + _dot_tn(k_dec, vb)
        ret = inner + cross
        ret = ret - jnp.mean(ret, axis=-1, keepdims=True)
        ret = ret * lax.rsqrt(jnp.mean(ret * ret, axis=-1, keepdims=True) + EPS)
        o_ref[:, vs] = (_silu(g_ref[:, vs]) * ret).astype(o_ref.dtype)

    @pl.when(c == pl.num_programs(1) - 1)
    def _():
        sout_ref[0] = s_ref[...]


def retention(proj, cos, sin, log_gamma, *, nbatch, seq, row0, pos0, L, state0=None):
    nc = seq // L
    rb0 = row0 // L
    pb0 = pos0 // L
    has_state = state0 is not None
    wq = HA * DA
    assert HA * DVA == wq

    def col(k):
        return lambda b, c, lg: (rb0 + b * nc + c, k)

    in_specs = [
        pl.BlockSpec((L, wq), col(0)),
        pl.BlockSpec((L, wq), col(1)),
        pl.BlockSpec((L, wq), col(2)),
        pl.BlockSpec((L, wq), col(3)),
        pl.BlockSpec((L, DA // 2), lambda b, c, lg: (pb0 + c, 0)),
        pl.BlockSpec((L, DA // 2), lambda b, c, lg: (pb0 + c, 0)),
    ]
    args = [proj, proj, proj, proj, cos, sin]
    if has_state:
        in_specs.append(pl.BlockSpec((1, HA, DA, DVA), lambda b, c, lg: (b, 0, 0, 0)))
        args.append(state0)
    return pl.pallas_call(
        functools.partial(_retention_kernel, L=L, has_state=has_state),
        out_shape=(jax.ShapeDtypeStruct((nbatch * seq, HA * DVA), BF16),
                   jax.ShapeDtypeStruct((nbatch, HA, DA, DVA), F32)),
        grid_spec=pltpu.PrefetchScalarGridSpec(
            num_scalar_prefetch=1,
            grid=(nbatch, nc),
            in_specs=in_specs,
            out_specs=(pl.BlockSpec((L, wq), lambda b, c, lg: (b * nc + c, 0)),
                       pl.BlockSpec((1, HA, DA, DVA), lambda b, c, lg: (b, 0, 0, 0))),
            scratch_shapes=[pltpu.VMEM((HA, DA, DVA), F32)],
        ),
        compiler_params=_params("parallel", "arbitrary"),
        name="retention",
    )(log_gamma, *args)


def _lambda_value(l_ref, lam_init):
    lv = l_ref[...]
    a = jnp.sum(lv[0:1] * lv[1:2], axis=-1, keepdims=True)
    b = jnp.sum(lv[2:3] * lv[3:4], axis=-1, keepdims=True)
    return jnp.exp(a) - jnp.exp(b) + lam_init


def _head_norm_scale(o, scale):
    return o * lax.rsqrt(jnp.mean(o * o, axis=-1, keepdims=True) + EPS) * scale


def _diff_softmax_pv(q, key_parts, val_parts, masks, lam):
    w = None
    for c in range(2):
        qc = q[:, c * DB:(c + 1) * DB].astype(BF16)
        s = [_dot_nt(qc, k[:, c * DB:(c + 1) * DB]) for k in key_parts]
        s = [x if m is None else jnp.where(m, x, NEG_BIG) for x, m in zip(s, masks)]
        mx = functools.reduce(jnp.maximum, [jnp.max(x, axis=-1, keepdims=True) for x in s])
        p = [jnp.exp(x - mx) for x in s]
        inv = 1.0 / functools.reduce(lambda a, b: a + b, [jnp.sum(x, axis=-1, keepdims=True) for x in p])
        if c == 0:
            w = [x * inv for x in p]
        else:
            w = [a - lam * (x * inv) for a, x in zip(w, p)]
    outs = [_dot(a.astype(BF16), v) for a, v in zip(w, val_parts)]
    return functools.reduce(lambda a, b: a + b, outs)


def _dattn_prompt_kernel(q_ref, k_ref, v_ref, l_ref, o_ref, *, tq, nq, lam_init):
    i = pl.program_id(2)
    lam = _lambda_value(l_ref, lam_init)
    r_chunk = lax.broadcasted_iota(jnp.int32, (tq, tq), 0) // CHUNK
    c_chunk = lax.broadcasted_iota(jnp.int32, (tq, tq), 1) // CHUNK
    diag_mask = c_chunk <= r_chunk

    for n in range(nq):
        @pl.when(i == n)
        def _(n=n):
            q = q_ref[...] * (DB ** -0.5)
            lo = n * tq
            keys = [k_ref[lo:lo + tq, :]]
            vals = [v_ref[lo:lo + tq, :]]
            masks = [diag_mask]
            if n > 0:
                keys.insert(0, k_ref[0:lo, :])
                vals.insert(0, v_ref[0:lo, :])
                masks.insert(0, None)
            o = _diff_softmax_pv(q, keys, vals, masks, lam)
            o_ref[...] = _head_norm_scale(o, 1.0 - lam_init).astype(o_ref.dtype)


def diff_attention_prompt(proj, k_bf, v_bf, lam_params, *, nbatch, seq, lam_init, tq=ATTN_TILE):
    nq = seq // tq
    assert tq % CHUNK == 0
    return pl.pallas_call(
        functools.partial(_dattn_prompt_kernel, tq=tq, nq=nq, lam_init=lam_init),
        out_shape=jax.ShapeDtypeStruct((nbatch * seq, HB * DVB), BF16),
        grid=(nbatch, HB, nq),
        in_specs=[
            pl.BlockSpec((tq, 2 * DB), lambda b, h, i: (b * nq + i, 4 * HA + h)),
            pl.BlockSpec((seq, 2 * DB), lambda b, h, i: (b, h)),
            pl.BlockSpec((seq, DVB), lambda b, h, i: (b, h)),
            pl.BlockSpec((4, DB), lambda b, h, i: (0, 0)),
        ],
        out_specs=pl.BlockSpec((tq, DVB), lambda b, h, i: (b * nq + i, h)),
        compiler_params=_params("parallel", "parallel", "arbitrary"),
        name="diff_attention_prompt",
    )(proj, k_bf, v_bf, lam_params)


def _dattn_sample_kernel(q_ref, kn_ref, vn_ref, kc_ref, vc_ref, l_ref, o_ref, *, lam_init):
    q = q_ref[...] * (DB ** -0.5)
    lam = _lambda_value(l_ref, lam_init)
    keys = [kc_ref[0].astype(BF16), kn_ref[...]]
    vals = [vc_ref[0].astype(BF16), vn_ref[...]]
    o = _diff_softmax_pv(q, keys, vals, [None, None], lam)
    o_ref[...] = _head_norm_scale(o, 1.0 - lam_init).astype(o_ref.dtype)


def diff_attention_sample(proj, k_new, v_new, cache_k, cache_v, lam_params, *, nbatch, seq, row0, lam_init):
    past = cache_k.shape[1]
    assert seq == CHUNK and past % CHUNK == 0 and row0 % seq == 0
    rb0 = row0 // seq
    return pl.pallas_call(
        functools.partial(_dattn_sample_kernel, lam_init=lam_init),
        out_shape=jax.ShapeDtypeStruct((nbatch * seq, HB * DVB), BF16),
        grid=(nbatch, HB),
        in_specs=[
            pl.BlockSpec((seq, 2 * DB), lambda b, h: (rb0 + b, 4 * HA + h)),
            pl.BlockSpec((seq, 2 * DB), lambda b, h: (b, h)),
            pl.BlockSpec((seq, DVB), lambda b, h: (b, h)),
            pl.BlockSpec((1, past, 2 * DB), lambda b, h: (b, 0, h)),
            pl.BlockSpec((1, past, DVB), lambda b, h: (b, 0, h)),
            pl.BlockSpec((4, DB), lambda b, h: (0, 0)),
        ],
        out_specs=pl.BlockSpec((seq, DVB), lambda b, h: (b, h)),
        compiler_params=_params("parallel", "parallel"),
        name="diff_attention_sample",
    )(proj, k_new, v_new, cache_k, cache_v, lam_params)


def _split3_bf16(x):
    hi = x.astype(BF16)
    r1 = x - hi.astype(F32)
    mid = r1.astype(BF16)
    lo = (r1 - mid.astype(F32)).astype(BF16)
    return hi, mid, lo


def _gla_kernel(*refs, nsub, has_state):
    if has_state:
        q_ref, k_ref, v_ref, g_ref, a_ref, s0_ref, o_ref, sout_ref, s_ref = refs
    else:
        q_ref, k_ref, v_ref, g_ref, a_ref, o_ref, sout_ref, s_ref = refs
    c = pl.program_id(1)
    L = CHUNK

    @pl.when(c == 0)
    def _():
        for h in range(HC):
            if has_state:
                s_ref[h] = s0_ref[0, h].T
            else:
                s_ref[h] = jnp.zeros((DVC, DKC), F32)

    row = lax.broadcasted_iota(jnp.int32, (L, L), 0)
    colm = lax.broadcasted_iota(jnp.int32, (L, L), 1)
    causal = colm <= row
    tril = jnp.where(causal, 1.0, 0.0).astype(BF16)

    def chunk(j, carry):
        r0 = pl.multiple_of(j * L, L)
        rows = pl.ds(r0, L)
        for h in range(HC):
            ks = slice(h * DKC, (h + 1) * DKC)
            vs = slice(h * DVC, (h + 1) * DVC)
            q = q_ref[rows, ks] * (DKC ** -0.5)
            k = k_ref[rows, ks]
            vb = v_ref[rows, vs].astype(BF16)
            hi, mid, lo = _split3_bf16(a_ref[rows, ks])
            b = _dot(tril, hi) + _dot(tril, mid) + _dot(tril, lo)
            ref = b[L // 2:L // 2 + 1, :]
            b_last = b[L - 1:L, :]
            q_in = (q * jnp.exp(b - ref)).astype(BF16)
            k_in = (k * jnp.exp(ref - b)).astype(BF16)
            att = jnp.where(causal, _dot_nt(q_in, k_in), 0.0)
            inner = _dot(att.astype(BF16), vb)
            state_t = s_ref[h]
            cross = _dot_nt((q * jnp.exp(b)).astype(BF16), state_t.astype(BF16))
            k_out = (k * jnp.exp(b_last - b)).astype(BF16)
            s_ref[h] = jnp.exp(b_last) * state_t + _dot_tn(vb, k_out)
            o = inner + cross
            o = o * lax.rsqrt(jnp.mean(o * o, axis=-1, keepdims=True) + EPS)
            o_ref[rows, vs] = (_silu(g_ref[rows, vs]) * o).astype(o_ref.dtype)
        return carry

    lax.fori_loop(0, nsub, chunk, 0)

    @pl.when(c == pl.num_programs(1) - 1)
    def _():
        for h in range(HC):
            sout_ref[0, h] = s_ref[h].T


def gla(proj, log_a, *, nbatch, seq, row0, rows_per_step, state0=None):
    lb = rows_per_step
    nc = seq // lb
    rb0 = row0 // lb
    has_state = state0 is not None
    wk = HC * DKC
    wv = HC * DVC
    assert wv == 2 * wk

    def rows(b, c):
        return rb0 + b * nc + c

    in_specs = [
        pl.BlockSpec((lb, wk), lambda b, c: (rows(b, c), 0)),
        pl.BlockSpec((lb, wk), lambda b, c: (rows(b, c), 1)),
        pl.BlockSpec((lb, wv), lambda b, c: (rows(b, c), 1)),
        pl.BlockSpec((lb, wv), lambda b, c: (rows(b, c), 2)),
        pl.BlockSpec((lb, wk), lambda b, c: (rows(b, c), 0)),
    ]
    args = [proj, proj, proj, proj, log_a]
    if has_state:
        in_specs.append(pl.BlockSpec((1, HC, DKC, DVC), lambda b, c: (b, 0, 0, 0)))
        args.append(state0)
    return pl.pallas_call(
        functools.partial(_gla_kernel, nsub=lb // CHUNK, has_state=has_state),
        out_shape=(jax.ShapeDtypeStruct((nbatch * seq, wv), BF16),
                   jax.ShapeDtypeStruct((nbatch, HC, DKC, DVC), F32)),
        grid=(nbatch, nc),
        in_specs=in_specs,
        out_specs=(pl.BlockSpec((lb, wv), lambda b, c: (b * nc + c, 0)),
                   pl.BlockSpec((1, HC, DKC, DVC), lambda b, c: (b, 0, 0, 0))),
        scratch_shapes=[pltpu.VMEM((HC, DVC, DKC), F32)],
        compiler_params=_params("parallel", "arbitrary"),
        name="gla",
    )(*args)


def _xattn_kernel(q_ref, k_ref, v_ref, o_ref):
    for h in range(HX):
        sl = slice(h * DX, (h + 1) * DX)
        q = q_ref[:, sl]
        k = k_ref[0, :, sl].astype(BF16)
        v = v_ref[0, :, sl].astype(BF16)
        s = _dot_nt(q, k) * (DX ** -0.5)
        m = jnp.max(s, axis=-1, keepdims=True)
        p = jnp.exp(s - m)
        p = p / jnp.sum(p, axis=-1, keepdims=True)
        o_ref[:, sl] = _dot(p.astype(BF16), v).astype(o_ref.dtype)


def cross_attention(q, mem_k, mem_v, *, nbatch, seq, row0, tq):
    d = q.shape[1]
    nq = seq // tq
    rb0 = row0 // tq
    return pl.pallas_call(
        _xattn_kernel,
        out_shape=jax.ShapeDtypeStruct((nbatch * seq, d), BF16),
        grid=(nbatch, nq),
        in_specs=[
            pl.BlockSpec((tq, d), lambda b, i: (rb0 + b * nq + i, 0)),
            pl.BlockSpec((1, N_MEM, d), lambda b, i: (b, 0, 0)),
            pl.BlockSpec((1, N_MEM, d), lambda b, i: (b, 0, 0)),
        ],
        out_specs=pl.BlockSpec((tq, d), lambda b, i: (b * nq + i, 0)),
        compiler_params=_params("parallel", "arbitrary"),
        name="cross_attention",
    )(q, mem_k, mem_v)


def _routing_tables(route, tm):
    n_tok = route.shape[0]
    n_rows = n_tok * TOP_K
    nb = (n_rows + N_EXPERTS * (tm - 1) + tm - 1) // tm
    flat_e = route[:, :TOP_K].astype(jnp.int32).reshape(-1)
    onehot = (flat_e[:, None] == jnp.arange(N_EXPERTS, dtype=jnp.int32)[None, :]).astype(jnp.int32)
    csum = jnp.cumsum(onehot, axis=0)
    rank = jnp.sum(csum * onehot, axis=1) - 1
    counts = csum[-1]
    padded = (counts + tm - 1) // tm * tm
    pad_end = jnp.cumsum(padded)
    pad_start = pad_end - padded
    dest = pad_start[flat_e] + rank
    tok = jnp.arange(n_rows, dtype=jnp.int32) // TOP_K
    row_tok = jnp.zeros((nb * tm,), jnp.int32).at[dest].set(tok)
    n_valid = (pad_end[-1] // tm).astype(jnp.int32)
    blk = jnp.minimum(jnp.arange(nb, dtype=jnp.int32), n_valid - 1) * tm
    block_expert = jnp.minimum(jnp.searchsorted(pad_end, blk, side='right'), N_EXPERTS - 1).astype(jnp.int32)
    dest2 = dest.reshape(n_tok, TOP_K).astype(jnp.int32)
    return row_tok, block_expert, n_valid.reshape(1), dest2[:, 0], dest2[:, 1]


def kernel(x_prompt, x_sample, mem_prompt, cache_diff_k, cache_diff_v, state_ret, state_gla, cache_mem_k, cache_mem_v, g_mix, g_xattn, g_mem, g_ffn, g_final, w_in_even, w_out_even, lambda_q1, lambda_k1, lambda_q2, lambda_k2, w_in_odd, w_gate_lr, b_gate_lr, w_out_odd, w_xq, w_xkv, w_xo, w_ffn_gu, w_ffn_dn, w_router, w_moe_gu, w_moe_dn):
    d = D_MODEL
    bp, tp, _ = x_prompt.shape
    bs, ts, _ = x_sample.shape
    past = cache_diff_k.shape[2]
    np_tok = bp * tp
    ns_tok = bs * ts
    n_tok = np_tok + ns_tok
    depth = g_mix.shape[0]

    x = jnp.concatenate([x_prompt.reshape(np_tok, d), x_sample.reshape(ns_tok, d)], axis=0)

    half = DA // 2
    inv = 1.0 / (ROPE_BASE ** jnp.linspace(0.0, 1.0, half, dtype=F32))
    pos = jnp.arange(max(tp, past + ts), dtype=jnp.int32).astype(F32)
    ang = pos[:, None] * inv[None, :]
    cos_t, sin_t = jnp.cos(ang), jnp.sin(ang)
    log_gamma = jnp.log1p(-jnp.power(2.0, -5.0 - jnp.arange(HA, dtype=F32)))

    ones_blocks = jnp.zeros((n_tok // ROW_TILE,), jnp.int32)
    all_valid = jnp.full((1,), n_tok // ROW_TILE, jnp.int32)

    ret_p, ret_s, gla_p, gla_s = [], [], [], []
    dk_p, dv_p, dk_s, dv_s = [], [], [], []
    mk_p, mv_p = [], []
    y_rows = route = d0 = d1 = None

    for i in range(depth):
        j = i // 2
        g_i = g_mix[i].reshape(1, d)
        if i % 2 == 0:
            w_in = w_in_even[j].astype(BF16)
            c_dk = 4 * HA * DA + HB * 2 * DB
            w_kv = HB * 2 * DB
            assert HB * DVB == w_kv
            proj = norm_matmul(x, g_i, w_in, col0=0, ncols=c_dk, tn=1024, out_dtype=F32, tm=PROJ_ROW_TILE)
            dk_f_p, dk_b_p, dv_f_p, dv_b_p = norm_matmul_groups(
                x, g_i, w_in, row0=0, nrows=np_tok, col0=c_dk, group_cols=w_kv, n_groups=2, out_dtypes=(F32, BF16))
            dk_f_s, dk_b_s, dv_f_s, dv_b_s = norm_matmul_groups(
                x, g_i, w_in, row0=np_tok, nrows=ns_tok, col0=c_dk, group_cols=w_kv, n_groups=2,
                out_dtypes=(F32, BF16))
            lam_init = 0.8 - 0.6 * math.exp(-0.3 * i)
            lam_params = jnp.stack([lambda_q1[j], lambda_k1[j], lambda_q2[j], lambda_k2[j]]).astype(F32)
            a_p, s_p = retention(proj, cos_t, sin_t, log_gamma, nbatch=bp, seq=tp, row0=0, pos0=0, L=RET_CHUNK)
            a_s, s_s = retention(proj, cos_t, sin_t, log_gamma, nbatch=bs, seq=ts, row0=np_tok, pos0=past,
                                 L=CHUNK, state0=state_ret[j])
            ret_p.append(s_p)
            ret_s.append(s_s)
            b_p = diff_attention_prompt(proj, dk_b_p, dv_b_p, lam_params, nbatch=bp, seq=tp, lam_init=lam_init)
            b_s = diff_attention_sample(
                proj, dk_b_s, dv_b_s, cache_diff_k[j].reshape(bs, past, HB * 2 * DB),
                cache_diff_v[j].reshape(bs, past, HB * DVB), lam_params, nbatch=bs, seq=ts, row0=np_tok,
                lam_init=lam_init)
            dk_p.append(dk_f_p.reshape(bp, tp, HB, 2, DB))
            dk_s.append(dk_f_s.reshape(bs, ts, HB, 2, DB))
            dv_p.append(dv_f_p.reshape(bp, tp, HB, DVB))
            dv_s.append(dv_f_s.reshape(bs, ts, HB, DVB))
            x = matmul_residual([(a_p, a_s), (b_p, b_s)], w_out_even[j].astype(BF16), x)
        else:
            n_main = 2 * HC * DKC + 2 * HC * DVC
            w_in = w_in_odd[j]
            proj = norm_matmul(x, g_i, w_in[:, :n_main].astype(BF16), col0=0, ncols=n_main, tn=1024, out_dtype=F32,
                               tm=PROJ_ROW_TILE)
            w_lr = jnp.zeros((d, LANES), BF16).at[:, :GATE_RANK].set(w_in[:, n_main:].astype(BF16))
            w_a2 = jnp.zeros((LANES, HC * DKC), BF16).at[:GATE_RANK].set(w_gate_lr[j].astype(BF16))
            log_a = gla_gate(x, g_i, w_lr, w_a2, b_gate_lr[j].reshape(1, -1).astype(F32))
            o_p, s_p = gla(proj, log_a, nbatch=bp, seq=tp, row0=0, rows_per_step=GLA_BLOCK)
            o_s, s_s = gla(proj, log_a, nbatch=bs, seq=ts, row0=np_tok, rows_per_step=ts, state0=state_gla[j])
            gla_p.append(s_p)
            gla_s.append(s_s)
            x = matmul_residual([(o_p, o_s)], w_out_odd[j].astype(BF16), x)

        (q,) = norm_matmul_groups(x, g_xattn[i].reshape(1, d), w_xq[i].astype(BF16), row0=0, nrows=n_tok, col0=0,
                                  group_cols=HX * DX, n_groups=1, out_dtypes=(BF16,))
        mem2d = mem_prompt.reshape(bp * N_MEM, d)
        mk, mv = norm_matmul_groups(mem2d, g_mem[i].reshape(1, d), w_xkv[i].astype(BF16), row0=0,
                                    nrows=bp * N_MEM, col0=0, group_cols=HX * DX, n_groups=2, out_dtypes=(F32,),
                                    tm=MEM_ROW_TILE)
        mk_p.append(mk.reshape(bp, N_MEM, HX, DX))
        mv_p.append(mv.reshape(bp, N_MEM, HX, DX))
        o_p = cross_attention(q, mk.reshape(bp, N_MEM, HX * DX), mv.reshape(bp, N_MEM, HX * DX),
                              nbatch=bp, seq=tp, row0=0, tq=512)
        o_s = cross_attention(q, cache_mem_k[i].reshape(bs, N_MEM, HX * DX),
                              cache_mem_v[i].reshape(bs, N_MEM, HX * DX), nbatch=bs, seq=ts, row0=np_tok, tq=ts)
        x = matmul_residual([(o_p, o_s)], w_xo[i].astype(BF16), x)

        g_f = g_ffn[i].reshape(1, d)
        if i % 2 == 0:
            x = swiglu_blocks(x, g_f, w_ffn_gu[j].astype(BF16)[None], w_ffn_dn[j].astype(BF16)[None],
                              ones_blocks, all_valid, residual=True)
        else:
            w_r = jnp.zeros((d, LANES), BF16).at[:, :N_EXPERTS].set(w_router[j].astype(BF16))
            route = router(x, g_f, w_r)
            row_tok, block_expert, n_valid, d0, d1 = _routing_tables(route, ROW_TILE)
            xs = gather_rows(x, row_tok)
            y_rows = swiglu_blocks(xs, g_f, w_moe_gu[j].astype(BF16), w_moe_dn[j].astype(BF16),
                                   block_expert, n_valid, residual=False)
            if i != depth - 1:
                raise NotImplementedError("MoE layer must be the last layer")

    g_fin = g_final.reshape(1, d)
    y_p = moe_combine_norm(x, route, g_fin, y_rows, d0, d1, row0=0, nrows=np_tok)
    y_s = moe_combine_norm(x, route, g_fin, y_rows, d0, d1, row0=np_tok, nrows=ns_tok)

    return (y_p.reshape(bp, tp, d), y_s.reshape(bs, ts, d),
            jnp.stack(dk_p), jnp.stack(dv_p), jnp.stack(ret_p), jnp.stack(gla_p),
            jnp.stack(mk_p), jnp.stack(mv_p),
            jnp.stack(dk_s), jnp.stack(dv_s), jnp.stack(ret_s), jnp.stack(gla_s))
```

```python
import functools
import math

import jax
import jax.numpy as jnp
import numpy as np
from jax import lax
from jax.experimental import pallas as pl
from jax.experimental.pallas import tpu as pltpu

F32 = jnp.float32
BF16 = jnp.bfloat16

D_MODEL = 2048
CHUNK = 64
N_MEM = 256
EPS = 1e-6
HA, DA, DVA = 4, 256, 256
ROPE_BASE = 10000.0
HB, DB, DVB = 4, 128, 256
HC, DKC, DVC = 4, 256, 512
GATE_RANK = 16
GATE_TAU = 16.0
HX, DX = 4, 512
D_FF = 5632
N_EXPERTS = 8
TOP_K = 2
NEG_BIG = -1e30

VMEM_LIMIT_BYTES = 56 * 1024 * 1024
LANES = 128

ROW_TILE = 512
PROJ_ROW_TILE = 768
MEM_ROW_TILE = 256
FF_TILE = 512
COMBINE_TILE = 256
DMA_ISSUE_UNROLL = 8
ATTN_TILE = 256
RET_CHUNK = 256
GLA_BLOCK = 256


def _params(*sem):
    return pltpu.CompilerParams(dimension_semantics=sem, vmem_limit_bytes=VMEM_LIMIT_BYTES)


def _resident_spec(block_shape, index_map):
    return pl.BlockSpec(block_shape, index_map, pipeline_mode=pl.Buffered(1))


def _rms_bf16(x, g):
    ms = jnp.mean(x * x, axis=-1, keepdims=True)
    return (x * lax.rsqrt(ms + EPS) * g).astype(BF16)


def _silu(x):
    return x / (1.0 + jnp.exp(-x))


def _dot(a, b):
    return jnp.dot(a, b, preferred_element_type=F32)


def _dot_nt(a, b):
    return lax.dot_general(a, b, (((1,), (1,)), ((), ())), preferred_element_type=F32)


def _dot_tn(a, b):
    return lax.dot_general(a, b, (((0,), (0,)), ((), ())), preferred_element_type=F32)


def _norm_mm_kernel(x_ref, g_ref, w_ref, o_ref, h_ref):
    @pl.when(pl.program_id(1) == 0)
    def _():
        h_ref[...] = _rms_bf16(x_ref[...], g_ref[...])

    o_ref[...] = _dot(h_ref[...], w_ref[...]).astype(o_ref.dtype)


def norm_matmul(x, g, w, *, col0, ncols, tn, out_dtype, tm=ROW_TILE):
    m, d = x.shape
    assert m % tm == 0 and ncols % tn == 0 and col0 % tn == 0
    cb0 = col0 // tn
    return pl.pallas_call(
        _norm_mm_kernel,
        out_shape=jax.ShapeDtypeStruct((m, ncols), out_dtype),
        grid=(m // tm, ncols // tn),
        in_specs=[
            pl.BlockSpec((tm, d), lambda i, j: (i, 0)),
            pl.BlockSpec((1, d), lambda i, j: (0, 0)),
            pl.BlockSpec((d, tn), lambda i, j: (0, cb0 + j)),
        ],
        out_specs=pl.BlockSpec((tm, tn), lambda i, j: (i, j)),
        scratch_shapes=[pltpu.VMEM((tm, d), BF16)],
        compiler_params=_params("parallel", "arbitrary"),
        name="norm_matmul",
    )(x, g, w)


def _norm_mm_groups_kernel(*refs, n_groups):
    x_ref, g_ref = refs[:2]
    w_refs = refs[2:2 + n_groups]
    o_refs = refs[2 + n_groups:]
    h = _rms_bf16(x_ref[...], g_ref[...])
    outs_per_group = len(o_refs) // n_groups
    for k, w_ref in enumerate(w_refs):
        y = _dot(h, w_ref[...])
        for o_ref in o_refs[k * outs_per_group:(k + 1) * outs_per_group]:
            o_ref[...] = y.astype(o_ref.dtype)


def norm_matmul_groups(x, g, w, *, row0, nrows, col0, group_cols, n_groups, out_dtypes, tm=ROW_TILE):
    d = x.shape[1]
    assert nrows % tm == 0 and row0 % tm == 0 and col0 % group_cols == 0
    rb0 = row0 // tm
    cb0 = col0 // group_cols
    in_specs = [pl.BlockSpec((tm, d), lambda i: (rb0 + i, 0)), pl.BlockSpec((1, d), lambda i: (0, 0))]
    for k in range(n_groups):
        in_specs.append(_resident_spec((d, group_cols), lambda i, k=k: (0, cb0 + k)))
    out_shape, out_specs = [], []
    for k in range(n_groups):
        for dt in out_dtypes:
            out_shape.append(jax.ShapeDtypeStruct((nrows, group_cols), dt))
            out_specs.append(pl.BlockSpec((tm, group_cols), lambda i: (i, 0)))
    return pl.pallas_call(
        functools.partial(_norm_mm_groups_kernel, n_groups=n_groups),
        out_shape=out_shape,
        grid=(nrows // tm,),
        in_specs=in_specs,
        out_specs=out_specs,
        compiler_params=_params("parallel"),
        name="norm_matmul_groups",
    )(x, g, *([w] * n_groups))


def _mm_res_kernel(*refs, n_lhs, n_head_blocks):
    head_refs = refs[:n_lhs]
    tail_refs = refs[n_lhs:2 * n_lhs]
    w_refs = refs[2 * n_lhs:3 * n_lhs]
    r_ref = refs[3 * n_lhs]
    o_ref = refs[3 * n_lhs + 1]

    def compute(a_refs):
        acc = r_ref[...]
        for a_ref, w_ref in zip(a_refs, w_refs):
            acc = acc + _dot(a_ref[...], w_ref[...])
        o_ref[...] = acc

    is_head = pl.program_id(0) < n_head_blocks

    @pl.when(is_head)
    def _():
        compute(head_refs)

    @pl.when(jnp.logical_not(is_head))
    def _():
        compute(tail_refs)


def matmul_residual(lhs_pairs, w, res, *, tm=ROW_TILE):
    m, n = res.shape
    n_lhs = len(lhs_pairs)
    nhb = lhs_pairs[0][0].shape[0] // tm
    in_specs = []
    for head, tail in lhs_pairs:
        assert head.shape[0] == nhb * tm and tail.shape[0] == tm and (nhb + 1) * tm == m
        in_specs.append(pl.BlockSpec((tm, head.shape[1]), lambda i: (jnp.minimum(i, nhb - 1), 0)))
    for head, tail in lhs_pairs:
        in_specs.append(pl.BlockSpec((tm, tail.shape[1]), lambda i: (0, 0)))
    row = 0
    for head, _ in lhs_pairs:
        kk = head.shape[1]
        assert row % kk == 0
        rb = row // kk
        in_specs.append(_resident_spec((kk, n), lambda i, rb=rb: (rb, 0)))
        row += kk
    in_specs.append(pl.BlockSpec((tm, n), lambda i: (i, 0)))
    heads = [p[0] for p in lhs_pairs]
    tails = [p[1] for p in lhs_pairs]
    return pl.pallas_call(
        functools.partial(_mm_res_kernel, n_lhs=n_lhs, n_head_blocks=nhb),
        out_shape=jax.ShapeDtypeStruct((m, n), F32),
        grid=(m // tm,),
        in_specs=in_specs,
        out_specs=pl.BlockSpec((tm, n), lambda i: (i, 0)),
        compiler_params=_params("parallel"),
        name="matmul_residual",
    )(*heads, *tails, *([w] * n_lhs), res)


def _ffn_kernel(be_ref, nv_ref, x_ref, g_ref, wg_ref, wu_ref, wd_ref, o_ref, h_ref, *, residual):
    i = pl.program_id(0)
    f = pl.program_id(1)
    valid = i < nv_ref[0]

    @pl.when(f == 0)
    def _():
        x = x_ref[...]
        h_ref[...] = _rms_bf16(x, g_ref[...])
        o_ref[...] = x if residual else jnp.zeros_like(x)

    @pl.when(valid)
    def _():
        h = h_ref[...]
        a = _dot(h, wg_ref[0])
        u = _dot(h, wu_ref[0])
        act = (_silu(a) * u).astype(BF16)
        o_ref[...] += _dot(act, wd_ref[0])


def swiglu_blocks(x, g, w_gu, w_dn, block_expert, n_valid, *, residual, tm=ROW_TILE, tf=FF_TILE):
    m, d = x.shape
    ff = w_dn.shape[1]
    nf = ff // tf
    nb = m // tm

    def _x_map(i, f, be, nv):
        return (jnp.minimum(i, nv[0] - 1), 0)

    def _f_eff(i, f, nv):
        return jnp.where(i < nv[0], f, nf - 1)

    return pl.pallas_call(
        functools.partial(_ffn_kernel, residual=residual),
        out_shape=jax.ShapeDtypeStruct((m, d), F32),
        grid_spec=pltpu.PrefetchScalarGridSpec(
            num_scalar_prefetch=2,
            grid=(nb, nf),
            in_specs=[
                pl.BlockSpec((tm, d), _x_map),
                pl.BlockSpec((1, d), lambda i, f, be, nv: (0, 0)),
                pl.BlockSpec((1, d, tf), lambda i, f, be, nv: (be[i], 0, _f_eff(i, f, nv))),
                pl.BlockSpec((1, d, tf), lambda i, f, be, nv: (be[i], 0, nf + _f_eff(i, f, nv))),
                pl.BlockSpec((1, tf, d), lambda i, f, be, nv: (be[i], _f_eff(i, f, nv), 0)),
            ],
            out_specs=pl.BlockSpec((tm, d), lambda i, f, be, nv: (i, 0)),
            scratch_shapes=[pltpu.VMEM((tm, d), BF16)],
        ),
        compiler_params=_params("parallel", "arbitrary"),
        name="swiglu_blocks",
    )(block_expert, n_valid, x, g, w_gu, w_gu, w_dn)


def _router_kernel(x_ref, g_ref, w_ref, o_ref):
    h = _rms_bf16(x_ref[...], g_ref[...])
    logits = _dot(h, w_ref[...])
    lane = lax.broadcasted_iota(jnp.int32, logits.shape, 1).astype(F32)
    l1 = jnp.where(lane < N_EXPERTS, logits, NEG_BIG)
    m1 = jnp.max(l1, axis=-1, keepdims=True)
    i1 = jnp.min(jnp.where(l1 == m1, lane, float(LANES)), axis=-1, keepdims=True)
    l2 = jnp.where(lane == i1, NEG_BIG, l1)
    m2 = jnp.max(l2, axis=-1, keepdims=True)
    i2 = jnp.min(jnp.where(l2 == m2, lane, float(LANES)), axis=-1, keepdims=True)
    e = jnp.exp(m2 - m1)
    g1 = 1.0 / (1.0 + e)
    g2 = e / (1.0 + e)
    out = jnp.where(lane == 0.0, i1,
                    jnp.where(lane == 1.0, i2,
                              jnp.where(lane == 2.0, g1, jnp.where(lane == 3.0, g2, 0.0))))
    o_ref[...] = out


def router(x, g, w_pad, *, tm=ROW_TILE):
    m, d = x.shape
    return pl.pallas_call(
        _router_kernel,
        out_shape=jax.ShapeDtypeStruct((m, LANES), F32),
        grid=(m // tm,),
        in_specs=[
            pl.BlockSpec((tm, d), lambda i: (i, 0)),
            pl.BlockSpec((1, d), lambda i: (0, 0)),
            pl.BlockSpec((d, LANES), lambda i: (0, 0)),
        ],
        out_specs=pl.BlockSpec((tm, LANES), lambda i: (i, 0)),
        compiler_params=_params("parallel"),
        name="router",
    )(x, g, w_pad)


def _row_copy(src_hbm, dst_vmem, src_row, dst_row, sem):
    return pltpu.make_async_copy(src_hbm.at[pl.ds(src_row, 1)], dst_vmem.at[pl.ds(dst_row, 1)], sem)


def _gather_kernel(idx_ref, x_hbm, o_ref, sem, *, tm):
    base = pl.program_id(0) * tm

    def start(r, c):
        _row_copy(x_hbm, o_ref, idx_ref[base + r], r, sem).start()
        return c

    lax.fori_loop(0, tm, start, 0, unroll=DMA_ISSUE_UNROLL)
    pltpu.make_async_copy(x_hbm.at[pl.ds(0, tm)], o_ref, sem).wait()


def gather_rows(x, row_idx, *, tm=ROW_TILE):
    r_total = row_idx.shape[0]
    d = x.shape[1]
    return pl.pallas_call(
        functools.partial(_gather_kernel, tm=tm),
        out_shape=jax.ShapeDtypeStruct((r_total, d), F32),
        grid_spec=pltpu.PrefetchScalarGridSpec(
            num_scalar_prefetch=1,
            grid=(r_total // tm,),
            in_specs=[pl.BlockSpec(memory_space=pl.ANY)],
            out_specs=pl.BlockSpec((tm, d), lambda i, idx: (i, 0)),
            scratch_shapes=[pltpu.SemaphoreType.DMA],
        ),
        compiler_params=_params("arbitrary"),
        name="moe_gather",
    )(row_idx, x)


def _combine_kernel(d0_ref, d1_ref, x_ref, r_ref, g_ref, y_hbm, o_ref, buf, sem, *, tm, row0):
    i = pl.program_id(0)
    slot = i % 2

    def issue(block, slot_):
        base = row0 + block * tm

        def start(r, c):
            _row_copy(y_hbm, buf.at[slot_], d0_ref[base + r], r, sem.at[slot_]).start()
            _row_copy(y_hbm, buf.at[slot_], d1_ref[base + r], tm + r, sem.at[slot_]).start()
            return c

        lax.fori_loop(0, tm, start, 0, unroll=DMA_ISSUE_UNROLL)

    @pl.when(i == 0)
    def _():
        issue(0, 0)

    @pl.when(i + 1 < pl.num_programs(0))
    def _():
        issue(i + 1, 1 - slot)

    pltpu.make_async_copy(y_hbm.at[pl.ds(0, 2 * tm)], buf.at[slot], sem.at[slot]).wait()
    rt = r_ref[...]
    g0 = rt[:, 2:3]
    g1 = rt[:, 3:4]
    x = x_ref[...] + (buf[slot, 0:tm, :] * g0 + buf[slot, tm:2 * tm, :] * g1)
    ms = jnp.mean(x * x, axis=-1, keepdims=True)
    o_ref[...] = x * lax.rsqrt(ms + EPS) * g_ref[...]


def moe_combine_norm(x, route, g, y_rows, d0, d1, *, row0, nrows, tm=COMBINE_TILE):
    d = x.shape[1]
    rb0 = row0 // tm
    return pl.pallas_call(
        functools.partial(_combine_kernel, tm=tm, row0=row0),
        out_shape=jax.ShapeDtypeStruct((nrows, d), F32),
        grid_spec=pltpu.PrefetchScalarGridSpec(
            num_scalar_prefetch=2,
            grid=(nrows // tm,),
            in_specs=[
                pl.BlockSpec((tm, d), lambda i, a, b: (rb0 + i, 0)),
                pl.BlockSpec((tm, LANES), lambda i, a, b: (rb0 + i, 0)),
                pl.BlockSpec((1, d), lambda i, a, b: (0, 0)),
                pl.BlockSpec(memory_space=pl.ANY),
            ],
            out_specs=pl.BlockSpec((tm, d), lambda i, a, b: (i, 0)),
            scratch_shapes=[pltpu.VMEM((2, 2 * tm, d), F32), pltpu.SemaphoreType.DMA((2,))],
        ),
        compiler_params=_params("arbitrary"),
        name="moe_combine",
    )(d0, d1, x, route, g, y_rows)


def _gate_kernel(x_ref, g_ref, wlr_ref, wa2_ref, b_ref, o_ref):
    h = _rms_bf16(x_ref[...], g_ref[...])
    a_lr = _dot(h, wlr_ref[...])
    z = _dot(a_lr.astype(BF16), wa2_ref[...]) + b_ref[...]
    o_ref[...] = (jnp.minimum(z, 0.0) - jnp.log(1.0 + jnp.exp(-jnp.abs(z)))) * (1.0 / GATE_TAU)


def gla_gate(x, g, w_lr_pad, w_a2_pad, b_a, *, tm=ROW_TILE):
    m, d = x.shape
    n = w_a2_pad.shape[1]
    return pl.pallas_call(
        _gate_kernel,
        out_shape=jax.ShapeDtypeStruct((m, n), F32),
        grid=(m // tm,),
        in_specs=[
            pl.BlockSpec((tm, d), lambda i: (i, 0)),
            pl.BlockSpec((1, d), lambda i: (0, 0)),
            pl.BlockSpec((d, LANES), lambda i: (0, 0)),
            pl.BlockSpec((LANES, n), lambda i: (0, 0)),
            pl.BlockSpec((1, n), lambda i: (0, 0)),
        ],
        out_specs=pl.BlockSpec((tm, n), lambda i: (i, 0)),
        compiler_params=_params("parallel"),
        name="gla_gate",
    )(x, g, w_lr_pad, w_a2_pad, b_a)


def _rotary(x, cos, sin):
    half = x.shape[-1] // 2
    x1, x2 = x[:, :half], x[:, half:]
    return jnp.concatenate([x1 * cos - x2 * sin, x1 * sin + x2 * cos], axis=-1)


def _retention_kernel(lg_ref, *refs, L, has_state):
    if has_state:
        q_ref, k_ref, v_ref, g_ref, cos_ref, sin_ref, s0_ref, o_ref, sout_ref, s_ref = refs
    else:
        q_ref, k_ref, v_ref, g_ref, cos_ref, sin_ref, o_ref, sout_ref, s_ref = refs
    c = pl.program_id(1)

    @pl.when(c == 0)
    def _():
        if has_state:
            s_ref[...] = s0_ref[0]
        else:
            s_ref[...] = jnp.zeros_like(s_ref)

    cos = cos_ref[...]
    sin = sin_ref[...]
    n_col = lax.broadcasted_iota(jnp.int32, (L, 1), 0).astype(F32)
    n_row = lax.broadcasted_iota(jnp.int32, (1, L), 1).astype(F32)
    diff = n_col - n_row
    for h in range(HA):
        lg = lg_ref[h]
        qs = slice(h * DA, (h + 1) * DA)
        vs = slice(h * DVA, (h + 1) * DVA)
        qr = _rotary(q_ref[:, qs], cos, sin)
        kr = _rotary(k_ref[:, qs], cos, sin) * (DA ** -0.5)
        vb = v_ref[:, vs].astype(BF16)
        decay = jnp.where(diff >= 0.0, jnp.exp(jnp.maximum(diff, 0.0) * lg), 0.0)
        qb = qr.astype(BF16)
        scores = _dot_nt(qb, kr.astype(BF16)) * decay
        inner = _dot(scores.astype(BF16), vb)
        state = s_ref[h]
        cross = _dot(qb, state.astype(BF16)) * jnp.exp((n_col + 1.0) * lg)
        k_dec = (kr * jnp.exp((L - 1.0 - n_col) * lg)).astype(BF16)
        s_ref[h] = state * jnp.exp(jnp.zeros((1, 1), F32) + L * lg) + _dot_tn(k_dec, vb)
        ret = inner + cross
        ret = ret - jnp.mean(ret, axis=-1, keepdims=True)
        ret = ret * lax.rsqrt(jnp.mean(ret * ret, axis=-1, keepdims=True) + EPS)
        o_ref[:, vs] = (_silu(g_ref[:, vs]) * ret).astype(o_ref.dtype)

    @pl.when(c == pl.num_programs(1) - 1)
    def _():
        sout_ref[0] = s_ref[...]


def retention(proj, cos, sin, log_gamma, *, nbatch, seq, row0, pos0, L, state0=None):
    nc = seq // L
    rb0 = row0 // L
    pb0 = pos0 // L
    has_state = state0 is not None
    wq = HA * DA
    assert HA * DVA == wq

    def col(k):
        return lambda b, c, lg: (rb0 + b * nc + c, k)

    in_specs = [
        pl.BlockSpec((L, wq), col(0)),
        pl.BlockSpec((L, wq), col(1)),
        pl.BlockSpec((L, wq), col(2)),
        pl.BlockSpec((L, wq), col(3)),
        pl.BlockSpec((L, DA // 2), lambda b, c, lg: (pb0 + c, 0)),
        pl.BlockSpec((L, DA // 2), lambda b, c, lg: (pb0 + c, 0)),
    ]
    args = [proj, proj, proj, proj, cos, sin]
    if has_state:
        in_specs.append(pl.BlockSpec((1, HA, DA, DVA), lambda b, c, lg: (b, 0, 0, 0)))
        args.append(state0)
    return pl.pallas_call(
        functools.partial(_retention_kernel, L=L, has_state=has_state),
        out_shape=(jax.ShapeDtypeStruct((nbatch * seq, HA * DVA), BF16),
                   jax.ShapeDtypeStruct((nbatch, HA, DA, DVA), F32)),
        grid_spec=pltpu.PrefetchScalarGridSpec(
            num_scalar_prefetch=1,
            grid=(nbatch, nc),
            in_specs=in_specs,
            out_specs=(pl.BlockSpec((L, wq), lambda b, c, lg: (b * nc + c, 0)),
                       pl.BlockSpec((1, HA, DA, DVA), lambda b, c, lg: (b, 0, 0, 0))),
            scratch_shapes=[pltpu.VMEM((HA, DA, DVA), F32)],
        ),
        compiler_params=_params("parallel", "arbitrary"),
        name="retention",
    )(log_gamma, *args)


def _lambda_value(l_ref, lam_init):
    lv = l_ref[...]
    a = jnp.sum(lv[0:1] * lv[1:2], axis=-1, keepdims=True)
    b = jnp.sum(lv[2:3] * lv[3:4], axis=-1, keepdims=True)
    return jnp.exp(a) - jnp.exp(b) + lam_init


def _head_norm_scale(o, scale):
    return o * lax.rsqrt(jnp.mean(o * o, axis=-1, keepdims=True) + EPS) * scale


def _diff_softmax_pv(q, key_parts, val_parts, masks, lam):
    w = None
    for c in range(2):
        qc = q[:, c * DB:(c + 1) * DB].astype(BF16)
        s = [_dot_nt(qc, k[:, c * DB:(c + 1) * DB]) for k in key_parts]
        s = [x if m is None else jnp.where(m, x, NEG_BIG) for x, m in zip(s, masks)]
        mx = functools.reduce(jnp.maximum, [jnp.max(x, axis=-1, keepdims=True) for x in s])
        p = [jnp.exp(x - mx) for x in s]
        inv = 1.0 / functools.reduce(lambda a, b: a + b, [jnp.sum(x, axis=-1, keepdims=True) for x in p])
        if c == 0:
            w = [x * inv for x in p]
        else:
            w = [a - lam * (x * inv) for a, x in zip(w, p)]
    outs = [_dot(a.astype(BF16), v) for a, v in zip(w, val_parts)]
    return functools.reduce(lambda a, b: a + b, outs)


def _dattn_prompt_kernel(q_ref, k_ref, v_ref, l_ref, o_ref, *, tq, nq, lam_init):
    i = pl.program_id(2)
    lam = _lambda_value(l_ref, lam_init)
    r_chunk = lax.broadcasted_iota(jnp.int32, (tq, tq), 0) // CHUNK
    c_chunk = lax.broadcasted_iota(jnp.int32, (tq, tq), 1) // CHUNK
    diag_mask = c_chunk <= r_chunk

    for n in range(nq):
        @pl.when(i == n)
        def _(n=n):
            q = q_ref[...] * (DB ** -0.5)
            lo = n * tq
            keys = [k_ref[lo:lo + tq, :]]
            vals = [v_ref[lo:lo + tq, :]]
            masks = [diag_mask]
            if n > 0:
                keys.insert(0, k_ref[0:lo, :])
                vals.insert(0, v_ref[0:lo, :])
                masks.insert(0, None)
            o = _diff_softmax_pv(q, keys, vals, masks, lam)
            o_ref[...] = _head_norm_scale(o, 1.0 - lam_init).astype(o_ref.dtype)


def diff_attention_prompt(proj, k_bf, v_bf, lam_params, *, nbatch, seq, lam_init, tq=ATTN_TILE):
    nq = seq // tq
    assert tq % CHUNK == 0
    return pl.pallas_call(
        functools.partial(_dattn_prompt_kernel, tq=tq, nq=nq, lam_init=lam_init),
        out_shape=jax.ShapeDtypeStruct((nbatch * seq, HB * DVB), BF16),
        grid=(nbatch, HB, nq),
        in_specs=[
            pl.BlockSpec((tq, 2 * DB), lambda b, h, i: (b * nq + i, 4 * HA + h)),
            pl.BlockSpec((seq, 2 * DB), lambda b, h, i: (b, h)),
            pl.BlockSpec((seq, DVB), lambda b, h, i: (b, h)),
            pl.BlockSpec((4, DB), lambda b, h, i: (0, 0)),
        ],
        out_specs=pl.BlockSpec((tq, DVB), lambda b, h, i: (b * nq + i, h)),
        compiler_params=_params("parallel", "parallel", "arbitrary"),
        name="diff_attention_prompt",
    )(proj, k_bf, v_bf, lam_params)


def _dattn_sample_kernel(q_ref, kn_ref, vn_ref, kc_ref, vc_ref, l_ref, o_ref, *, lam_init):
    q = q_ref[...] * (DB ** -0.5)
    lam = _lambda_value(l_ref, lam_init)
    keys = [kc_ref[0].astype(BF16), kn_ref[...]]
    vals = [vc_ref[0].astype(BF16), vn_ref[...]]
    o = _diff_softmax_pv(q, keys, vals, [None, None], lam)
    o_ref[...] = _head_norm_scale(o, 1.0 - lam_init).astype(o_ref.dtype)


def diff_attention_sample(proj, k_new, v_new, cache_k, cache_v, lam_params, *, nbatch, seq, row0, lam_init):
    past = cache_k.shape[1]
    assert seq == CHUNK and past % CHUNK == 0 and row0 % seq == 0
    rb0 = row0 // seq
    return pl.pallas_call(
        functools.partial(_dattn_sample_kernel, lam_init=lam_init),
        out_shape=jax.ShapeDtypeStruct((nbatch * seq, HB * DVB), BF16),
        grid=(nbatch, HB),
        in_specs=[
            pl.BlockSpec((seq, 2 * DB), lambda b, h: (rb0 + b, 4 * HA + h)),
            pl.BlockSpec((seq, 2 * DB), lambda b, h: (b, h)),
            pl.BlockSpec((seq, DVB), lambda b, h: (b, h)),
            pl.BlockSpec((1, past, 2 * DB), lambda b, h: (b, 0, h)),
            pl.BlockSpec((1, past, DVB), lambda b, h: (b, 0, h)),
            pl.BlockSpec((4, DB), lambda b, h: (0, 0)),
        ],
        out_specs=pl.BlockSpec((seq, DVB), lambda b, h: (b, h)),
        compiler_params=_params("parallel", "parallel"),
        name="diff_attention_sample",
    )(proj, k_new, v_new, cache_k, cache_v, lam_params)


def _split3_bf16(x):
    hi = x.astype(BF16)
    r1 = x - hi.astype(F32)
    mid = r1.astype(BF16)
    lo = (r1 - mid.astype(F32)).astype(BF16)
    return hi, mid, lo


def _gla_kernel(*refs, nsub, has_state):
    if has_state:
        q_ref, k_ref, v_ref, g_ref, a_ref, s0_ref, o_ref, sout_ref, s_ref = refs
    else:
        q_ref, k_ref, v_ref, g_ref, a_ref, o_ref, sout_ref, s_ref = refs
    c = pl.program_id(1)
    L = CHUNK

    @pl.when(c == 0)
    def _():
        for h in range(HC):
            if has_state:
                s_ref[h] = s0_ref[0, h].T
            else:
                s_ref[h] = jnp.zeros((DVC, DKC), F32)

    row = lax.broadcasted_iota(jnp.int32, (L, L), 0)
    colm = lax.broadcasted_iota(jnp.int32, (L, L), 1)
    causal = colm <= row
    tril = jnp.where(causal, 1.0, 0.0).astype(BF16)

    def chunk(j):
        rows = pl.ds(j * L, L)
        for h in range(HC):
            ks = slice(h * DKC, (h + 1) * DKC)
            vs = slice(h * DVC, (h + 1) * DVC)
            q = q_ref[rows, ks] * (DKC ** -0.5)
            k = k_ref[rows, ks]
            vb = v_ref[rows, vs].astype(BF16)
            hi, mid, lo = _split3_bf16(a_ref[rows, ks])
            b = _dot(tril, hi) + _dot(tril, mid) + _dot(tril, lo)
            ref = b[L // 2:L // 2 + 1, :]
            b_last = b[L - 1:L, :]
            q_in = (q * jnp.exp(b - ref)).astype(BF16)
            k_in = (k * jnp.exp(ref - b)).astype(BF16)
            att = jnp.where(causal, _dot_nt(q_in, k_in), 0.0)
            inner = _dot(att.astype(BF16), vb)
            state_t = s_ref[h]
            cross = _dot_nt((q * jnp.exp(b)).astype(BF16), state_t.astype(BF16))
            k_out = (k * jnp.exp(b_last - b)).astype(BF16)
            s_ref[h] = jnp.exp(b_last) * state_t + _dot_tn(vb, k_out)
            o = inner + cross
            o = o * lax.rsqrt(jnp.mean(o * o, axis=-1, keepdims=True) + EPS)
            o_ref[rows, vs] = (_silu(g_ref[rows, vs]) * o).astype(o_ref.dtype)

    for j in range(nsub):
        chunk(j)

    @pl.when(c == pl.num_programs(1) - 1)
    def _():
        for h in range(HC):
            sout_ref[0, h] = s_ref[h].T


def gla(proj, log_a, *, nbatch, seq, row0, rows_per_step, state0=None):
    lb = rows_per_step
    nc = seq // lb
    rb0 = row0 // lb
    has_state = state0 is not None
    wk = HC * DKC
    wv = HC * DVC
    assert wv == 2 * wk

    def rows(b, c):
        return rb0 + b * nc + c

    in_specs = [
        pl.BlockSpec((lb, wk), lambda b, c: (rows(b, c), 0)),
        pl.BlockSpec((lb, wk), lambda b, c: (rows(b, c), 1)),
        pl.BlockSpec((lb, wv), lambda b, c: (rows(b, c), 1)),
        pl.BlockSpec((lb, wv), lambda b, c: (rows(b, c), 2)),
        pl.BlockSpec((lb, wk), lambda b, c: (rows(b, c), 0)),
    ]
    args = [proj, proj, proj, proj, log_a]
    if has_state:
        in_specs.append(pl.BlockSpec((1, HC, DKC, DVC), lambda b, c: (b, 0, 0, 0)))
        args.append(state0)
    return pl.pallas_call(
        functools.partial(_gla_kernel, nsub=lb // CHUNK, has_state=has_state),
        out_shape=(jax.ShapeDtypeStruct((nbatch * seq, wv), BF16),
                   jax.ShapeDtypeStruct((nbatch, HC, DKC, DVC), F32)),
        grid=(nbatch, nc),
        in_specs=in_specs,
        out_specs=(pl.BlockSpec((lb, wv), lambda b, c: (b * nc + c, 0)),
                   pl.BlockSpec((1, HC, DKC, DVC), lambda b, c: (b, 0, 0, 0))),
        scratch_shapes=[pltpu.VMEM((HC, DVC, DKC), F32)],
        compiler_params=_params("parallel", "arbitrary"),
        name="gla",
    )(*args)


def _xattn_kernel(q_ref, k_ref, v_ref, o_ref):
    for h in range(HX):
        sl = slice(h * DX, (h + 1) * DX)
        q = q_ref[:, sl]
        k = k_ref[0, :, sl].astype(BF16)
        v = v_ref[0, :, sl].astype(BF16)
        s = _dot_nt(q, k) * (DX ** -0.5)
        m = jnp.max(s, axis=-1, keepdims=True)
        p = jnp.exp(s - m)
        p = p / jnp.sum(p, axis=-1, keepdims=True)
        o_ref[:, sl] = _dot(p.astype(BF16), v).astype(o_ref.dtype)


def cross_attention(q, mem_k, mem_v, *, nbatch, seq, row0, tq):
    d = q.shape[1]
    nq = seq // tq
    rb0 = row0 // tq
    return pl.pallas_call(
        _xattn_kernel,
        out_shape=jax.ShapeDtypeStruct((nbatch * seq, d), BF16),
        grid=(nbatch, nq),
        in_specs=[
            pl.BlockSpec((tq, d), lambda b, i: (rb0 + b * nq + i, 0)),
            pl.BlockSpec((1, N_MEM, d), lambda b, i: (b, 0, 0)),
            pl.BlockSpec((1, N_MEM, d), lambda b, i: (b, 0, 0)),
        ],
        out_specs=pl.BlockSpec((tq, d), lambda b, i: (b * nq + i, 0)),
        compiler_params=_params("parallel", "arbitrary"),
        name="cross_attention",
    )(q, mem_k, mem_v)


def _routing_tables(route, tm):
    n_tok = route.shape[0]
    n_rows = n_tok * TOP_K
    nb = (n_rows + N_EXPERTS * (tm - 1) + tm - 1) // tm
    flat_e = route[:, :TOP_K].astype(jnp.int32).reshape(-1)
    onehot = (flat_e[:, None] == jnp.arange(N_EXPERTS, dtype=jnp.int32)[None, :]).astype(jnp.int32)
    csum = jnp.cumsum(onehot, axis=0)
    rank = jnp.sum(csum * onehot, axis=1) - 1
    counts = csum[-1]
    padded = (counts + tm - 1) // tm * tm
    pad_end = jnp.cumsum(padded)
    pad_start = pad_end - padded
    dest = pad_start[flat_e] + rank
    tok = jnp.arange(n_rows, dtype=jnp.int32) // TOP_K
    row_tok = jnp.zeros((nb * tm,), jnp.int32).at[dest].set(tok)
    n_valid = (pad_end[-1] // tm).astype(jnp.int32)
    blk = jnp.minimum(jnp.arange(nb, dtype=jnp.int32), n_valid - 1) * tm
    block_expert = jnp.minimum(jnp.searchsorted(pad_end, blk, side='right'), N_EXPERTS - 1).astype(jnp.int32)
    dest2 = dest.reshape(n_tok, TOP_K).astype(jnp.int32)
    return row_tok, block_expert, n_valid.reshape(1), dest2[:, 0], dest2[:, 1]


def kernel(x_prompt, x_sample, mem_prompt, cache_diff_k, cache_diff_v, state_ret, state_gla, cache_mem_k, cache_mem_v, g_mix, g_xattn, g_mem, g_ffn, g_final, w_in_even, w_out_even, lambda_q1, lambda_k1, lambda_q2, lambda_k2, w_in_odd, w_gate_lr, b_gate_lr, w_out_odd, w_xq, w_xkv, w_xo, w_ffn_gu, w_ffn_dn, w_router, w_moe_gu, w_moe_dn):
    d = D_MODEL
    bp, tp, _ = x_prompt.shape
    bs, ts, _ = x_sample.shape
    past = cache_diff_k.shape[2]
    np_tok = bp * tp
    ns_tok = bs * ts
    n_tok = np_tok + ns_tok
    depth = g_mix.shape[0]

    x = jnp.concatenate([x_prompt.reshape(np_tok, d), x_sample.reshape(ns_tok, d)], axis=0)

    half = DA // 2
    inv = 1.0 / (ROPE_BASE ** jnp.linspace(0.0, 1.0, half, dtype=F32))
    pos = jnp.arange(max(tp, past + ts), dtype=jnp.int32).astype(F32)
    ang = pos[:, None] * inv[None, :]
    cos_t, sin_t = jnp.cos(ang), jnp.sin(ang)
    log_gamma = jnp.log1p(-jnp.power(2.0, -5.0 - jnp.arange(HA, dtype=F32)))

    ones_blocks = jnp.zeros((n_tok // ROW_TILE,), jnp.int32)
    all_valid = jnp.full((1,), n_tok // ROW_TILE, jnp.int32)

    ret_p, ret_s, gla_p, gla_s = [], [], [], []
    dk_p, dv_p, dk_s, dv_s = [], [], [], []
    mk_p, mv_p = [], []
    y_rows = route = d0 = d1 = None

    for i in range(depth):
        j = i // 2
        g_i = g_mix[i].reshape(1, d)
        if i % 2 == 0:
            w_in = w_in_even[j].astype(BF16)
            c_dk = 4 * HA * DA + HB * 2 * DB
            w_kv = HB * 2 * DB
            assert HB * DVB == w_kv
            proj = norm_matmul(x, g_i, w_in, col0=0, ncols=c_dk, tn=1024, out_dtype=F32, tm=PROJ_ROW_TILE)
            dk_f_p, dk_b_p, dv_f_p, dv_b_p = norm_matmul_groups(
                x, g_i, w_in, row0=0, nrows=np_tok, col0=c_dk, group_cols=w_kv, n_groups=2, out_dtypes=(F32, BF16))
            dk_f_s, dk_b_s, dv_f_s, dv_b_s = norm_matmul_groups(
                x, g_i, w_in, row0=np_tok, nrows=ns_tok, col0=c_dk, group_cols=w_kv, n_groups=2,
                out_dtypes=(F32, BF16))
            lam_init = 0.8 - 0.6 * math.exp(-0.3 * i)
            lam_params = jnp.stack([lambda_q1[j], lambda_k1[j], lambda_q2[j], lambda_k2[j]]).astype(F32)
            a_p, s_p = retention(proj, cos_t, sin_t, log_gamma, nbatch=bp, seq=tp, row0=0, pos0=0, L=RET_CHUNK)
            a_s, s_s = retention(proj, cos_t, sin_t, log_gamma, nbatch=bs, seq=ts, row0=np_tok, pos0=past,
                                 L=CHUNK, state0=state_ret[j])
            ret_p.append(s_p)
            ret_s.append(s_s)
            b_p = diff_attention_prompt(proj, dk_b_p, dv_b_p, lam_params, nbatch=bp, seq=tp, lam_init=lam_init)
            b_s = diff_attention_sample(
                proj, dk_b_s, dv_b_s, cache_diff_k[j].reshape(bs, past, HB * 2 * DB),
                cache_diff_v[j].reshape(bs, past, HB * DVB), lam_params, nbatch=bs, seq=ts, row0=np_tok,
                lam_init=lam_init)
            dk_p.append(dk_f_p.reshape(bp, tp, HB, 2, DB))
            dk_s.append(dk_f_s.reshape(bs, ts, HB, 2, DB))
            dv_p.append(dv_f_p.reshape(bp, tp, HB, DVB))
            dv_s.append(dv_f_s.reshape(bs, ts, HB, DVB))
            x = matmul_residual([(a_p, a_s), (b_p, b_s)], w_out_even[j].astype(BF16), x)
        else:
            n_main = 2 * HC * DKC + 2 * HC * DVC
            w_in = w_in_odd[j]
            proj = norm_matmul(x, g_i, w_in[:, :n_main].astype(BF16), col0=0, ncols=n_main, tn=1024, out_dtype=F32,
                               tm=PROJ_ROW_TILE)
            w_lr = jnp.zeros((d, LANES), BF16).at[:, :GATE_RANK].set(w_in[:, n_main:].astype(BF16))
            w_a2 = jnp.zeros((LANES, HC * DKC), BF16).at[:GATE_RANK].set(w_gate_lr[j].astype(BF16))
            log_a = gla_gate(x, g_i, w_lr, w_a2, b_gate_lr[j].reshape(1, -1).astype(F32))
            o_p, s_p = gla(proj, log_a, nbatch=bp, seq=tp, row0=0, rows_per_step=GLA_BLOCK)
            o_s, s_s = gla(proj, log_a, nbatch=bs, seq=ts, row0=np_tok, rows_per_step=ts, state0=state_gla[j])
            gla_p.append(s_p)
            gla_s.append(s_s)
            x = matmul_residual([(o_p, o_s)], w_out_odd[j].astype(BF16), x)

        (q,) = norm_matmul_groups(x, g_xattn[i].reshape(1, d), w_xq[i].astype(BF16), row0=0, nrows=n_tok, col0=0,
                                  group_cols=HX * DX, n_groups=1, out_dtypes=(BF16,))
        mem2d = mem_prompt.reshape(bp * N_MEM, d)
        mk, mv = norm_matmul_groups(mem2d, g_mem[i].reshape(1, d), w_xkv[i].astype(BF16), row0=0,
                                    nrows=bp * N_MEM, col0=0, group_cols=HX * DX, n_groups=2, out_dtypes=(F32,),
                                    tm=MEM_ROW_TILE)
        mk_p.append(mk.reshape(bp, N_MEM, HX, DX))
        mv_p.append(mv.reshape(bp, N_MEM, HX, DX))
        o_p = cross_attention(q, mk.reshape(bp, N_MEM, HX * DX), mv.reshape(bp, N_MEM, HX * DX),
                              nbatch=bp, seq=tp, row0=0, tq=512)
        o_s = cross_attention(q, cache_mem_k[i].reshape(bs, N_MEM, HX * DX),
                              cache_mem_v[i].reshape(bs, N_MEM, HX * DX), nbatch=bs, seq=ts, row0=np_tok, tq=ts)
        x = matmul_residual([(o_p, o_s)], w_xo[i].astype(BF16), x)

        g_f = g_ffn[i].reshape(1, d)
        if i % 2 == 0:
            x = swiglu_blocks(x, g_f, w_ffn_gu[j].astype(BF16)[None], w_ffn_dn[j].astype(BF16)[None],
                              ones_blocks, all_valid, residual=True)
        else:
            w_r = jnp.zeros((d, LANES), BF16).at[:, :N_EXPERTS].set(w_router[j].astype(BF16))
            route = router(x, g_f, w_r)
            row_tok, block_expert, n_valid, d0, d1 = _routing_tables(route, ROW_TILE)
            xs = gather_rows(x, row_tok)
            y_rows = swiglu_blocks(xs, g_f, w_moe_gu[j].astype(BF16), w_moe_dn[j].astype(BF16),
                                   block_expert, n_valid, residual=False)
            if i != depth - 1:
                raise NotImplementedError("MoE layer must be the last layer")

    g_fin = g_final.reshape(1, d)
    y_p = moe_combine_norm(x, route, g_fin, y_rows, d0, d1, row0=0, nrows=np_tok)
    y_s = moe_combine_norm(x, route, g_fin, y_rows, d0, d1, row0=np_tok, nrows=ns_tok)

    return (y_p.reshape(bp, tp, d), y_s.reshape(bs, ts, d),
            jnp.stack(dk_p), jnp.stack(dv_p), jnp.stack(ret_p), jnp.stack(gla_p),
            jnp.stack(mk_p), jnp.stack(mv_p),
            jnp.stack(dk_s), jnp.stack(dv_s), jnp.stack(ret_s), jnp.stack(gla_s))
```

```python
import functools
import math

import jax
import jax.numpy as jnp
import numpy as np
from jax import lax
from jax.experimental import pallas as pl
from jax.experimental.pallas import tpu as pltpu

F32 = jnp.float32
BF16 = jnp.bfloat16

D_MODEL = 2048
CHUNK = 64
N_MEM = 256
EPS = 1e-6
HA, DA, DVA = 4, 256, 256
ROPE_BASE = 10000.0
HB, DB, DVB = 4, 128, 256
HC, DKC, DVC = 4, 256, 512
GATE_RANK = 16
GATE_TAU = 16.0
HX, DX = 4, 512
D_FF = 5632
N_EXPERTS = 8
TOP_K = 2
NEG_BIG = -1e30

VMEM_LIMIT_BYTES = 56 * 1024 * 1024
LANES = 128

ROW_TILE = 512
PROJ_ROW_TILE = 768
PROMPT_PROJ_ROW_TILE = 1024
MEM_ROW_TILE = 256
FF_TILE = 512
COMBINE_TILE = 256
DMA_ISSUE_UNROLL = 8
ATTN_TILE = 256
RET_CHUNK = 256
GLA_BLOCK = 256


def _params(*sem):
    return pltpu.CompilerParams(dimension_semantics=sem, vmem_limit_bytes=VMEM_LIMIT_BYTES)


def _resident_spec(block_shape, index_map):
    return pl.BlockSpec(block_shape, index_map, pipeline_mode=pl.Buffered(1))


def _rms_bf16(x, g):
    ms = jnp.mean(x * x, axis=-1, keepdims=True)
    return (x * lax.rsqrt(ms + EPS) * g).astype(BF16)


def _silu(x):
    return x / (1.0 + jnp.exp(-x))


def _dot(a, b):
    return jnp.dot(a, b, preferred_element_type=F32)


def _dot_nt(a, b):
    return lax.dot_general(a, b, (((1,), (1,)), ((), ())), preferred_element_type=F32)


def _dot_tn(a, b):
    return lax.dot_general(a, b, (((0,), (0,)), ((), ())), preferred_element_type=F32)


def _norm_mm_kernel(x_ref, g_ref, w_ref, o_ref, h_ref):
    @pl.when(pl.program_id(1) == 0)
    def _():
        h_ref[...] = _rms_bf16(x_ref[...], g_ref[...])

    o_ref[...] = _dot(h_ref[...], w_ref[...]).astype(o_ref.dtype)


def norm_matmul(x, g, w, *, col0, ncols, tn, out_dtype, tm=ROW_TILE):
    m, d = x.shape
    assert m % tm == 0 and ncols % tn == 0 and col0 % tn == 0
    cb0 = col0 // tn
    return pl.pallas_call(
        _norm_mm_kernel,
        out_shape=jax.ShapeDtypeStruct((m, ncols), out_dtype),
        grid=(m // tm, ncols // tn),
        in_specs=[
            pl.BlockSpec((tm, d), lambda i, j: (i, 0)),
            pl.BlockSpec((1, d), lambda i, j: (0, 0)),
            pl.BlockSpec((d, tn), lambda i, j: (0, cb0 + j)),
        ],
        out_specs=pl.BlockSpec((tm, tn), lambda i, j: (i, j)),
        scratch_shapes=[pltpu.VMEM((tm, d), BF16)],
        compiler_params=_params("parallel", "arbitrary"),
        name="norm_matmul",
    )(x, g, w)


def _norm_mm_groups_kernel(*refs, n_groups):
    x_ref, g_ref = refs[:2]
    w_refs = refs[2:2 + n_groups]
    o_refs = refs[2 + n_groups:]
    h = _rms_bf16(x_ref[...], g_ref[...])
    outs_per_group = len(o_refs) // n_groups
    for k, w_ref in enumerate(w_refs):
        y = _dot(h, w_ref[...])
        for o_ref in o_refs[k * outs_per_group:(k + 1) * outs_per_group]:
            o_ref[...] = y.astype(o_ref.dtype)


def norm_matmul_groups(x, g, w, *, row0, nrows, col0, group_cols, n_groups, out_dtypes, tm=ROW_TILE):
    d = x.shape[1]
    assert nrows % tm == 0 and row0 % tm == 0 and col0 % group_cols == 0
    rb0 = row0 // tm
    cb0 = col0 // group_cols
    in_specs = [pl.BlockSpec((tm, d), lambda i: (rb0 + i, 0)), pl.BlockSpec((1, d), lambda i: (0, 0))]
    for k in range(n_groups):
        in_specs.append(_resident_spec((d, group_cols), lambda i, k=k: (0, cb0 + k)))
    out_shape, out_specs = [], []
    for k in range(n_groups):
        for dt in out_dtypes:
            out_shape.append(jax.ShapeDtypeStruct((nrows, group_cols), dt))
            out_specs.append(pl.BlockSpec((tm, group_cols), lambda i: (i, 0)))
    return pl.pallas_call(
        functools.partial(_norm_mm_groups_kernel, n_groups=n_groups),
        out_shape=out_shape,
        grid=(nrows // tm,),
        in_specs=in_specs,
        out_specs=out_specs,
        compiler_params=_params("parallel"),
        name="norm_matmul_groups",
    )(x, g, *([w] * n_groups))


def _mm_res_kernel(*refs, n_lhs, n_head_blocks):
    head_refs = refs[:n_lhs]
    tail_refs = refs[n_lhs:2 * n_lhs]
    w_refs = refs[2 * n_lhs:3 * n_lhs]
    r_refs = refs[3 * n_lhs:-1]
    o_ref = refs[-1]

    def compute(a_refs, r_ref):
        acc = r_ref[...]
        for a_ref, w_ref in zip(a_refs, w_refs):
            acc = acc + _dot(a_ref[...], w_ref[...])
        o_ref[...] = acc

    is_head = pl.program_id(0) < n_head_blocks

    @pl.when(is_head)
    def _():
        compute(head_refs, r_refs[0])

    @pl.when(jnp.logical_not(is_head))
    def _():
        compute(tail_refs, r_refs[-1])


def matmul_residual(lhs_pairs, w, res, *, tm=ROW_TILE):
    res_parts = tuple(res) if isinstance(res, (tuple, list)) else (res,)
    m = sum(r.shape[0] for r in res_parts)
    n = res_parts[0].shape[1]
    n_lhs = len(lhs_pairs)
    nhb = lhs_pairs[0][0].shape[0] // tm
    in_specs = []
    for head, tail in lhs_pairs:
        assert head.shape[0] == nhb * tm and tail.shape[0] == tm and (nhb + 1) * tm == m
        in_specs.append(pl.BlockSpec((tm, head.shape[1]), lambda i: (jnp.minimum(i, nhb - 1), 0)))
    for head, tail in lhs_pairs:
        in_specs.append(pl.BlockSpec((tm, tail.shape[1]), lambda i: (0, 0)))
    row = 0
    for head, _ in lhs_pairs:
        kk = head.shape[1]
        assert row % kk == 0
        rb = row // kk
        in_specs.append(_resident_spec((kk, n), lambda i, rb=rb: (rb, 0)))
        row += kk
    if len(res_parts) == 1:
        in_specs.append(pl.BlockSpec((tm, n), lambda i: (i, 0)))
    else:
        assert res_parts[0].shape[0] == nhb * tm and res_parts[1].shape[0] == tm
        in_specs.append(pl.BlockSpec((tm, n), lambda i: (jnp.minimum(i, nhb - 1), 0)))
        in_specs.append(pl.BlockSpec((tm, n), lambda i: (0, 0)))
    heads = [p[0] for p in lhs_pairs]
    tails = [p[1] for p in lhs_pairs]
    return pl.pallas_call(
        functools.partial(_mm_res_kernel, n_lhs=n_lhs, n_head_blocks=nhb),
        out_shape=jax.ShapeDtypeStruct((m, n), F32),
        grid=(m // tm,),
        in_specs=in_specs,
        out_specs=pl.BlockSpec((tm, n), lambda i: (i, 0)),
        compiler_params=_params("parallel"),
        name="matmul_residual",
    )(*heads, *tails, *([w] * n_lhs), *res_parts)


def _ffn_kernel(be_ref, nv_ref, x_ref, g_ref, wg_ref, wu_ref, wd_ref, o_ref, h_ref, *, residual, packed):
    i = pl.program_id(0)
    f = pl.program_id(1)
    valid = i < nv_ref[0]

    @pl.when(f == 0)
    def _():
        if packed:
            h_ref[...] = _unpack_bf16_pairs(x_ref[...])
            o_ref[...] = jnp.zeros_like(o_ref)
        else:
            x = x_ref[...]
            h_ref[...] = _rms_bf16(x, g_ref[...])
            o_ref[...] = x if residual else jnp.zeros_like(x)

    @pl.when(valid)
    def _():
        h = h_ref[...]
        a = _dot(h, wg_ref[0])
        u = _dot(h, wu_ref[0])
        act = (_silu(a) * u).astype(BF16)
        o_ref[...] += _dot(act, wd_ref[0])


def swiglu_blocks(x, g, w_gu, w_dn, block_expert, n_valid, *, residual, tm=ROW_TILE, tf=FF_TILE):
    m = x.shape[0]
    d = w_gu.shape[1]
    packed = x.dtype == jnp.uint32
    assert x.shape[1] == (d // 2 if packed else d) and not (packed and residual)
    ff = w_dn.shape[1]
    nf = ff // tf
    nb = m // tm

    def _x_map(i, f, be, nv):
        return (jnp.minimum(i, nv[0] - 1), 0)

    def _f_eff(i, f, nv):
        return jnp.where(i < nv[0], f, nf - 1)

    return pl.pallas_call(
        functools.partial(_ffn_kernel, residual=residual, packed=packed),
        out_shape=jax.ShapeDtypeStruct((m, d), F32),
        grid_spec=pltpu.PrefetchScalarGridSpec(
            num_scalar_prefetch=2,
            grid=(nb, nf),
            in_specs=[
                pl.BlockSpec((tm, x.shape[1]), _x_map),
                pl.BlockSpec((1, d), lambda i, f, be, nv: (0, 0)),
                pl.BlockSpec((1, d, tf), lambda i, f, be, nv: (be[i], 0, _f_eff(i, f, nv))),
                pl.BlockSpec((1, d, tf), lambda i, f, be, nv: (be[i], 0, nf + _f_eff(i, f, nv))),
                pl.BlockSpec((1, tf, d), lambda i, f, be, nv: (be[i], _f_eff(i, f, nv), 0)),
            ],
            out_specs=pl.BlockSpec((tm, d), lambda i, f, be, nv: (i, 0)),
            scratch_shapes=[pltpu.VMEM((tm, d), BF16)],
        ),
        compiler_params=_params("parallel", "arbitrary"),
        name="swiglu_blocks",
    )(block_expert, n_valid, x, g, w_gu, w_gu, w_dn)


def _router_kernel(x_ref, g_ref, w_ref, o_ref):
    h = _rms_bf16(x_ref[...], g_ref[...])
    logits = _dot(h, w_ref[...])
    lane = lax.broadcasted_iota(jnp.int32, logits.shape, 1).astype(F32)
    l1 = jnp.where(lane < N_EXPERTS, logits, NEG_BIG)
    m1 = jnp.max(l1, axis=-1, keepdims=True)
    i1 = jnp.min(jnp.where(l1 == m1, lane, float(LANES)), axis=-1, keepdims=True)
    l2 = jnp.where(lane == i1, NEG_BIG, l1)
    m2 = jnp.max(l2, axis=-1, keepdims=True)
    i2 = jnp.min(jnp.where(l2 == m2, lane, float(LANES)), axis=-1, keepdims=True)
    e = jnp.exp(m2 - m1)
    g1 = 1.0 / (1.0 + e)
    g2 = e / (1.0 + e)
    out = jnp.where(lane == 0.0, i1,
                    jnp.where(lane == 1.0, i2,
                              jnp.where(lane == 2.0, g1, jnp.where(lane == 3.0, g2, 0.0))))
    o_ref[...] = out


def router(x, g, w_pad, *, tm=ROW_TILE):
    m, d = x.shape
    return pl.pallas_call(
        _router_kernel,
        out_shape=jax.ShapeDtypeStruct((m, LANES), F32),
        grid=(m // tm,),
        in_specs=[
            pl.BlockSpec((tm, d), lambda i: (i, 0)),
            pl.BlockSpec((1, d), lambda i: (0, 0)),
            pl.BlockSpec((d, LANES), lambda i: (0, 0)),
        ],
        out_specs=pl.BlockSpec((tm, LANES), lambda i: (i, 0)),
        compiler_params=_params("parallel"),
        name="router",
    )(x, g, w_pad)


def _row_copy(src, dst, src_row, dst_row, sem):
    return pltpu.make_async_copy(src.at[pl.ds(src_row, 1)], dst.at[pl.ds(dst_row, 1)], sem)


def _pack_bf16_pairs(h):
    half = h.shape[-1] // 2
    lo = lax.bitcast_convert_type(h[:, :half].astype(F32), jnp.uint32) >> 16
    hi = lax.bitcast_convert_type(h[:, half:].astype(F32), jnp.uint32) & jnp.uint32(0xFFFF0000)
    return lo | hi


def _unpack_bf16_pairs(u):
    lo = lax.bitcast_convert_type(u << 16, F32)
    hi = lax.bitcast_convert_type(u & jnp.uint32(0xFFFF0000), F32)
    return jnp.concatenate([lo, hi], axis=-1).astype(BF16)


def _dispatch_kernel(d0_ref, d1_ref, plo_ref, phi_ref, x_ref, g_ref, xs_hbm, hbuf, zrow, sem, zsem, *, tm):
    i = pl.program_id(0)
    last = pl.num_programs(0) - 1
    slot = i % 2
    base = i * tm

    def zero_copy(r):
        return pltpu.make_async_copy(zrow.at[pl.ds(0, 1)], xs_hbm.at[pl.ds(r, 1)], zsem)

    def for_each_unrouted_row(fn):
        for e in range(N_EXPERTS):
            def body(r, c):
                fn(r)
                return c
            lax.fori_loop(plo_ref[e], phi_ref[e], body, 0)

    @pl.when(i == 0)
    def _():
        zrow[...] = jnp.zeros_like(zrow)
        for_each_unrouted_row(lambda r: zero_copy(r).start())

    hbuf[slot] = _pack_bf16_pairs(_rms_bf16(x_ref[...], g_ref[...]))

    def start(r, c):
        _row_copy(hbuf.at[slot], xs_hbm, r, d0_ref[base + r], sem.at[slot]).start()
        _row_copy(hbuf.at[slot], xs_hbm, r, d1_ref[base + r], sem.at[slot]).start()
        return c

    lax.fori_loop(0, tm, start, 0, unroll=DMA_ISSUE_UNROLL)

    def wait_slot(s):
        for _ in range(TOP_K):
            pltpu.make_async_copy(hbuf.at[s], xs_hbm.at[pl.ds(0, tm)], sem.at[s]).wait()

    @pl.when(i >= 1)
    def _():
        wait_slot(1 - slot)

    @pl.when(i == last)
    def _():
        wait_slot(slot)
        for_each_unrouted_row(lambda r: zero_copy(r).wait())


def moe_dispatch(x, g, d0, d1, pad_lo, pad_hi, n_rows, *, tm=ROW_TILE):
    m, d = x.shape
    return pl.pallas_call(
        functools.partial(_dispatch_kernel, tm=tm),
        out_shape=jax.ShapeDtypeStruct((n_rows, d // 2), jnp.uint32),
        grid_spec=pltpu.PrefetchScalarGridSpec(
            num_scalar_prefetch=4,
            grid=(m // tm,),
            in_specs=[
                pl.BlockSpec((tm, d), lambda i, *_: (i, 0)),
                pl.BlockSpec((1, d), lambda i, *_: (0, 0)),
            ],
            out_specs=pl.BlockSpec(memory_space=pl.ANY),
            scratch_shapes=[pltpu.VMEM((2, tm, d // 2), jnp.uint32), pltpu.VMEM((8, d // 2), jnp.uint32),
                            pltpu.SemaphoreType.DMA((2,)), pltpu.SemaphoreType.DMA],
        ),
        compiler_params=_params("arbitrary"),
        name="moe_dispatch",
    )(d0, d1, pad_lo, pad_hi, x, g)


def _combine_kernel(d0_ref, d1_ref, x_ref, r_ref, g_ref, y_hbm, o_ref, buf, sem, *, tm, row0):
    i = pl.program_id(0)
    slot = i % 2

    def issue(block, slot_):
        base = row0 + block * tm

        def start(r, c):
            _row_copy(y_hbm, buf.at[slot_], d0_ref[base + r], r, sem.at[slot_]).start()
            _row_copy(y_hbm, buf.at[slot_], d1_ref[base + r], tm + r, sem.at[slot_]).start()
            return c

        lax.fori_loop(0, tm, start, 0, unroll=DMA_ISSUE_UNROLL)

    @pl.when(i == 0)
    def _():
        issue(0, 0)

    @pl.when(i + 1 < pl.num_programs(0))
    def _():
        issue(i + 1, 1 - slot)

    pltpu.make_async_copy(y_hbm.at[pl.ds(0, 2 * tm)], buf.at[slot], sem.at[slot]).wait()
    rt = r_ref[...]
    g0 = rt[:, 2:3]
    g1 = rt[:, 3:4]
    x = x_ref[...] + (buf[slot, 0:tm, :] * g0 + buf[slot, tm:2 * tm, :] * g1)
    ms = jnp.mean(x * x, axis=-1, keepdims=True)
    o_ref[...] = x * lax.rsqrt(ms + EPS) * g_ref[...]


def moe_combine_norm(x, route, g, y_rows, d0, d1, *, row0, nrows, tm=COMBINE_TILE):
    d = x.shape[1]
    rb0 = row0 // tm
    return pl.pallas_call(
        functools.partial(_combine_kernel, tm=tm, row0=row0),
        out_shape=jax.ShapeDtypeStruct((nrows, d), F32),
        grid_spec=pltpu.PrefetchScalarGridSpec(
            num_scalar_prefetch=2,
            grid=(nrows // tm,),
            in_specs=[
                pl.BlockSpec((tm, d), lambda i, a, b: (rb0 + i, 0)),
                pl.BlockSpec((tm, LANES), lambda i, a, b: (rb0 + i, 0)),
                pl.BlockSpec((1, d), lambda i, a, b: (0, 0)),
                pl.BlockSpec(memory_space=pl.ANY),
            ],
            out_specs=pl.BlockSpec((tm, d), lambda i, a, b: (i, 0)),
            scratch_shapes=[pltpu.VMEM((2, 2 * tm, d), F32), pltpu.SemaphoreType.DMA((2,))],
        ),
        compiler_params=_params("arbitrary"),
        name="moe_combine",
    )(d0, d1, x, route, g, y_rows)


def _gate_kernel(x_ref, g_ref, wlr_ref, wa2_ref, b_ref, o_ref):
    h = _rms_bf16(x_ref[...], g_ref[...])
    a_lr = _dot(h, wlr_ref[...])
    z = _dot(a_lr.astype(BF16), wa2_ref[...]) + b_ref[...]
    o_ref[...] = (jnp.minimum(z, 0.0) - jnp.log(1.0 + jnp.exp(-jnp.abs(z)))) * (1.0 / GATE_TAU)


def gla_gate(x, g, w_lr_pad, w_a2_pad, b_a, *, tm=ROW_TILE):
    m, d = x.shape
    n = w_a2_pad.shape[1]
    return pl.pallas_call(
        _gate_kernel,
        out_shape=jax.ShapeDtypeStruct((m, n), F32),
        grid=(m // tm,),
        in_specs=[
            pl.BlockSpec((tm, d), lambda i: (i, 0)),
            pl.BlockSpec((1, d), lambda i: (0, 0)),
            pl.BlockSpec((d, LANES), lambda i: (0, 0)),
            pl.BlockSpec((LANES, n), lambda i: (0, 0)),
            pl.BlockSpec((1, n), lambda i: (0, 0)),
        ],
        out_specs=pl.BlockSpec((tm, n), lambda i: (i, 0)),
        compiler_params=_params("parallel"),
        name="gla_gate",
    )(x, g, w_lr_pad, w_a2_pad, b_a)


def _rotary(x, cos, sin):
    half = x.shape[-1] // 2
    x1, x2 = x[:, :half], x[:, half:]
    return jnp.concatenate([x1 * cos - x2 * sin, x1 * sin + x2 * cos], axis=-1)


def _retention_kernel(lg_ref, *refs, L, has_state):
    if has_state:
        q_ref, k_ref, v_ref, g_ref, cos_ref, sin_ref, s0_ref, o_ref, sout_ref, s_ref = refs
    else:
        q_ref, k_ref, v_ref, g_ref, cos_ref, sin_ref, o_ref, sout_ref, s_ref = refs
    c = pl.program_id(1)

    @pl.when(c == 0)
    def _():
        if has_state:
            s_ref[...] = s0_ref[0]
        else:
            s_ref[...] = jnp.zeros_like(s_ref)

    cos = cos_ref[...]
    sin = sin_ref[...]
    n_col = lax.broadcasted_iota(jnp.int32, (L, 1), 0).astype(F32)
    n_row = lax.broadcasted_iota(jnp.int32, (1, L), 1).astype(F32)
    diff = n_col - n_row
    for h in range(HA):
        lg = lg_ref[h]
        qs = slice(h * DA, (h + 1) * DA)
        vs = slice(h * DVA, (h + 1) * DVA)
        qr = _rotary(q_ref[:, qs], cos, sin)
        kr = _rotary(k_ref[:, qs], cos, sin) * (DA ** -0.5)
        vb = v_ref[:, vs].astype(BF16)
        decay = jnp.where(diff >= 0.0, jnp.exp(jnp.maximum(diff, 0.0) * lg), 0.0)
        qb = qr.astype(BF16)
        scores = _dot_nt(qb, kr.astype(BF16)) * decay
        inner = _dot(scores.astype(BF16), vb)
        state = s_ref[h]
        cross = _dot(qb, state.astype(BF16)) * jnp.exp((n_col + 1.0) * lg)
        k_dec = (kr * jnp.exp((L - 1.0 - n_col) * lg)).astype(BF16)
        s_ref[h] = state * jnp.exp(jnp.zeros((1, 1), F32) + L * lg) + _dot_tn(k_dec, vb)
        ret = inner + cross
        ret = ret - jnp.mean(ret, axis=-1, keepdims=True)
        ret = ret * lax.rsqrt(jnp.mean(ret * ret, axis=-1, keepdims=True) + EPS)
        o_ref[:, vs] = (_silu(g_ref[:, vs]) * ret).astype(o_ref.dtype)

    @pl.when(c == pl.num_programs(1) - 1)
    def _():
        sout_ref[0] = s_ref[...]


def retention(proj, cos, sin, log_gamma, *, nbatch, seq, row0, pos0, L, state0=None):
    nc = seq // L
    rb0 = row0 // L
    pb0 = pos0 // L
    has_state = state0 is not None
    wq = HA * DA
    assert HA * DVA == wq

    def col(k):
        return lambda b, c, lg: (rb0 + b * nc + c, k)

    in_specs = [
        pl.BlockSpec((L, wq), col(0)),
        pl.BlockSpec((L, wq), col(1)),
        pl.BlockSpec((L, wq), col(2)),
        pl.BlockSpec((L, wq), col(3)),
        pl.BlockSpec((L, DA // 2), lambda b, c, lg: (pb0 + c, 0)),
        pl.BlockSpec((L, DA // 2), lambda b, c, lg: (pb0 + c, 0)),
    ]
    args = [proj, proj, proj, proj, cos, sin]
    if has_state:
        in_specs.append(pl.BlockSpec((1, HA, DA, DVA), lambda b, c, lg: (b, 0, 0, 0)))
        args.append(state0)
    return pl.pallas_call(
        functools.partial(_retention_kernel, L=L, has_state=has_state),
        out_shape=(jax.ShapeDtypeStruct((nbatch * seq, HA * DVA), BF16),
                   jax.ShapeDtypeStruct((nbatch, HA, DA, DVA), F32)),
        grid_spec=pltpu.PrefetchScalarGridSpec(
            num_scalar_prefetch=1,
            grid=(nbatch, nc),
            in_specs=in_specs,
            out_specs=(pl.BlockSpec((L, wq), lambda b, c, lg: (b * nc + c, 0)),
                       pl.BlockSpec((1, HA, DA, DVA), lambda b, c, lg: (b, 0, 0, 0))),
            scratch_shapes=[pltpu.VMEM((HA, DA, DVA), F32)],
        ),
        compiler_params=_params("parallel", "arbitrary"),
        name="retention",
    )(log_gamma, *args)


def _lambda_value(l_ref, lam_init):
    lv = l_ref[...]
    a = jnp.sum(lv[0:1] * lv[1:2], axis=-1, keepdims=True)
    b = jnp.sum(lv[2:3] * lv[3:4], axis=-1, keepdims=True)
    return jnp.exp(a) - jnp.exp(b) + lam_init


def _head_norm_scale(o, scale):
    return o * lax.rsqrt(jnp.mean(o * o, axis=-1, keepdims=True) + EPS) * scale


def _diff_softmax_pv(q, key_parts, val_parts, masks, lam):
    w = None
    for c in range(2):
        qc = q[:, c * DB:(c + 1) * DB].astype(BF16)
        s = [_dot_nt(qc, k[:, c * DB:(c + 1) * DB]) for k in key_parts]
        s = [x if m is None else jnp.where(m, x, NEG_BIG) for x, m in zip(s, masks)]
        mx = functools.reduce(jnp.maximum, [jnp.max(x, axis=-1, keepdims=True) for x in s])
        p = [jnp.exp(x - mx) for x in s]
        inv = 1.0 / functools.reduce(lambda a, b: a + b, [jnp.sum(x, axis=-1, keepdims=True) for x in p])
        if c == 0:
            w = [x * inv for x in p]
        else:
            w = [a - lam * (x * inv) for a, x in zip(w, p)]
    outs = [_dot(a.astype(BF16), v) for a, v in zip(w, val_parts)]
    return functools.reduce(lambda a, b: a + b, outs)


def _dattn_prompt_kernel(q_ref, k_ref, v_ref, l_ref, o_ref, *, tq, nq, lam_init):
    i = pl.program_id(2)
    lam = _lambda_value(l_ref, lam_init)
    r_chunk = lax.broadcasted_iota(jnp.int32, (tq, tq), 0) // CHUNK
    c_chunk = lax.broadcasted_iota(jnp.int32, (tq, tq), 1) // CHUNK
    diag_mask = c_chunk <= r_chunk

    for n in range(nq):
        @pl.when(i == n)
        def _(n=n):
            q = q_ref[...] * (DB ** -0.5)
            lo = n * tq
            keys = [k_ref[lo:lo + tq, :]]
            vals = [v_ref[lo:lo + tq, :]]
            masks = [diag_mask]
            if n > 0:
                keys.insert(0, k_ref[0:lo, :])
                vals.insert(0, v_ref[0:lo, :])
                masks.insert(0, None)
            o = _diff_softmax_pv(q, keys, vals, masks, lam)
            o_ref[...] = _head_norm_scale(o, 1.0 - lam_init).astype(o_ref.dtype)


def diff_attention_prompt(proj, k_bf, v_bf, lam_params, *, nbatch, seq, lam_init, tq=ATTN_TILE):
    nq = seq // tq
    assert tq % CHUNK == 0
    return pl.pallas_call(
        functools.partial(_dattn_prompt_kernel, tq=tq, nq=nq, lam_init=lam_init),
        out_shape=jax.ShapeDtypeStruct((nbatch * seq, HB * DVB), BF16),
        grid=(nbatch, HB, nq),
        in_specs=[
            pl.BlockSpec((tq, 2 * DB), lambda b, h, i: (b * nq + i, 4 * HA + h)),
            pl.BlockSpec((seq, 2 * DB), lambda b, h, i: (b, h)),
            pl.BlockSpec((seq, DVB), lambda b, h, i: (b, h)),
            pl.BlockSpec((4, DB), lambda b, h, i: (0, 0)),
        ],
        out_specs=pl.BlockSpec((tq, DVB), lambda b, h, i: (b * nq + i, h)),
        compiler_params=_params("parallel", "parallel", "arbitrary"),
        name="diff_attention_prompt",
    )(proj, k_bf, v_bf, lam_params)


def _dattn_sample_kernel(q_ref, kn_ref, vn_ref, kc_ref, vc_ref, l_ref, o_ref, *, lam_init):
    q = q_ref[...] * (DB ** -0.5)
    lam = _lambda_value(l_ref, lam_init)
    keys = [kc_ref[0].astype(BF16), kn_ref[...]]
    vals = [vc_ref[0].astype(BF16), vn_ref[...]]
    o = _diff_softmax_pv(q, keys, vals, [None, None], lam)
    o_ref[...] = _head_norm_scale(o, 1.0 - lam_init).astype(o_ref.dtype)


def diff_attention_sample(proj, k_new, v_new, cache_k, cache_v, lam_params, *, nbatch, seq, row0, lam_init):
    past = cache_k.shape[1]
    assert seq == CHUNK and past % CHUNK == 0 and row0 % seq == 0
    rb0 = row0 // seq
    return pl.pallas_call(
        functools.partial(_dattn_sample_kernel, lam_init=lam_init),
        out_shape=jax.ShapeDtypeStruct((nbatch * seq, HB * DVB), BF16),
        grid=(nbatch, HB),
        in_specs=[
            pl.BlockSpec((seq, 2 * DB), lambda b, h: (rb0 + b, 4 * HA + h)),
            pl.BlockSpec((seq, 2 * DB), lambda b, h: (b, h)),
            pl.BlockSpec((seq, DVB), lambda b, h: (b, h)),
            pl.BlockSpec((1, past, 2 * DB), lambda b, h: (b, 0, h)),
            pl.BlockSpec((1, past, DVB), lambda b, h: (b, 0, h)),
            pl.BlockSpec((4, DB), lambda b, h: (0, 0)),
        ],
        out_specs=pl.BlockSpec((seq, DVB), lambda b, h: (b, h)),
        compiler_params=_params("parallel", "parallel"),
        name="diff_attention_sample",
    )(proj, k_new, v_new, cache_k, cache_v, lam_params)


def _split3_bf16(x):
    hi = x.astype(BF16)
    r1 = x - hi.astype(F32)
    mid = r1.astype(BF16)
    lo = (r1 - mid.astype(F32)).astype(BF16)
    return hi, mid, lo


def _gla_kernel(*refs, nsub, has_state):
    if has_state:
        q_ref, k_ref, v_ref, g_ref, a_ref, s0_ref, o_ref, sout_ref, s_ref = refs
    else:
        q_ref, k_ref, v_ref, g_ref, a_ref, o_ref, sout_ref, s_ref = refs
    c = pl.program_id(1)
    L = CHUNK

    @pl.when(c == 0)
    def _():
        for h in range(HC):
            if has_state:
                s_ref[h] = s0_ref[0, h].T
            else:
                s_ref[h] = jnp.zeros((DVC, DKC), F32)

    row = lax.broadcasted_iota(jnp.int32, (L, L), 0)
    colm = lax.broadcasted_iota(jnp.int32, (L, L), 1)
    causal = colm <= row
    tril = jnp.where(causal, 1.0, 0.0).astype(BF16)

    def chunk(j):
        rows = pl.ds(j * L, L)
        for h in range(HC):
            ks = slice(h * DKC, (h + 1) * DKC)
            vs = slice(h * DVC, (h + 1) * DVC)
            q = q_ref[rows, ks] * (DKC ** -0.5)
            k = k_ref[rows, ks]
            vb = v_ref[rows, vs].astype(BF16)
            hi, mid, lo = _split3_bf16(a_ref[rows, ks])
            b = _dot(tril, hi) + _dot(tril, mid) + _dot(tril, lo)
            ref = b[L // 2:L // 2 + 1, :]
            b_last = b[L - 1:L, :]
            q_in = (q * jnp.exp(b - ref)).astype(BF16)
            k_in = (k * jnp.exp(ref - b)).astype(BF16)
            att = jnp.where(causal, _dot_nt(q_in, k_in), 0.0)
            inner = _dot(att.astype(BF16), vb)
            state_t = s_ref[h]
            cross = _dot_nt((q * jnp.exp(b)).astype(BF16), state_t.astype(BF16))
            k_out = (k * jnp.exp(b_last - b)).astype(BF16)
            s_ref[h] = jnp.exp(b_last) * state_t + _dot_tn(vb, k_out)
            o = inner + cross
            o = o * lax.rsqrt(jnp.mean(o * o, axis=-1, keepdims=True) + EPS)
            o_ref[rows, vs] = (_silu(g_ref[rows, vs]) * o).astype(o_ref.dtype)

    for j in range(nsub):
        chunk(j)

    @pl.when(c == pl.num_programs(1) - 1)
    def _():
        for h in range(HC):
            sout_ref[0, h] = s_ref[h].T


def gla(proj, log_a, *, nbatch, seq, row0, rows_per_step, state0=None):
    lb = rows_per_step
    nc = seq // lb
    rb0 = row0 // lb
    has_state = state0 is not None
    wk = HC * DKC
    wv = HC * DVC
    assert wv == 2 * wk

    def rows(b, c):
        return rb0 + b * nc + c

    in_specs = [
        pl.BlockSpec((lb, wk), lambda b, c: (rows(b, c), 0)),
        pl.BlockSpec((lb, wk), lambda b, c: (rows(b, c), 1)),
        pl.BlockSpec((lb, wv), lambda b, c: (rows(b, c), 1)),
        pl.BlockSpec((lb, wv), lambda b, c: (rows(b, c), 2)),
        pl.BlockSpec((lb, wk), lambda b, c: (rows(b, c), 0)),
    ]
    args = [proj, proj, proj, proj, log_a]
    if has_state:
        in_specs.append(pl.BlockSpec((1, HC, DKC, DVC), lambda b, c: (b, 0, 0, 0)))
        args.append(state0)
    return pl.pallas_call(
        functools.partial(_gla_kernel, nsub=lb // CHUNK, has_state=has_state),
        out_shape=(jax.ShapeDtypeStruct((nbatch * seq, wv), BF16),
                   jax.ShapeDtypeStruct((nbatch, HC, DKC, DVC), F32)),
        grid=(nbatch, nc),
        in_specs=in_specs,
        out_specs=(pl.BlockSpec((lb, wv), lambda b, c: (b * nc + c, 0)),
                   pl.BlockSpec((1, HC, DKC, DVC), lambda b, c: (b, 0, 0, 0))),
        scratch_shapes=[pltpu.VMEM((HC, DVC, DKC), F32)],
        compiler_params=_params("parallel", "arbitrary"),
        name="gla",
    )(*args)


def _xattn_kernel(q_ref, k_ref, v_ref, o_ref):
    for h in range(HX):
        sl = slice(h * DX, (h + 1) * DX)
        q = q_ref[:, sl]
        k = k_ref[0, :, sl].astype(BF16)
        v = v_ref[0, :, sl].astype(BF16)
        s = _dot_nt(q, k) * (DX ** -0.5)
        m = jnp.max(s, axis=-1, keepdims=True)
        p = jnp.exp(s - m)
        p = p / jnp.sum(p, axis=-1, keepdims=True)
        o_ref[:, sl] = _dot(p.astype(BF16), v).astype(o_ref.dtype)


def cross_attention(q, mem_k, mem_v, *, nbatch, seq, row0, tq):
    d = q.shape[1]
    nq = seq // tq
    rb0 = row0 // tq
    return pl.pallas_call(
        _xattn_kernel,
        out_shape=jax.ShapeDtypeStruct((nbatch * seq, d), BF16),
        grid=(nbatch, nq),
        in_specs=[
            pl.BlockSpec((tq, d), lambda b, i: (rb0 + b * nq + i, 0)),
            pl.BlockSpec((1, N_MEM, d), lambda b, i: (b, 0, 0)),
            pl.BlockSpec((1, N_MEM, d), lambda b, i: (b, 0, 0)),
        ],
        out_specs=pl.BlockSpec((tq, d), lambda b, i: (b * nq + i, 0)),
        compiler_params=_params("parallel", "arbitrary"),
        name="cross_attention",
    )(q, mem_k, mem_v)


def _routing_tables(route, tm):
    n_tok = route.shape[0]
    n_rows = n_tok * TOP_K
    nb = (n_rows + N_EXPERTS * (tm - 1) + tm - 1) // tm
    flat_e = route[:, :TOP_K].astype(jnp.int32).reshape(-1)
    onehot = (flat_e[:, None] == jnp.arange(N_EXPERTS, dtype=jnp.int32)[None, :]).astype(jnp.int32)
    csum = jnp.cumsum(onehot, axis=0)
    rank = jnp.sum(csum * onehot, axis=1) - 1
    counts = csum[-1]
    padded = (counts + tm - 1) // tm * tm
    pad_end = jnp.cumsum(padded)
    pad_start = pad_end - padded
    dest = jnp.sum(onehot * pad_start[None, :], axis=1) + rank
    n_valid = (pad_end[-1] // tm).astype(jnp.int32)
    blk = jnp.minimum(jnp.arange(nb, dtype=jnp.int32), n_valid - 1) * tm
    block_expert = jnp.minimum(jnp.searchsorted(pad_end, blk, side='right'), N_EXPERTS - 1).astype(jnp.int32)
    dest2 = dest.reshape(n_tok, TOP_K).astype(jnp.int32)
    pad_lo = (pad_start + counts).astype(jnp.int32)
    pad_hi = pad_end.astype(jnp.int32).at[N_EXPERTS - 1].set(nb * tm)
    return block_expert, n_valid.reshape(1), dest2[:, 0], dest2[:, 1], pad_lo, pad_hi, nb * tm


def kernel(x_prompt, x_sample, mem_prompt, cache_diff_k, cache_diff_v, state_ret, state_gla, cache_mem_k, cache_mem_v, g_mix, g_xattn, g_mem, g_ffn, g_final, w_in_even, w_out_even, lambda_q1, lambda_k1, lambda_q2, lambda_k2, w_in_odd, w_gate_lr, b_gate_lr, w_out_odd, w_xq, w_xkv, w_xo, w_ffn_gu, w_ffn_dn, w_router, w_moe_gu, w_moe_dn):
    d = D_MODEL
    bp, tp, _ = x_prompt.shape
    bs, ts, _ = x_sample.shape
    past = cache_diff_k.shape[2]
    np_tok = bp * tp
    ns_tok = bs * ts
    n_tok = np_tok + ns_tok
    depth = g_mix.shape[0]

    x = (x_prompt.reshape(np_tok, d), x_sample.reshape(ns_tok, d))

    half = DA // 2
    inv = 1.0 / (ROPE_BASE ** jnp.linspace(0.0, 1.0, half, dtype=F32))
    pos = jnp.arange(max(tp, past + ts), dtype=jnp.int32).astype(F32)
    ang = pos[:, None] * inv[None, :]
    cos_t, sin_t = jnp.cos(ang), jnp.sin(ang)
    log_gamma = jnp.log1p(-jnp.power(2.0, -5.0 - jnp.arange(HA, dtype=F32)))

    ones_blocks = jnp.zeros((n_tok // ROW_TILE,), jnp.int32)
    all_valid = jnp.full((1,), n_tok // ROW_TILE, jnp.int32)

    ret_p, ret_s, gla_p, gla_s = [], [], [], []
    dk_p, dv_p, dk_s, dv_s = [], [], [], []
    mk_p, mv_p = [], []
    y_rows = route = d0 = d1 = None

    for i in range(depth):
        j = i // 2
        g_i = g_mix[i].reshape(1, d)
        if i % 2 == 0:
            w_in = w_in_even[j].astype(BF16)
            c_dk = 4 * HA * DA + HB * 2 * DB
            w_kv = HB * 2 * DB
            assert HB * DVB == w_kv
            if isinstance(x, tuple):
                (x_p, x_s), off_s = x, 0
                proj_p = norm_matmul(x_p, g_i, w_in, col0=0, ncols=c_dk, tn=1024, out_dtype=F32,
                                     tm=PROMPT_PROJ_ROW_TILE)
                proj_s = norm_matmul(x_s, g_i, w_in, col0=0, ncols=c_dk, tn=1024, out_dtype=F32)
            else:
                x_p, x_s, off_s = x, x, np_tok
                proj_p = proj_s = norm_matmul(x, g_i, w_in, col0=0, ncols=c_dk, tn=1024, out_dtype=F32,
                                              tm=PROJ_ROW_TILE)
            dk_f_p, dk_b_p, dv_f_p, dv_b_p = norm_matmul_groups(
                x_p, g_i, w_in, row0=0, nrows=np_tok, col0=c_dk, group_cols=w_kv, n_groups=2,
                out_dtypes=(F32, BF16))
            dk_f_s, dk_b_s, dv_f_s, dv_b_s = norm_matmul_groups(
                x_s, g_i, w_in, row0=off_s, nrows=ns_tok, col0=c_dk, group_cols=w_kv, n_groups=2,
                out_dtypes=(F32, BF16))
            lam_init = 0.8 - 0.6 * math.exp(-0.3 * i)
            lam_params = jnp.stack([lambda_q1[j], lambda_k1[j], lambda_q2[j], lambda_k2[j]]).astype(F32)
            a_p, s_p = retention(proj_p, cos_t, sin_t, log_gamma, nbatch=bp, seq=tp, row0=0, pos0=0, L=RET_CHUNK)
            a_s, s_s = retention(proj_s, cos_t, sin_t, log_gamma, nbatch=bs, seq=ts, row0=off_s, pos0=past,
                                 L=CHUNK, state0=state_ret[j])
            ret_p.append(s_p)
            ret_s.append(s_s)
            b_p = diff_attention_prompt(proj_p, dk_b_p, dv_b_p, lam_params, nbatch=bp, seq=tp, lam_init=lam_init)
            b_s = diff_attention_sample(
                proj_s, dk_b_s, dv_b_s, cache_diff_k[j].reshape(bs, past, HB * 2 * DB),
                cache_diff_v[j].reshape(bs, past, HB * DVB), lam_params, nbatch=bs, seq=ts, row0=off_s,
                lam_init=lam_init)
            dk_p.append(dk_f_p.reshape(bp, tp, HB, 2, DB))
            dk_s.append(dk_f_s.reshape(bs, ts, HB, 2, DB))
            dv_p.append(dv_f_p.reshape(bp, tp, HB, DVB))
            dv_s.append(dv_f_s.reshape(bs, ts, HB, DVB))
            x = matmul_residual([(a_p, a_s), (b_p, b_s)], w_out_even[j].astype(BF16), x)
        else:
            n_main = 2 * HC * DKC + 2 * HC * DVC
            w_in = w_in_odd[j]
            proj = norm_matmul(x, g_i, w_in[:, :n_main].astype(BF16), col0=0, ncols=n_main, tn=1024, out_dtype=F32,
                               tm=PROJ_ROW_TILE)
            w_lr = jnp.zeros((d, LANES), BF16).at[:, :GATE_RANK].set(w_in[:, n_main:].astype(BF16))
            w_a2 = jnp.zeros((LANES, HC * DKC), BF16).at[:GATE_RANK].set(w_gate_lr[j].astype(BF16))
            log_a = gla_gate(x, g_i, w_lr, w_a2, b_gate_lr[j].reshape(1, -1).astype(F32))
            o_p, s_p = gla(proj, log_a, nbatch=bp, seq=tp, row0=0, rows_per_step=GLA_BLOCK)
            o_s, s_s = gla(proj, log_a, nbatch=bs, seq=ts, row0=np_tok, rows_per_step=ts, state0=state_gla[j])
            gla_p.append(s_p)
            gla_s.append(s_s)
            x = matmul_residual([(o_p, o_s)], w_out_odd[j].astype(BF16), x)

        (q,) = norm_matmul_groups(x, g_xattn[i].reshape(1, d), w_xq[i].astype(BF16), row0=0, nrows=n_tok, col0=0,
                                  group_cols=HX * DX, n_groups=1, out_dtypes=(BF16,))
        mem2d = mem_prompt.reshape(bp * N_MEM, d)
        mk, mv = norm_matmul_groups(mem2d, g_mem[i].reshape(1, d), w_xkv[i].astype(BF16), row0=0,
                                    nrows=bp * N_MEM, col0=0, group_cols=HX * DX, n_groups=2, out_dtypes=(F32,),
                                    tm=MEM_ROW_TILE)
        mk_p.append(mk.reshape(bp, N_MEM, HX, DX))
        mv_p.append(mv.reshape(bp, N_MEM, HX, DX))
        o_p = cross_attention(q, mk.reshape(bp, N_MEM, HX * DX), mv.reshape(bp, N_MEM, HX * DX),
                              nbatch=bp, seq=tp, row0=0, tq=512)
        o_s = cross_attention(q, cache_mem_k[i].reshape(bs, N_MEM, HX * DX),
                              cache_mem_v[i].reshape(bs, N_MEM, HX * DX), nbatch=bs, seq=ts, row0=np_tok, tq=ts)
        x = matmul_residual([(o_p, o_s)], w_xo[i].astype(BF16), x)

        g_f = g_ffn[i].reshape(1, d)
        if i % 2 == 0:
            x = swiglu_blocks(x, g_f, w_ffn_gu[j].astype(BF16)[None], w_ffn_dn[j].astype(BF16)[None],
                              ones_blocks, all_valid, residual=True)
        else:
            w_r = jnp.zeros((d, LANES), BF16).at[:, :N_EXPERTS].set(w_router[j].astype(BF16))
            route = router(x, g_f, w_r)
            block_expert, n_valid, d0, d1, pad_lo, pad_hi, n_rows = _routing_tables(route, ROW_TILE)
            xs = moe_dispatch(x, g_f, d0, d1, pad_lo, pad_hi, n_rows)
            y_rows = swiglu_blocks(xs, g_f, w_moe_gu[j].astype(BF16), w_moe_dn[j].astype(BF16),
                                   block_expert, n_valid, residual=False)
            if i != depth - 1:
                raise NotImplementedError("MoE layer must be the last layer")

    g_fin = g_final.reshape(1, d)
    y_p = moe_combine_norm(x, route, g_fin, y_rows, d0, d1, row0=0, nrows=np_tok)
    y_s = moe_combine_norm(x, route, g_fin, y_rows, d0, d1, row0=np_tok, nrows=ns_tok)

    return (y_p.reshape(bp, tp, d), y_s.reshape(bs, ts, d),
            jnp.stack(dk_p), jnp.stack(dv_p), jnp.stack(ret_p), jnp.stack(gla_p),
            jnp.stack(mk_p), jnp.stack(mv_p),
            jnp.stack(dk_s), jnp.stack(dv_s), jnp.stack(ret_s), jnp.stack(gla_s))
```

```python
import functools
import math

import jax
import jax.numpy as jnp
import numpy as np
from jax import lax
from jax.experimental import pallas as pl
from jax.experimental.pallas import tpu as pltpu

F32 = jnp.float32
BF16 = jnp.bfloat16

D_MODEL = 2048
CHUNK = 64
N_MEM = 256
EPS = 1e-6
HA, DA, DVA = 4, 256, 256
ROPE_BASE = 10000.0
HB, DB, DVB = 4, 128, 256
HC, DKC, DVC = 4, 256, 512
GATE_RANK = 16
GATE_TAU = 16.0
HX, DX = 4, 512
D_FF = 5632
N_EXPERTS = 8
TOP_K = 2
NEG_BIG = -1e30

VMEM_LIMIT_BYTES = 56 * 1024 * 1024
LANES = 128

ROW_TILE = 512
PROJ_ROW_TILE = 768
PROMPT_PROJ_ROW_TILE = 1024
MEM_ROW_TILE = 256
FF_TILE = 256
DENSE_FFN_ROW_TILE = 768
MOE_ROW_TILE = 1024
COMBINE_TILE = 256
DMA_ISSUE_UNROLL = 8
ATTN_TILE = 256
RET_CHUNK = 256
GLA_BLOCK = 256


def _params(*sem):
    return pltpu.CompilerParams(dimension_semantics=sem, vmem_limit_bytes=VMEM_LIMIT_BYTES)


def _resident_spec(block_shape, index_map):
    return pl.BlockSpec(block_shape, index_map, pipeline_mode=pl.Buffered(1))


def _rms_bf16(x, g):
    ms = jnp.mean(x * x, axis=-1, keepdims=True)
    return (x * lax.rsqrt(ms + EPS) * g).astype(BF16)


def _silu(x):
    return x / (1.0 + jnp.exp(-x))


def _dot(a, b):
    return jnp.dot(a, b, preferred_element_type=F32)


def _dot_nt(a, b):
    return lax.dot_general(a, b, (((1,), (1,)), ((), ())), preferred_element_type=F32)


def _dot_tn(a, b):
    return lax.dot_general(a, b, (((0,), (0,)), ((), ())), preferred_element_type=F32)


def _norm_mm_kernel(x_ref, g_ref, w_ref, o_ref, h_ref):
    @pl.when(pl.program_id(1) == 0)
    def _():
        h_ref[...] = _rms_bf16(x_ref[...], g_ref[...])

    o_ref[...] = _dot(h_ref[...], w_ref[...]).astype(o_ref.dtype)


def norm_matmul(x, g, w, *, col0, ncols, tn, out_dtype, tm=ROW_TILE):
    m, d = x.shape
    assert m % tm == 0 and ncols % tn == 0 and col0 % tn == 0
    cb0 = col0 // tn
    return pl.pallas_call(
        _norm_mm_kernel,
        out_shape=jax.ShapeDtypeStruct((m, ncols), out_dtype),
        grid=(m // tm, ncols // tn),
        in_specs=[
            pl.BlockSpec((tm, d), lambda i, j: (i, 0)),
            pl.BlockSpec((1, d), lambda i, j: (0, 0)),
            pl.BlockSpec((d, tn), lambda i, j: (0, cb0 + j)),
        ],
        out_specs=pl.BlockSpec((tm, tn), lambda i, j: (i, j)),
        scratch_shapes=[pltpu.VMEM((tm, d), BF16)],
        compiler_params=_params("parallel", "arbitrary"),
        name="norm_matmul",
    )(x, g, w)


def _norm_mm_groups_kernel(*refs, n_groups):
    x_ref, g_ref = refs[:2]
    w_refs = refs[2:2 + n_groups]
    o_refs = refs[2 + n_groups:]
    h = _rms_bf16(x_ref[...], g_ref[...])
    outs_per_group = len(o_refs) // n_groups
    for k, w_ref in enumerate(w_refs):
        y = _dot(h, w_ref[...])
        for o_ref in o_refs[k * outs_per_group:(k + 1) * outs_per_group]:
            o_ref[...] = y.astype(o_ref.dtype)


def norm_matmul_groups(x, g, w, *, row0, nrows, col0, group_cols, n_groups, out_dtypes, tm=ROW_TILE):
    d = x.shape[1]
    assert nrows % tm == 0 and row0 % tm == 0 and col0 % group_cols == 0
    rb0 = row0 // tm
    cb0 = col0 // group_cols
    in_specs = [pl.BlockSpec((tm, d), lambda i: (rb0 + i, 0)), pl.BlockSpec((1, d), lambda i: (0, 0))]
    for k in range(n_groups):
        in_specs.append(_resident_spec((d, group_cols), lambda i, k=k: (0, cb0 + k)))
    out_shape, out_specs = [], []
    for k in range(n_groups):
        for dt in out_dtypes:
            out_shape.append(jax.ShapeDtypeStruct((nrows, group_cols), dt))
            out_specs.append(pl.BlockSpec((tm, group_cols), lambda i: (i, 0)))
    return pl.pallas_call(
        functools.partial(_norm_mm_groups_kernel, n_groups=n_groups),
        out_shape=out_shape,
        grid=(nrows // tm,),
        in_specs=in_specs,
        out_specs=out_specs,
        compiler_params=_params("parallel"),
        name="norm_matmul_groups",
    )(x, g, *([w] * n_groups))


def _mm_res_kernel(*refs, n_lhs, n_head_blocks):
    head_refs = refs[:n_lhs]
    tail_refs = refs[n_lhs:2 * n_lhs]
    w_refs = refs[2 * n_lhs:3 * n_lhs]
    r_refs = refs[3 * n_lhs:-1]
    o_ref = refs[-1]

    def compute(a_refs, r_ref):
        acc = r_ref[...]
        for a_ref, w_ref in zip(a_refs, w_refs):
            acc = acc + _dot(a_ref[...], w_ref[...])
        o_ref[...] = acc

    is_head = pl.program_id(0) < n_head_blocks

    @pl.when(is_head)
    def _():
        compute(head_refs, r_refs[0])

    @pl.when(jnp.logical_not(is_head))
    def _():
        compute(tail_refs, r_refs[-1])


def matmul_residual(lhs_pairs, w, res, *, tm=ROW_TILE):
    res_parts = tuple(res) if isinstance(res, (tuple, list)) else (res,)
    m = sum(r.shape[0] for r in res_parts)
    n = res_parts[0].shape[1]
    n_lhs = len(lhs_pairs)
    nhb = lhs_pairs[0][0].shape[0] // tm
    in_specs = []
    for head, tail in lhs_pairs:
        assert head.shape[0] == nhb * tm and tail.shape[0] == tm and (nhb + 1) * tm == m
        in_specs.append(pl.BlockSpec((tm, head.shape[1]), lambda i: (jnp.minimum(i, nhb - 1), 0)))
    for head, tail in lhs_pairs:
        in_specs.append(pl.BlockSpec((tm, tail.shape[1]), lambda i: (0, 0)))
    row = 0
    for head, _ in lhs_pairs:
        kk = head.shape[1]
        assert row % kk == 0
        rb = row // kk
        in_specs.append(_resident_spec((kk, n), lambda i, rb=rb: (rb, 0)))
        row += kk
    if len(res_parts) == 1:
        in_specs.append(pl.BlockSpec((tm, n), lambda i: (i, 0)))
    else:
        assert res_parts[0].shape[0] == nhb * tm and res_parts[1].shape[0] == tm
        in_specs.append(pl.BlockSpec((tm, n), lambda i: (jnp.minimum(i, nhb - 1), 0)))
        in_specs.append(pl.BlockSpec((tm, n), lambda i: (0, 0)))
    heads = [p[0] for p in lhs_pairs]
    tails = [p[1] for p in lhs_pairs]
    return pl.pallas_call(
        functools.partial(_mm_res_kernel, n_lhs=n_lhs, n_head_blocks=nhb),
        out_shape=jax.ShapeDtypeStruct((m, n), F32),
        grid=(m // tm,),
        in_specs=in_specs,
        out_specs=pl.BlockSpec((tm, n), lambda i: (i, 0)),
        compiler_params=_params("parallel"),
        name="matmul_residual",
    )(*heads, *tails, *([w] * n_lhs), *res_parts)


def _ffn_kernel(be_ref, nv_ref, x_ref, g_ref, wg_ref, wu_ref, wd_ref, o_ref, h_ref, *, residual, packed):
    i = pl.program_id(0)
    f = pl.program_id(1)
    valid = i < nv_ref[0]

    @pl.when(f == 0)
    def _():
        if packed:
            h_ref[...] = _unpack_bf16_pairs(x_ref[...])
            o_ref[...] = jnp.zeros_like(o_ref)
        else:
            x = x_ref[...]
            h_ref[...] = _rms_bf16(x, g_ref[...])
            o_ref[...] = x if residual else jnp.zeros_like(x)

    @pl.when(valid)
    def _():
        h = h_ref[...]
        a = _dot(h, wg_ref[0].astype(BF16))
        u = _dot(h, wu_ref[0].astype(BF16))
        act = (_silu(a) * u).astype(BF16)
        o_ref[...] += _dot(act, wd_ref[0].astype(BF16))


def swiglu_blocks(x, g, w_gu, w_dn, block_expert, n_valid, *, residual, tm=ROW_TILE, tf=FF_TILE):
    m = x.shape[0]
    d = w_gu.shape[1]
    packed = x.dtype == jnp.uint32
    assert x.shape[1] == (d // 2 if packed else d) and not (packed and residual)
    ff = w_dn.shape[1]
    nf = ff // tf
    nb = m // tm

    def _x_map(i, f, be, nv):
        return (jnp.minimum(i, nv[0] - 1), 0)

    def _f_eff(i, f, nv):
        return jnp.where(i < nv[0], f, nf - 1)

    return pl.pallas_call(
        functools.partial(_ffn_kernel, residual=residual, packed=packed),
        out_shape=jax.ShapeDtypeStruct((m, d), F32),
        grid_spec=pltpu.PrefetchScalarGridSpec(
            num_scalar_prefetch=2,
            grid=(nb, nf),
            in_specs=[
                pl.BlockSpec((tm, x.shape[1]), _x_map),
                pl.BlockSpec((1, d), lambda i, f, be, nv: (0, 0)),
                pl.BlockSpec((1, d, tf), lambda i, f, be, nv: (be[i], 0, _f_eff(i, f, nv))),
                pl.BlockSpec((1, d, tf), lambda i, f, be, nv: (be[i], 0, nf + _f_eff(i, f, nv))),
                pl.BlockSpec((1, tf, d), lambda i, f, be, nv: (be[i], _f_eff(i, f, nv), 0)),
            ],
            out_specs=pl.BlockSpec((tm, d), lambda i, f, be, nv: (i, 0)),
            scratch_shapes=[pltpu.VMEM((tm, d), BF16)],
        ),
        compiler_params=_params("parallel", "arbitrary"),
        name="swiglu_blocks",
    )(block_expert, n_valid, x, g, w_gu, w_gu, w_dn)


def _router_kernel(x_ref, g_ref, w_ref, o_ref):
    h = _rms_bf16(x_ref[...], g_ref[...])
    logits = _dot(h, w_ref[...])
    lane = lax.broadcasted_iota(jnp.int32, logits.shape, 1).astype(F32)
    l1 = jnp.where(lane < N_EXPERTS, logits, NEG_BIG)
    m1 = jnp.max(l1, axis=-1, keepdims=True)
    i1 = jnp.min(jnp.where(l1 == m1, lane, float(LANES)), axis=-1, keepdims=True)
    l2 = jnp.where(lane == i1, NEG_BIG, l1)
    m2 = jnp.max(l2, axis=-1, keepdims=True)
    i2 = jnp.min(jnp.where(l2 == m2, lane, float(LANES)), axis=-1, keepdims=True)
    e = jnp.exp(m2 - m1)
    g1 = 1.0 / (1.0 + e)
    g2 = e / (1.0 + e)
    out = jnp.where(lane == 0.0, i1,
                    jnp.where(lane == 1.0, i2,
                              jnp.where(lane == 2.0, g1, jnp.where(lane == 3.0, g2, 0.0))))
    o_ref[...] = out


def router(x, g, w_pad, *, tm=ROW_TILE):
    m, d = x.shape
    return pl.pallas_call(
        _router_kernel,
        out_shape=jax.ShapeDtypeStruct((m, LANES), F32),
        grid=(m // tm,),
        in_specs=[
            pl.BlockSpec((tm, d), lambda i: (i, 0)),
            pl.BlockSpec((1, d), lambda i: (0, 0)),
            pl.BlockSpec((d, LANES), lambda i: (0, 0)),
        ],
        out_specs=pl.BlockSpec((tm, LANES), lambda i: (i, 0)),
        compiler_params=_params("parallel"),
        name="router",
    )(x, g, w_pad)


def _row_copy(src, dst, src_row, dst_row, sem):
    return pltpu.make_async_copy(src.at[pl.ds(src_row, 1)], dst.at[pl.ds(dst_row, 1)], sem)


def _pack_bf16_pairs(h):
    half = h.shape[-1] // 2
    lo = lax.bitcast_convert_type(h[:, :half].astype(F32), jnp.uint32) >> 16
    hi = lax.bitcast_convert_type(h[:, half:].astype(F32), jnp.uint32) & jnp.uint32(0xFFFF0000)
    return lo | hi


def _unpack_bf16_pairs(u):
    lo = lax.bitcast_convert_type(u << 16, F32)
    hi = lax.bitcast_convert_type(u & jnp.uint32(0xFFFF0000), F32)
    return jnp.concatenate([lo, hi], axis=-1).astype(BF16)


def _dispatch_kernel(d0_ref, d1_ref, plo_ref, phi_ref, x_ref, g_ref, xs_hbm, hbuf, zrow, sem, zsem, *, tm):
    i = pl.program_id(0)
    last = pl.num_programs(0) - 1
    slot = i % 2
    base = i * tm

    def zero_copy(r):
        return pltpu.make_async_copy(zrow.at[pl.ds(0, 1)], xs_hbm.at[pl.ds(r, 1)], zsem)

    def for_each_unrouted_row(fn):
        for e in range(N_EXPERTS):
            def body(r, c):
                fn(r)
                return c
            lax.fori_loop(plo_ref[e], phi_ref[e], body, 0)

    @pl.when(i == 0)
    def _():
        zrow[...] = jnp.zeros_like(zrow)
        for_each_unrouted_row(lambda r: zero_copy(r).start())

    hbuf[slot] = _pack_bf16_pairs(_rms_bf16(x_ref[...], g_ref[...]))

    def start(r, c):
        _row_copy(hbuf.at[slot], xs_hbm, r, d0_ref[base + r], sem.at[slot]).start()
        _row_copy(hbuf.at[slot], xs_hbm, r, d1_ref[base + r], sem.at[slot]).start()
        return c

    lax.fori_loop(0, tm, start, 0, unroll=DMA_ISSUE_UNROLL)

    def wait_slot(s):
        for _ in range(TOP_K):
            pltpu.make_async_copy(hbuf.at[s], xs_hbm.at[pl.ds(0, tm)], sem.at[s]).wait()

    @pl.when(i >= 1)
    def _():
        wait_slot(1 - slot)

    @pl.when(i == last)
    def _():
        wait_slot(slot)
        for_each_unrouted_row(lambda r: zero_copy(r).wait())


def moe_dispatch(x, g, d0, d1, pad_lo, pad_hi, n_rows, *, tm=ROW_TILE):
    m, d = x.shape
    return pl.pallas_call(
        functools.partial(_dispatch_kernel, tm=tm),
        out_shape=jax.ShapeDtypeStruct((n_rows, d // 2), jnp.uint32),
        grid_spec=pltpu.PrefetchScalarGridSpec(
            num_scalar_prefetch=4,
            grid=(m // tm,),
            in_specs=[
                pl.BlockSpec((tm, d), lambda i, *_: (i, 0)),
                pl.BlockSpec((1, d), lambda i, *_: (0, 0)),
            ],
            out_specs=pl.BlockSpec(memory_space=pl.ANY),
            scratch_shapes=[pltpu.VMEM((2, tm, d // 2), jnp.uint32), pltpu.VMEM((8, d // 2), jnp.uint32),
                            pltpu.SemaphoreType.DMA((2,)), pltpu.SemaphoreType.DMA],
        ),
        compiler_params=_params("arbitrary"),
        name="moe_dispatch",
    )(d0, d1, pad_lo, pad_hi, x, g)


def _combine_kernel(d0_ref, d1_ref, x_ref, r_ref, g_ref, y_hbm, o_ref, buf, sem, *, tm, row0):
    i = pl.program_id(0)
    slot = i % 2

    def issue(block, slot_):
        base = row0 + block * tm

        def start(r, c):
            _row_copy(y_hbm, buf.at[slot_], d0_ref[base + r], r, sem.at[slot_]).start()
            _row_copy(y_hbm, buf.at[slot_], d1_ref[base + r], tm + r, sem.at[slot_]).start()
            return c

        lax.fori_loop(0, tm, start, 0, unroll=DMA_ISSUE_UNROLL)

    @pl.when(i == 0)
    def _():
        issue(0, 0)

    @pl.when(i + 1 < pl.num_programs(0))
    def _():
        issue(i + 1, 1 - slot)

    pltpu.make_async_copy(y_hbm.at[pl.ds(0, 2 * tm)], buf.at[slot], sem.at[slot]).wait()
    rt = r_ref[...]
    g0 = rt[:, 2:3]
    g1 = rt[:, 3:4]
    x = x_ref[...] + (buf[slot, 0:tm, :] * g0 + buf[slot, tm:2 * tm, :] * g1)
    ms = jnp.mean(x * x, axis=-1, keepdims=True)
    o_ref[...] = x * lax.rsqrt(ms + EPS) * g_ref[...]


def moe_combine_norm(x, route, g, y_rows, d0, d1, *, row0, nrows, tm=COMBINE_TILE):
    d = x.shape[1]
    rb0 = row0 // tm
    return pl.pallas_call(
        functools.partial(_combine_kernel, tm=tm, row0=row0),
        out_shape=jax.ShapeDtypeStruct((nrows, d), F32),
        grid_spec=pltpu.PrefetchScalarGridSpec(
            num_scalar_prefetch=2,
            grid=(nrows // tm,),
            in_specs=[
                pl.BlockSpec((tm, d), lambda i, a, b: (rb0 + i, 0)),
                pl.BlockSpec((tm, LANES), lambda i, a, b: (rb0 + i, 0)),
                pl.BlockSpec((1, d), lambda i, a, b: (0, 0)),
                pl.BlockSpec(memory_space=pl.ANY),
            ],
            out_specs=pl.BlockSpec((tm, d), lambda i, a, b: (i, 0)),
            scratch_shapes=[pltpu.VMEM((2, 2 * tm, d), F32), pltpu.SemaphoreType.DMA((2,))],
        ),
        compiler_params=_params("arbitrary"),
        name="moe_combine",
    )(d0, d1, x, route, g, y_rows)


def _gate_kernel(x_ref, g_ref, wlr_ref, wa2_ref, b_ref, o_ref):
    h = _rms_bf16(x_ref[...], g_ref[...])
    a_lr = _dot(h, wlr_ref[...])
    z = _dot(a_lr.astype(BF16), wa2_ref[...]) + b_ref[...]
    o_ref[...] = (jnp.minimum(z, 0.0) - jnp.log(1.0 + jnp.exp(-jnp.abs(z)))) * (1.0 / GATE_TAU)


def gla_gate(x, g, w_lr_pad, w_a2_pad, b_a, *, tm=ROW_TILE):
    m, d = x.shape
    n = w_a2_pad.shape[1]
    return pl.pallas_call(
        _gate_kernel,
        out_shape=jax.ShapeDtypeStruct((m, n), F32),
        grid=(m // tm,),
        in_specs=[
            pl.BlockSpec((tm, d), lambda i: (i, 0)),
            pl.BlockSpec((1, d), lambda i: (0, 0)),
            pl.BlockSpec((d, LANES), lambda i: (0, 0)),
            pl.BlockSpec((LANES, n), lambda i: (0, 0)),
            pl.BlockSpec((1, n), lambda i: (0, 0)),
        ],
        out_specs=pl.BlockSpec((tm, n), lambda i: (i, 0)),
        compiler_params=_params("parallel"),
        name="gla_gate",
    )(x, g, w_lr_pad, w_a2_pad, b_a)


def _rotary(x, cos, sin):
    half = x.shape[-1] // 2
    x1, x2 = x[:, :half], x[:, half:]
    return jnp.concatenate([x1 * cos - x2 * sin, x1 * sin + x2 * cos], axis=-1)


def _retention_kernel(lg_ref, *refs, L, has_state):
    if has_state:
        q_ref, k_ref, v_ref, g_ref, cos_ref, sin_ref, s0_ref, o_ref, sout_ref, s_ref = refs
    else:
        q_ref, k_ref, v_ref, g_ref, cos_ref, sin_ref, o_ref, sout_ref, s_ref = refs
    c = pl.program_id(1)

    @pl.when(c == 0)
    def _():
        if has_state:
            s_ref[...] = s0_ref[0]
        else:
            s_ref[...] = jnp.zeros_like(s_ref)

    cos = cos_ref[...]
    sin = sin_ref[...]
    n_col = lax.broadcasted_iota(jnp.int32, (L, 1), 0).astype(F32)
    n_row = lax.broadcasted_iota(jnp.int32, (1, L), 1).astype(F32)
    diff = n_col - n_row
    for h in range(HA):
        lg = lg_ref[h]
        qs = slice(h * DA, (h + 1) * DA)
        vs = slice(h * DVA, (h + 1) * DVA)
        qr = _rotary(q_ref[:, qs], cos, sin)
        kr = _rotary(k_ref[:, qs], cos, sin) * (DA ** -0.5)
        vb = v_ref[:, vs].astype(BF16)
        decay = jnp.where(diff >= 0.0, jnp.exp(jnp.maximum(diff, 0.0) * lg), 0.0)
        qb = qr.astype(BF16)
        scores = _dot_nt(qb, kr.astype(BF16)) * decay
        inner = _dot(scores.astype(BF16), vb)
        state = s_ref[h]
        cross = _dot(qb, state.astype(BF16)) * jnp.exp((n_col + 1.0) * lg)
        k_dec = (kr * jnp.exp((L - 1.0 - n_col) * lg)).astype(BF16)
        s_ref[h] = state * jnp.exp(jnp.zeros((1, 1), F32) + L * lg) + _dot_tn(k_dec, vb)
        ret = inner + cross
        ret = ret - jnp.mean(ret, axis=-1, keepdims=True)
        ret = ret * lax.rsqrt(jnp.mean(ret * ret, axis=-1, keepdims=True) + EPS)
        o_ref[:, vs] = (_silu(g_ref[:, vs]) * ret).astype(o_ref.dtype)

    @pl.when(c == pl.num_programs(1) - 1)
    def _():
        sout_ref[0] = s_ref[...]


def retention(proj, cos, sin, log_gamma, *, nbatch, seq, row0, pos0, L, state0=None):
    nc = seq // L
    rb0 = row0 // L
    pb0 = pos0 // L
    has_state = state0 is not None
    wq = HA * DA
    assert HA * DVA == wq

    def col(k):
        return lambda b, c, lg: (rb0 + b * nc + c, k)

    in_specs = [
        pl.BlockSpec((L, wq), col(0)),
        pl.BlockSpec((L, wq), col(1)),
        pl.BlockSpec((L, wq), col(2)),
        pl.BlockSpec((L, wq), col(3)),
        pl.BlockSpec((L, DA // 2), lambda b, c, lg: (pb0 + c, 0)),
        pl.BlockSpec((L, DA // 2), lambda b, c, lg: (pb0 + c, 0)),
    ]
    args = [proj, proj, proj, proj, cos, sin]
    if has_state:
        in_specs.append(pl.BlockSpec((1, HA, DA, DVA), lambda b, c, lg: (b, 0, 0, 0)))
        args.append(state0)
    return pl.pallas_call(
        functools.partial(_retention_kernel, L=L, has_state=has_state),
        out_shape=(jax.ShapeDtypeStruct((nbatch * seq, HA * DVA), BF16),
                   jax.ShapeDtypeStruct((nbatch, HA, DA, DVA), F32)),
        grid_spec=pltpu.PrefetchScalarGridSpec(
            num_scalar_prefetch=1,
            grid=(nbatch, nc),
            in_specs=in_specs,
            out_specs=(pl.BlockSpec((L, wq), lambda b, c, lg: (b * nc + c, 0)),
                       pl.BlockSpec((1, HA, DA, DVA), lambda b, c, lg: (b, 0, 0, 0))),
            scratch_shapes=[pltpu.VMEM((HA, DA, DVA), F32)],
        ),
        compiler_params=_params("parallel", "arbitrary"),
        name="retention",
    )(log_gamma, *args)


def _lambda_value(l_ref, lam_init):
    lv = l_ref[...]
    a = jnp.sum(lv[0:1] * lv[1:2], axis=-1, keepdims=True)
    b = jnp.sum(lv[2:3] * lv[3:4], axis=-1, keepdims=True)
    return jnp.exp(a) - jnp.exp(b) + lam_init


def _head_norm_scale(o, scale):
    return o * lax.rsqrt(jnp.mean(o * o, axis=-1, keepdims=True) + EPS) * scale


def _diff_softmax_pv(q, key_parts, val_parts, masks, lam):
    w = None
    for c in range(2):
        qc = q[:, c * DB:(c + 1) * DB].astype(BF16)
        s = [_dot_nt(qc, k[:, c * DB:(c + 1) * DB]) for k in key_parts]
        s = [x if m is None else jnp.where(m, x, NEG_BIG) for x, m in zip(s, masks)]
        mx = functools.reduce(jnp.maximum, [jnp.max(x, axis=-1, keepdims=True) for x in s])
        p = [jnp.exp(x - mx) for x in s]
        inv = 1.0 / functools.reduce(lambda a, b: a + b, [jnp.sum(x, axis=-1, keepdims=True) for x in p])
        if c == 0:
            w = [x * inv for x in p]
        else:
            w = [a - lam * (x * inv) for a, x in zip(w, p)]
    outs = [_dot(a.astype(BF16), v) for a, v in zip(w, val_parts)]
    return functools.reduce(lambda a, b: a + b, outs)


def _dattn_prompt_kernel(q_ref, k_ref, v_ref, l_ref, o_ref, *, tq, nq, lam_init):
    i = pl.program_id(2)
    lam = _lambda_value(l_ref, lam_init)
    r_chunk = lax.broadcasted_iota(jnp.int32, (tq, tq), 0) // CHUNK
    c_chunk = lax.broadcasted_iota(jnp.int32, (tq, tq), 1) // CHUNK
    diag_mask = c_chunk <= r_chunk

    for n in range(nq):
        @pl.when(i == n)
        def _(n=n):
            q = q_ref[...] * (DB ** -0.5)
            lo = n * tq
            keys = [k_ref[lo:lo + tq, :]]
            vals = [v_ref[lo:lo + tq, :]]
            masks = [diag_mask]
            if n > 0:
                keys.insert(0, k_ref[0:lo, :])
                vals.insert(0, v_ref[0:lo, :])
                masks.insert(0, None)
            o = _diff_softmax_pv(q, keys, vals, masks, lam)
            o_ref[...] = _head_norm_scale(o, 1.0 - lam_init).astype(o_ref.dtype)


def diff_attention_prompt(proj, k_bf, v_bf, lam_params, *, nbatch, seq, lam_init, tq=ATTN_TILE):
    nq = seq // tq
    assert tq % CHUNK == 0
    return pl.pallas_call(
        functools.partial(_dattn_prompt_kernel, tq=tq, nq=nq, lam_init=lam_init),
        out_shape=jax.ShapeDtypeStruct((nbatch * seq, HB * DVB), BF16),
        grid=(nbatch, HB, nq),
        in_specs=[
            pl.BlockSpec((tq, 2 * DB), lambda b, h, i: (b * nq + i, 4 * HA + h)),
            pl.BlockSpec((seq, 2 * DB), lambda b, h, i: (b, h)),
            pl.BlockSpec((seq, DVB), lambda b, h, i: (b, h)),
            pl.BlockSpec((4, DB), lambda b, h, i: (0, 0)),
        ],
        out_specs=pl.BlockSpec((tq, DVB), lambda b, h, i: (b * nq + i, h)),
        compiler_params=_params("parallel", "parallel", "arbitrary"),
        name="diff_attention_prompt",
    )(proj, k_bf, v_bf, lam_params)


def _dattn_sample_kernel(q_ref, kn_ref, vn_ref, kc_ref, vc_ref, l_ref, o_ref, *, lam_init):
    q = q_ref[...] * (DB ** -0.5)
    lam = _lambda_value(l_ref, lam_init)
    keys = [kc_ref[0].astype(BF16), kn_ref[...]]
    vals = [vc_ref[0].astype(BF16), vn_ref[...]]
    o = _diff_softmax_pv(q, keys, vals, [None, None], lam)
    o_ref[...] = _head_norm_scale(o, 1.0 - lam_init).astype(o_ref.dtype)


def diff_attention_sample(proj, k_new, v_new, cache_k, cache_v, lam_params, *, nbatch, seq, row0, lam_init):
    past = cache_k.shape[1]
    assert seq == CHUNK and past % CHUNK == 0 and row0 % seq == 0
    rb0 = row0 // seq
    return pl.pallas_call(
        functools.partial(_dattn_sample_kernel, lam_init=lam_init),
        out_shape=jax.ShapeDtypeStruct((nbatch * seq, HB * DVB), BF16),
        grid=(nbatch, HB),
        in_specs=[
            pl.BlockSpec((seq, 2 * DB), lambda b, h: (rb0 + b, 4 * HA + h)),
            pl.BlockSpec((seq, 2 * DB), lambda b, h: (b, h)),
            pl.BlockSpec((seq, DVB), lambda b, h: (b, h)),
            pl.BlockSpec((1, past, 2 * DB), lambda b, h: (b, 0, h)),
            pl.BlockSpec((1, past, DVB), lambda b, h: (b, 0, h)),
            pl.BlockSpec((4, DB), lambda b, h: (0, 0)),
        ],
        out_specs=pl.BlockSpec((seq, DVB), lambda b, h: (b, h)),
        compiler_params=_params("parallel", "parallel"),
        name="diff_attention_sample",
    )(proj, k_new, v_new, cache_k, cache_v, lam_params)


def _split3_bf16(x):
    hi = x.astype(BF16)
    r1 = x - hi.astype(F32)
    mid = r1.astype(BF16)
    lo = (r1 - mid.astype(F32)).astype(BF16)
    return hi, mid, lo


def _gla_kernel(*refs, nsub, has_state):
    if has_state:
        q_ref, k_ref, v_ref, g_ref, a_ref, s0_ref, o_ref, sout_ref, s_ref = refs
    else:
        q_ref, k_ref, v_ref, g_ref, a_ref, o_ref, sout_ref, s_ref = refs
    c = pl.program_id(1)
    L = CHUNK

    @pl.when(c == 0)
    def _():
        for h in range(HC):
            if has_state:
                s_ref[h] = s0_ref[0, h].T
            else:
                s_ref[h] = jnp.zeros((DVC, DKC), F32)

    row = lax.broadcasted_iota(jnp.int32, (L, L), 0)
    colm = lax.broadcasted_iota(jnp.int32, (L, L), 1)
    causal = colm <= row
    tril = jnp.where(causal, 1.0, 0.0).astype(BF16)

    def chunk(j):
        rows = pl.ds(j * L, L)
        for h in range(HC):
            ks = slice(h * DKC, (h + 1) * DKC)
            vs = slice(h * DVC, (h + 1) * DVC)
            q = q_ref[rows, ks] * (DKC ** -0.5)
            k = k_ref[rows, ks]
            vb = v_ref[rows, vs].astype(BF16)
            hi, mid, lo = _split3_bf16(a_ref[rows, ks])
            b = _dot(tril, hi) + _dot(tril, mid) + _dot(tril, lo)
            ref = b[L // 2:L // 2 + 1, :]
            b_last = b[L - 1:L, :]
            q_in = (q * jnp.exp(b - ref)).astype(BF16)
            k_in = (k * jnp.exp(ref - b)).astype(BF16)
            att = jnp.where(causal, _dot_nt(q_in, k_in), 0.0)
            inner = _dot(att.astype(BF16), vb)
            state_t = s_ref[h]
            cross = _dot_nt((q * jnp.exp(b)).astype(BF16), state_t.astype(BF16))
            k_out = (k * jnp.exp(b_last - b)).astype(BF16)
            s_ref[h] = jnp.exp(b_last) * state_t + _dot_tn(vb, k_out)
            o = inner + cross
            o = o * lax.rsqrt(jnp.mean(o * o, axis=-1, keepdims=True) + EPS)
            o_ref[rows, vs] = (_silu(g_ref[rows, vs]) * o).astype(o_ref.dtype)

    for j in range(nsub):
        chunk(j)

    @pl.when(c == pl.num_programs(1) - 1)
    def _():
        for h in range(HC):
            sout_ref[0, h] = s_ref[h].T


def gla(proj, log_a, *, nbatch, seq, row0, rows_per_step, state0=None):
    lb = rows_per_step
    nc = seq // lb
    rb0 = row0 // lb
    has_state = state0 is not None
    wk = HC * DKC
    wv = HC * DVC
    assert wv == 2 * wk

    def rows(b, c):
        return rb0 + b * nc + c

    in_specs = [
        pl.BlockSpec((lb, wk), lambda b, c: (rows(b, c), 0)),
        pl.BlockSpec((lb, wk), lambda b, c: (rows(b, c), 1)),
        pl.BlockSpec((lb, wv), lambda b, c: (rows(b, c), 1)),
        pl.BlockSpec((lb, wv), lambda b, c: (rows(b, c), 2)),
        pl.BlockSpec((lb, wk), lambda b, c: (rows(b, c), 0)),
    ]
    args = [proj, proj, proj, proj, log_a]
    if has_state:
        in_specs.append(pl.BlockSpec((1, HC, DKC, DVC), lambda b, c: (b, 0, 0, 0)))
        args.append(state0)
    return pl.pallas_call(
        functools.partial(_gla_kernel, nsub=lb // CHUNK, has_state=has_state),
        out_shape=(jax.ShapeDtypeStruct((nbatch * seq, wv), BF16),
                   jax.ShapeDtypeStruct((nbatch, HC, DKC, DVC), F32)),
        grid=(nbatch, nc),
        in_specs=in_specs,
        out_specs=(pl.BlockSpec((lb, wv), lambda b, c: (b * nc + c, 0)),
                   pl.BlockSpec((1, HC, DKC, DVC), lambda b, c: (b, 0, 0, 0))),
        scratch_shapes=[pltpu.VMEM((HC, DVC, DKC), F32)],
        compiler_params=_params("parallel", "arbitrary"),
        name="gla",
    )(*args)


def _xattn_kernel(q_ref, k_ref, v_ref, o_ref):
    for h in range(HX):
        sl = slice(h * DX, (h + 1) * DX)
        q = q_ref[:, sl]
        k = k_ref[0, :, sl].astype(BF16)
        v = v_ref[0, :, sl].astype(BF16)
        s = _dot_nt(q, k) * (DX ** -0.5)
        m = jnp.max(s, axis=-1, keepdims=True)
        p = jnp.exp(s - m)
        p = p / jnp.sum(p, axis=-1, keepdims=True)
        o_ref[:, sl] = _dot(p.astype(BF16), v).astype(o_ref.dtype)


def cross_attention(q, mem_k, mem_v, *, nbatch, seq, row0, tq):
    d = q.shape[1]
    nq = seq // tq
    rb0 = row0 // tq
    return pl.pallas_call(
        _xattn_kernel,
        out_shape=jax.ShapeDtypeStruct((nbatch * seq, d), BF16),
        grid=(nbatch, nq),
        in_specs=[
            pl.BlockSpec((tq, d), lambda b, i: (rb0 + b * nq + i, 0)),
            pl.BlockSpec((1, N_MEM, d), lambda b, i: (b, 0, 0)),
            pl.BlockSpec((1, N_MEM, d), lambda b, i: (b, 0, 0)),
        ],
        out_specs=pl.BlockSpec((tq, d), lambda b, i: (b * nq + i, 0)),
        compiler_params=_params("parallel", "arbitrary"),
        name="cross_attention",
    )(q, mem_k, mem_v)


def _routing_tables(route, tm):
    n_tok = route.shape[0]
    n_rows = n_tok * TOP_K
    nb = (n_rows + N_EXPERTS * (tm - 1) + tm - 1) // tm
    flat_e = route[:, :TOP_K].astype(jnp.int32).reshape(-1)
    onehot = (flat_e[:, None] == jnp.arange(N_EXPERTS, dtype=jnp.int32)[None, :]).astype(jnp.int32)
    csum = jnp.cumsum(onehot, axis=0)
    rank = jnp.sum(csum * onehot, axis=1) - 1
    counts = csum[-1]
    padded = (counts + tm - 1) // tm * tm
    pad_end = jnp.cumsum(padded)
    pad_start = pad_end - padded
    dest = jnp.sum(onehot * pad_start[None, :], axis=1) + rank
    n_valid = (pad_end[-1] // tm).astype(jnp.int32)
    blk = jnp.minimum(jnp.arange(nb, dtype=jnp.int32), n_valid - 1) * tm
    block_expert = jnp.minimum(jnp.searchsorted(pad_end, blk, side='right'), N_EXPERTS - 1).astype(jnp.int32)
    dest2 = dest.reshape(n_tok, TOP_K).astype(jnp.int32)
    pad_lo = (pad_start + counts).astype(jnp.int32)
    pad_hi = pad_end.astype(jnp.int32).at[N_EXPERTS - 1].set(nb * tm)
    return block_expert, n_valid.reshape(1), dest2[:, 0], dest2[:, 1], pad_lo, pad_hi, nb * tm


def kernel(x_prompt, x_sample, mem_prompt, cache_diff_k, cache_diff_v, state_ret, state_gla, cache_mem_k, cache_mem_v, g_mix, g_xattn, g_mem, g_ffn, g_final, w_in_even, w_out_even, lambda_q1, lambda_k1, lambda_q2, lambda_k2, w_in_odd, w_gate_lr, b_gate_lr, w_out_odd, w_xq, w_xkv, w_xo, w_ffn_gu, w_ffn_dn, w_router, w_moe_gu, w_moe_dn):
    d = D_MODEL
    bp, tp, _ = x_prompt.shape
    bs, ts, _ = x_sample.shape
    past = cache_diff_k.shape[2]
    np_tok = bp * tp
    ns_tok = bs * ts
    n_tok = np_tok + ns_tok
    depth = g_mix.shape[0]

    x = (x_prompt.reshape(np_tok, d), x_sample.reshape(ns_tok, d))

    half = DA // 2
    inv = 1.0 / (ROPE_BASE ** jnp.linspace(0.0, 1.0, half, dtype=F32))
    pos = jnp.arange(max(tp, past + ts), dtype=jnp.int32).astype(F32)
    ang = pos[:, None] * inv[None, :]
    cos_t, sin_t = jnp.cos(ang), jnp.sin(ang)
    log_gamma = jnp.log1p(-jnp.power(2.0, -5.0 - jnp.arange(HA, dtype=F32)))


    ret_p, ret_s, gla_p, gla_s = [], [], [], []
    dk_p, dv_p, dk_s, dv_s = [], [], [], []
    mk_p, mv_p = [], []
    y_rows = route = d0 = d1 = None

    for i in range(depth):
        j = i // 2
        g_i = g_mix[i].reshape(1, d)
        if i % 2 == 0:
            w_in = w_in_even[j].astype(BF16)
            c_dk = 4 * HA * DA + HB * 2 * DB
            w_kv = HB * 2 * DB
            assert HB * DVB == w_kv
            if isinstance(x, tuple):
                (x_p, x_s), off_s = x, 0
                proj_p = norm_matmul(x_p, g_i, w_in, col0=0, ncols=c_dk, tn=1024, out_dtype=F32,
                                     tm=PROMPT_PROJ_ROW_TILE)
                proj_s = norm_matmul(x_s, g_i, w_in, col0=0, ncols=c_dk, tn=1024, out_dtype=F32)
            else:
                x_p, x_s, off_s = x, x, np_tok
                proj_p = proj_s = norm_matmul(x, g_i, w_in, col0=0, ncols=c_dk, tn=1024, out_dtype=F32,
                                              tm=PROJ_ROW_TILE)
            dk_f_p, dk_b_p, dv_f_p, dv_b_p = norm_matmul_groups(
                x_p, g_i, w_in, row0=0, nrows=np_tok, col0=c_dk, group_cols=w_kv, n_groups=2,
                out_dtypes=(F32, BF16))
            dk_f_s, dk_b_s, dv_f_s, dv_b_s = norm_matmul_groups(
                x_s, g_i, w_in, row0=off_s, nrows=ns_tok, col0=c_dk, group_cols=w_kv, n_groups=2,
                out_dtypes=(F32, BF16))
            lam_init = 0.8 - 0.6 * math.exp(-0.3 * i)
            lam_params = jnp.stack([lambda_q1[j], lambda_k1[j], lambda_q2[j], lambda_k2[j]]).astype(F32)
            a_p, s_p = retention(proj_p, cos_t, sin_t, log_gamma, nbatch=bp, seq=tp, row0=0, pos0=0, L=RET_CHUNK)
            a_s, s_s = retention(proj_s, cos_t, sin_t, log_gamma, nbatch=bs, seq=ts, row0=off_s, pos0=past,
                                 L=CHUNK, state0=state_ret[j])
            ret_p.append(s_p)
            ret_s.append(s_s)
            b_p = diff_attention_prompt(proj_p, dk_b_p, dv_b_p, lam_params, nbatch=bp, seq=tp, lam_init=lam_init)
            b_s = diff_attention_sample(
                proj_s, dk_b_s, dv_b_s, cache_diff_k[j].reshape(bs, past, HB * 2 * DB),
                cache_diff_v[j].reshape(bs, past, HB * DVB), lam_params, nbatch=bs, seq=ts, row0=off_s,
                lam_init=lam_init)
            dk_p.append(dk_f_p.reshape(bp, tp, HB, 2, DB))
            dk_s.append(dk_f_s.reshape(bs, ts, HB, 2, DB))
            dv_p.append(dv_f_p.reshape(bp, tp, HB, DVB))
            dv_s.append(dv_f_s.reshape(bs, ts, HB, DVB))
            x = matmul_residual([(a_p, a_s), (b_p, b_s)], w_out_even[j].astype(BF16), x)
        else:
            n_main = 2 * HC * DKC + 2 * HC * DVC
            w_in = w_in_odd[j]
            proj = norm_matmul(x, g_i, w_in[:, :n_main].astype(BF16), col0=0, ncols=n_main, tn=1024, out_dtype=F32,
                               tm=PROJ_ROW_TILE)
            w_lr = jnp.zeros((d, LANES), BF16).at[:, :GATE_RANK].set(w_in[:, n_main:].astype(BF16))
            w_a2 = jnp.zeros((LANES, HC * DKC), BF16).at[:GATE_RANK].set(w_gate_lr[j].astype(BF16))
            log_a = gla_gate(x, g_i, w_lr, w_a2, b_gate_lr[j].reshape(1, -1).astype(F32))
            o_p, s_p = gla(proj, log_a, nbatch=bp, seq=tp, row0=0, rows_per_step=GLA_BLOCK)
            o_s, s_s = gla(proj, log_a, nbatch=bs, seq=ts, row0=np_tok, rows_per_step=ts, state0=state_gla[j])
            gla_p.append(s_p)
            gla_s.append(s_s)
            x = matmul_residual([(o_p, o_s)], w_out_odd[j].astype(BF16), x)

        (q,) = norm_matmul_groups(x, g_xattn[i].reshape(1, d), w_xq[i].astype(BF16), row0=0, nrows=n_tok, col0=0,
                                  group_cols=HX * DX, n_groups=1, out_dtypes=(BF16,))
        mem2d = mem_prompt.reshape(bp * N_MEM, d)
        mk, mv = norm_matmul_groups(mem2d, g_mem[i].reshape(1, d), w_xkv[i].astype(BF16), row0=0,
                                    nrows=bp * N_MEM, col0=0, group_cols=HX * DX, n_groups=2, out_dtypes=(F32,),
                                    tm=MEM_ROW_TILE)
        mk_p.append(mk.reshape(bp, N_MEM, HX, DX))
        mv_p.append(mv.reshape(bp, N_MEM, HX, DX))
        o_p = cross_attention(q, mk.reshape(bp, N_MEM, HX * DX), mv.reshape(bp, N_MEM, HX * DX),
                              nbatch=bp, seq=tp, row0=0, tq=512)
        o_s = cross_attention(q, cache_mem_k[i].reshape(bs, N_MEM, HX * DX),
                              cache_mem_v[i].reshape(bs, N_MEM, HX * DX), nbatch=bs, seq=ts, row0=np_tok, tq=ts)
        x = matmul_residual([(o_p, o_s)], w_xo[i].astype(BF16), x)

        g_f = g_ffn[i].reshape(1, d)
        if i % 2 == 0:
            dense_blocks = n_tok // DENSE_FFN_ROW_TILE
            x = swiglu_blocks(x, g_f, w_ffn_gu[j][None], w_ffn_dn[j][None],
                              jnp.zeros((dense_blocks,), jnp.int32), jnp.full((1,), dense_blocks, jnp.int32),
                              residual=True, tm=DENSE_FFN_ROW_TILE)
        else:
            w_r = jnp.zeros((d, LANES), BF16).at[:, :N_EXPERTS].set(w_router[j].astype(BF16))
            route = router(x, g_f, w_r)
            block_expert, n_valid, d0, d1, pad_lo, pad_hi, n_rows = _routing_tables(route, MOE_ROW_TILE)
            xs = moe_dispatch(x, g_f, d0, d1, pad_lo, pad_hi, n_rows)
            y_rows = swiglu_blocks(xs, g_f, w_moe_gu[j], w_moe_dn[j], block_expert, n_valid, residual=False,
                                   tm=MOE_ROW_TILE)
            if i != depth - 1:
                raise NotImplementedError("MoE layer must be the last layer")

    g_fin = g_final.reshape(1, d)
    y_p = moe_combine_norm(x, route, g_fin, y_rows, d0, d1, row0=0, nrows=np_tok)
    y_s = moe_combine_norm(x, route, g_fin, y_rows, d0, d1, row0=np_tok, nrows=ns_tok)

    return (y_p.reshape(bp, tp, d), y_s.reshape(bs, ts, d),
            jnp.stack(dk_p), jnp.stack(dv_p), jnp.stack(ret_p), jnp.stack(gla_p),
            jnp.stack(mk_p), jnp.stack(mv_p),
            jnp.stack(dk_s), jnp.stack(dv_s), jnp.stack(ret_s), jnp.stack(gla_s))
```

```python
import functools
import math

import jax
import jax.numpy as jnp
import numpy as np
from jax import lax
from jax.experimental import pallas as pl
from jax.experimental.pallas import tpu as pltpu

F32 = jnp.float32
BF16 = jnp.bfloat16

D_MODEL = 2048
CHUNK = 64
N_MEM = 256
EPS = 1e-6
HA, DA, DVA = 4, 256, 256
ROPE_BASE = 10000.0
HB, DB, DVB = 4, 128, 256
HC, DKC, DVC = 4, 256, 512
GATE_RANK = 16
GATE_TAU = 16.0
HX, DX = 4, 512
D_FF = 5632
N_EXPERTS = 8
TOP_K = 2
NEG_BIG = -1e30

VMEM_LIMIT_BYTES = 56 * 1024 * 1024
LANES = 128
SUBLANES = 8

ROW_TILE = 512
PROJ_ROW_TILE = 768
PROMPT_PROJ_ROW_TILE = 1024
MEM_ROW_TILE = 256
FF_TILE = 256
DENSE_FFN_ROW_TILE = 768
MOE_ROW_TILE = 1024
COMBINE_TILE = 256
DMA_ISSUE_UNROLL = 8
ZERO_FILL_ROWS = 64
ATTN_TILE = 256
RET_CHUNK = 256
GLA_BLOCK = 256


def _params(*sem):
    return pltpu.CompilerParams(dimension_semantics=sem, vmem_limit_bytes=VMEM_LIMIT_BYTES)


def _resident_spec(block_shape, index_map):
    return pl.BlockSpec(block_shape, index_map, pipeline_mode=pl.Buffered(1))


def _rms_bf16(x, g):
    ms = jnp.mean(x * x, axis=-1, keepdims=True)
    return (x * lax.rsqrt(ms + EPS) * g).astype(BF16)


def _silu(x):
    return x / (1.0 + jnp.exp(-x))


def _dot(a, b):
    return jnp.dot(a, b, preferred_element_type=F32)


def _dot_nt(a, b):
    return lax.dot_general(a, b, (((1,), (1,)), ((), ())), preferred_element_type=F32)


def _dot_tn(a, b):
    return lax.dot_general(a, b, (((0,), (0,)), ((), ())), preferred_element_type=F32)


def _norm_mm_kernel(x_ref, g_ref, w_ref, o_ref, h_ref):
    @pl.when(pl.program_id(1) == 0)
    def _():
        h_ref[...] = _rms_bf16(x_ref[...], g_ref[...])

    o_ref[...] = _dot(h_ref[...], w_ref[...]).astype(o_ref.dtype)


def norm_matmul(x, g, w, *, col0, ncols, tn, out_dtype, tm=ROW_TILE):
    m, d = x.shape
    assert m % tm == 0 and ncols % tn == 0 and col0 % tn == 0
    cb0 = col0 // tn
    return pl.pallas_call(
        _norm_mm_kernel,
        out_shape=jax.ShapeDtypeStruct((m, ncols), out_dtype),
        grid=(m // tm, ncols // tn),
        in_specs=[
            pl.BlockSpec((tm, d), lambda i, j: (i, 0)),
            pl.BlockSpec((1, d), lambda i, j: (0, 0)),
            pl.BlockSpec((d, tn), lambda i, j: (0, cb0 + j)),
        ],
        out_specs=pl.BlockSpec((tm, tn), lambda i, j: (i, j)),
        scratch_shapes=[pltpu.VMEM((tm, d), BF16)],
        compiler_params=_params("parallel", "arbitrary"),
        name="norm_matmul",
    )(x, g, w)


def _norm_mm_groups_kernel(*refs, n_groups):
    x_ref, g_ref = refs[:2]
    w_refs = refs[2:2 + n_groups]
    o_refs = refs[2 + n_groups:]
    h = _rms_bf16(x_ref[...], g_ref[...])
    outs_per_group = len(o_refs) // n_groups
    for k, w_ref in enumerate(w_refs):
        y = _dot(h, w_ref[...])
        for o_ref in o_refs[k * outs_per_group:(k + 1) * outs_per_group]:
            o_ref[...] = y.astype(o_ref.dtype)


def norm_matmul_groups(x, g, w, *, row0, nrows, col0, group_cols, n_groups, out_dtypes, tm=ROW_TILE):
    d = x.shape[1]
    assert nrows % tm == 0 and row0 % tm == 0 and col0 % group_cols == 0
    rb0 = row0 // tm
    cb0 = col0 // group_cols
    in_specs = [pl.BlockSpec((tm, d), lambda i: (rb0 + i, 0)), pl.BlockSpec((1, d), lambda i: (0, 0))]
    for k in range(n_groups):
        in_specs.append(_resident_spec((d, group_cols), lambda i, k=k: (0, cb0 + k)))
    out_shape, out_specs = [], []
    for k in range(n_groups):
        for dt in out_dtypes:
            out_shape.append(jax.ShapeDtypeStruct((nrows, group_cols), dt))
            out_specs.append(pl.BlockSpec((tm, group_cols), lambda i: (i, 0)))
    return pl.pallas_call(
        functools.partial(_norm_mm_groups_kernel, n_groups=n_groups),
        out_shape=out_shape,
        grid=(nrows // tm,),
        in_specs=in_specs,
        out_specs=out_specs,
        compiler_params=_params("parallel"),
        name="norm_matmul_groups",
    )(x, g, *([w] * n_groups))


def _mm_res_kernel(*refs, n_lhs, n_head_blocks):
    head_refs = refs[:n_lhs]
    tail_refs = refs[n_lhs:2 * n_lhs]
    w_refs = refs[2 * n_lhs:3 * n_lhs]
    r_refs = refs[3 * n_lhs:-1]
    o_ref = refs[-1]

    def compute(a_refs, r_ref):
        acc = r_ref[...]
        for a_ref, w_ref in zip(a_refs, w_refs):
            acc = acc + _dot(a_ref[...], w_ref[...])
        o_ref[...] = acc

    is_head = pl.program_id(0) < n_head_blocks

    @pl.when(is_head)
    def _():
        compute(head_refs, r_refs[0])

    @pl.when(jnp.logical_not(is_head))
    def _():
        compute(tail_refs, r_refs[-1])


def matmul_residual(lhs_pairs, w, res, *, tm=ROW_TILE):
    res_parts = tuple(res) if isinstance(res, (tuple, list)) else (res,)
    m = sum(r.shape[0] for r in res_parts)
    n = res_parts[0].shape[1]
    n_lhs = len(lhs_pairs)
    nhb = lhs_pairs[0][0].shape[0] // tm
    in_specs = []
    for head, tail in lhs_pairs:
        assert head.shape[0] == nhb * tm and tail.shape[0] == tm and (nhb + 1) * tm == m
        in_specs.append(pl.BlockSpec((tm, head.shape[1]), lambda i: (jnp.minimum(i, nhb - 1), 0)))
    for head, tail in lhs_pairs:
        in_specs.append(pl.BlockSpec((tm, tail.shape[1]), lambda i: (0, 0)))
    row = 0
    for head, _ in lhs_pairs:
        kk = head.shape[1]
        assert row % kk == 0
        rb = row // kk
        in_specs.append(_resident_spec((kk, n), lambda i, rb=rb: (rb, 0)))
        row += kk
    if len(res_parts) == 1:
        in_specs.append(pl.BlockSpec((tm, n), lambda i: (i, 0)))
    else:
        assert res_parts[0].shape[0] == nhb * tm and res_parts[1].shape[0] == tm
        in_specs.append(pl.BlockSpec((tm, n), lambda i: (jnp.minimum(i, nhb - 1), 0)))
        in_specs.append(pl.BlockSpec((tm, n), lambda i: (0, 0)))
    heads = [p[0] for p in lhs_pairs]
    tails = [p[1] for p in lhs_pairs]
    return pl.pallas_call(
        functools.partial(_mm_res_kernel, n_lhs=n_lhs, n_head_blocks=nhb),
        out_shape=jax.ShapeDtypeStruct((m, n), F32),
        grid=(m // tm,),
        in_specs=in_specs,
        out_specs=pl.BlockSpec((tm, n), lambda i: (i, 0)),
        compiler_params=_params("parallel"),
        name="matmul_residual",
    )(*heads, *tails, *([w] * n_lhs), *res_parts)


def _ffn_kernel(be_ref, nv_ref, x_ref, g_ref, wg_ref, wu_ref, wd_ref, o_ref, h_ref, *, residual, packed,
                half_blocks):
    i = pl.program_id(0)
    f = pl.program_id(1)
    valid = i < nv_ref[0]

    @pl.when(f == 0)
    def _():
        if packed:
            h_ref[...] = _unpack_bf16_pairs(x_ref[...])
            o_ref[...] = jnp.zeros_like(o_ref)
        else:
            x = x_ref[...]
            h_ref[...] = _rms_bf16(x, g_ref[...])
            o_ref[...] = x if residual else jnp.zeros_like(x)

    def accumulate(n_rows):
        h = h_ref[0:n_rows, :]
        a = _dot(h, wg_ref[0].astype(BF16))
        u = _dot(h, wu_ref[0].astype(BF16))
        act = (_silu(a) * u).astype(BF16)
        o_ref[0:n_rows, :] += _dot(act, wd_ref[0].astype(BF16))

    tm = h_ref.shape[0]
    if half_blocks:
        upper_used = nv_ref[1 + i] > tm // 2

        @pl.when(jnp.logical_and(valid, upper_used))
        def _():
            accumulate(tm)

        @pl.when(jnp.logical_and(valid, jnp.logical_not(upper_used)))
        def _():
            accumulate(tm // 2)
    else:
        @pl.when(valid)
        def _():
            accumulate(tm)


def swiglu_blocks(x, g, w_gu, w_dn, block_expert, n_valid, *, residual, tm=ROW_TILE, tf=FF_TILE):
    m = x.shape[0]
    d = w_gu.shape[1]
    packed = x.dtype == jnp.uint32
    assert x.shape[1] == (d // 2 if packed else d) and not (packed and residual)
    ff = w_dn.shape[1]
    nf = ff // tf
    nb = m // tm

    def _x_map(i, f, be, nv):
        return (jnp.minimum(i, nv[0] - 1), 0)

    def _f_eff(i, f, nv):
        return jnp.where(i < nv[0], f, nf - 1)

    return pl.pallas_call(
        functools.partial(_ffn_kernel, residual=residual, packed=packed, half_blocks=packed),
        out_shape=jax.ShapeDtypeStruct((m, d), F32),
        grid_spec=pltpu.PrefetchScalarGridSpec(
            num_scalar_prefetch=2,
            grid=(nb, nf),
            in_specs=[
                pl.BlockSpec((tm, x.shape[1]), _x_map),
                pl.BlockSpec((1, d), lambda i, f, be, nv: (0, 0)),
                pl.BlockSpec((1, d, tf), lambda i, f, be, nv: (be[i], 0, _f_eff(i, f, nv))),
                pl.BlockSpec((1, d, tf), lambda i, f, be, nv: (be[i], 0, nf + _f_eff(i, f, nv))),
                pl.BlockSpec((1, tf, d), lambda i, f, be, nv: (be[i], _f_eff(i, f, nv), 0)),
            ],
            out_specs=pl.BlockSpec((tm, d), lambda i, f, be, nv: (i, 0)),
            scratch_shapes=[pltpu.VMEM((tm, d), BF16)],
        ),
        compiler_params=_params("parallel", "arbitrary"),
        name="swiglu_blocks",
    )(block_expert, n_valid, x, g, w_gu, w_gu, w_dn)


def _router_kernel(x_ref, g_ref, w_ref, o_ref):
    h = _rms_bf16(x_ref[...], g_ref[...])
    logits = _dot(h, w_ref[...])
    lane = lax.broadcasted_iota(jnp.int32, logits.shape, 1).astype(F32)
    l1 = jnp.where(lane < N_EXPERTS, logits, NEG_BIG)
    m1 = jnp.max(l1, axis=-1, keepdims=True)
    i1 = jnp.min(jnp.where(l1 == m1, lane, float(LANES)), axis=-1, keepdims=True)
    l2 = jnp.where(lane == i1, NEG_BIG, l1)
    m2 = jnp.max(l2, axis=-1, keepdims=True)
    i2 = jnp.min(jnp.where(l2 == m2, lane, float(LANES)), axis=-1, keepdims=True)
    e = jnp.exp(m2 - m1)
    g1 = 1.0 / (1.0 + e)
    g2 = e / (1.0 + e)
    out = jnp.where(lane == 0.0, i1,
                    jnp.where(lane == 1.0, i2,
                              jnp.where(lane == 2.0, g1, jnp.where(lane == 3.0, g2, 0.0))))
    o_ref[...] = out


def router(x, g, w_pad, *, tm=ROW_TILE):
    m, d = x.shape
    return pl.pallas_call(
        _router_kernel,
        out_shape=jax.ShapeDtypeStruct((m, LANES), F32),
        grid=(m // tm,),
        in_specs=[
            pl.BlockSpec((tm, d), lambda i: (i, 0)),
            pl.BlockSpec((1, d), lambda i: (0, 0)),
            pl.BlockSpec((d, LANES), lambda i: (0, 0)),
        ],
        out_specs=pl.BlockSpec((tm, LANES), lambda i: (i, 0)),
        compiler_params=_params("parallel"),
        name="router",
    )(x, g, w_pad)


def _row_copy(src, dst, src_row, dst_row, sem):
    return pltpu.make_async_copy(src.at[pl.ds(src_row, 1)], dst.at[pl.ds(dst_row, 1)], sem)


def _pack_bf16_pairs(h):
    half = h.shape[-1] // 2
    lo = lax.bitcast_convert_type(h[:, :half].astype(F32), jnp.uint32) >> 16
    hi = lax.bitcast_convert_type(h[:, half:].astype(F32), jnp.uint32) & jnp.uint32(0xFFFF0000)
    return lo | hi


def _unpack_bf16_pairs(u):
    lo = lax.bitcast_convert_type(u << 16, F32)
    hi = lax.bitcast_convert_type(u & jnp.uint32(0xFFFF0000), F32)
    return jnp.concatenate([lo, hi], axis=-1).astype(BF16)


def _dispatch_kernel(d0_ref, d1_ref, plo_ref, phi_ref, x_ref, g_ref, xs_hbm, hbuf, zrow, sem, zsem, *, tm):
    i = pl.program_id(0)
    last = pl.num_programs(0) - 1
    slot = i % 2
    base = i * tm

    zr = zrow.shape[0]

    def zero_copy(r, n):
        return pltpu.make_async_copy(zrow.at[pl.ds(0, n)], xs_hbm.at[pl.ds(r, n)], zsem)

    def for_each_unrouted_span(fn):
        for e in range(N_EXPERTS):
            lo = plo_ref[e]
            hi = phi_ref[e]
            lo_al = jnp.minimum((lo + SUBLANES - 1) // SUBLANES * SUBLANES, hi)
            n_big = (hi - lo_al) // zr

            def big(k, c, lo_al=lo_al):
                fn(pl.multiple_of(lo_al + k * zr, SUBLANES), zr)
                return c

            def small(r, c):
                fn(r, 1)
                return c

            lax.fori_loop(lo, lo_al, small, 0)
            lax.fori_loop(0, n_big, big, 0)
            lax.fori_loop(lo_al + n_big * zr, hi, small, 0)

    @pl.when(i == 0)
    def _():
        zrow[...] = jnp.zeros_like(zrow)
        for_each_unrouted_span(lambda r, n: zero_copy(r, n).start())

    hbuf[slot] = _pack_bf16_pairs(_rms_bf16(x_ref[...], g_ref[...]))

    def start(r, c):
        _row_copy(hbuf.at[slot], xs_hbm, r, d0_ref[base + r], sem.at[slot]).start()
        _row_copy(hbuf.at[slot], xs_hbm, r, d1_ref[base + r], sem.at[slot]).start()
        return c

    lax.fori_loop(0, tm, start, 0, unroll=DMA_ISSUE_UNROLL)

    def wait_slot(s):
        for _ in range(TOP_K):
            pltpu.make_async_copy(hbuf.at[s], xs_hbm.at[pl.ds(0, tm)], sem.at[s]).wait()

    @pl.when(i >= 1)
    def _():
        wait_slot(1 - slot)

    @pl.when(i == last)
    def _():
        wait_slot(slot)
        for_each_unrouted_span(lambda r, n: zero_copy(r, n).wait())


def moe_dispatch(x, g, d0, d1, pad_lo, pad_hi, n_rows, *, tm=ROW_TILE):
    m, d = x.shape
    return pl.pallas_call(
        functools.partial(_dispatch_kernel, tm=tm),
        out_shape=jax.ShapeDtypeStruct((n_rows, d // 2), jnp.uint32),
        grid_spec=pltpu.PrefetchScalarGridSpec(
            num_scalar_prefetch=4,
            grid=(m // tm,),
            in_specs=[
                pl.BlockSpec((tm, d), lambda i, *_: (i, 0)),
                pl.BlockSpec((1, d), lambda i, *_: (0, 0)),
            ],
            out_specs=pl.BlockSpec(memory_space=pl.ANY),
            scratch_shapes=[pltpu.VMEM((2, tm, d // 2), jnp.uint32), pltpu.VMEM((ZERO_FILL_ROWS, d // 2), jnp.uint32),
                            pltpu.SemaphoreType.DMA((2,)), pltpu.SemaphoreType.DMA],
        ),
        compiler_params=_params("arbitrary"),
        name="moe_dispatch",
    )(d0, d1, pad_lo, pad_hi, x, g)


def _combine_kernel(d0_ref, d1_ref, x_ref, r_ref, g_ref, y_hbm, o_ref, buf, sem, *, tm, row0):
    i = pl.program_id(0)
    slot = i % 2

    def issue(block, slot_):
        base = row0 + block * tm

        def start(r, c):
            _row_copy(y_hbm, buf.at[slot_], d0_ref[base + r], r, sem.at[slot_]).start()
            _row_copy(y_hbm, buf.at[slot_], d1_ref[base + r], tm + r, sem.at[slot_]).start()
            return c

        lax.fori_loop(0, tm, start, 0, unroll=DMA_ISSUE_UNROLL)

    @pl.when(i == 0)
    def _():
        issue(0, 0)

    @pl.when(i + 1 < pl.num_programs(0))
    def _():
        issue(i + 1, 1 - slot)

    pltpu.make_async_copy(y_hbm.at[pl.ds(0, 2 * tm)], buf.at[slot], sem.at[slot]).wait()
    rt = r_ref[...]
    g0 = rt[:, 2:3]
    g1 = rt[:, 3:4]
    x = x_ref[...] + (buf[slot, 0:tm, :] * g0 + buf[slot, tm:2 * tm, :] * g1)
    ms = jnp.mean(x * x, axis=-1, keepdims=True)
    o_ref[...] = x * lax.rsqrt(ms + EPS) * g_ref[...]


def moe_combine_norm(x, route, g, y_rows, d0, d1, *, row0, nrows, tm=COMBINE_TILE):
    d = x.shape[1]
    rb0 = row0 // tm
    return pl.pallas_call(
        functools.partial(_combine_kernel, tm=tm, row0=row0),
        out_shape=jax.ShapeDtypeStruct((nrows, d), F32),
        grid_spec=pltpu.PrefetchScalarGridSpec(
            num_scalar_prefetch=2,
            grid=(nrows // tm,),
            in_specs=[
                pl.BlockSpec((tm, d), lambda i, a, b: (rb0 + i, 0)),
                pl.BlockSpec((tm, LANES), lambda i, a, b: (rb0 + i, 0)),
                pl.BlockSpec((1, d), lambda i, a, b: (0, 0)),
                pl.BlockSpec(memory_space=pl.ANY),
            ],
            out_specs=pl.BlockSpec((tm, d), lambda i, a, b: (i, 0)),
            scratch_shapes=[pltpu.VMEM((2, 2 * tm, d), F32), pltpu.SemaphoreType.DMA((2,))],
        ),
        compiler_params=_params("arbitrary"),
        name="moe_combine",
    )(d0, d1, x, route, g, y_rows)


def _gate_kernel(x_ref, g_ref, wlr_ref, wa2_ref, b_ref, o_ref):
    h = _rms_bf16(x_ref[...], g_ref[...])
    a_lr = _dot(h, wlr_ref[...])
    z = _dot(a_lr.astype(BF16), wa2_ref[...]) + b_ref[...]
    o_ref[...] = (jnp.minimum(z, 0.0) - jnp.log(1.0 + jnp.exp(-jnp.abs(z)))) * (1.0 / GATE_TAU)


def gla_gate(x, g, w_lr_pad, w_a2_pad, b_a, *, tm=ROW_TILE):
    m, d = x.shape
    n = w_a2_pad.shape[1]
    return pl.pallas_call(
        _gate_kernel,
        out_shape=jax.ShapeDtypeStruct((m, n), F32),
        grid=(m // tm,),
        in_specs=[
            pl.BlockSpec((tm, d), lambda i: (i, 0)),
            pl.BlockSpec((1, d), lambda i: (0, 0)),
            pl.BlockSpec((d, LANES), lambda i: (0, 0)),
            pl.BlockSpec((LANES, n), lambda i: (0, 0)),
            pl.BlockSpec((1, n), lambda i: (0, 0)),
        ],
        out_specs=pl.BlockSpec((tm, n), lambda i: (i, 0)),
        compiler_params=_params("parallel"),
        name="gla_gate",
    )(x, g, w_lr_pad, w_a2_pad, b_a)


def _rotary(x, cos, sin):
    half = x.shape[-1] // 2
    x1, x2 = x[:, :half], x[:, half:]
    return jnp.concatenate([x1 * cos - x2 * sin, x1 * sin + x2 * cos], axis=-1)


def _retention_kernel(lg_ref, *refs, L, has_state):
    if has_state:
        q_ref, k_ref, v_ref, g_ref, cos_ref, sin_ref, s0_ref, o_ref, sout_ref, s_ref = refs
    else:
        q_ref, k_ref, v_ref, g_ref, cos_ref, sin_ref, o_ref, sout_ref, s_ref = refs
    c = pl.program_id(1)

    @pl.when(c == 0)
    def _():
        if has_state:
            s_ref[...] = s0_ref[0]
        else:
            s_ref[...] = jnp.zeros_like(s_ref)

    cos = cos_ref[...]
    sin = sin_ref[...]
    n_col = lax.broadcasted_iota(jnp.int32, (L, 1), 0).astype(F32)
    n_row = lax.broadcasted_iota(jnp.int32, (1, L), 1).astype(F32)
    diff = n_col - n_row
    for h in range(HA):
        lg = lg_ref[h]
        qs = slice(h * DA, (h + 1) * DA)
        vs = slice(h * DVA, (h + 1) * DVA)
        qr = _rotary(q_ref[:, qs], cos, sin)
        kr = _rotary(k_ref[:, qs], cos, sin) * (DA ** -0.5)
        vb = v_ref[:, vs].astype(BF16)
        decay = jnp.where(diff >= 0.0, jnp.exp(jnp.maximum(diff, 0.0) * lg), 0.0)
        qb = qr.astype(BF16)
        scores = _dot_nt(qb, kr.astype(BF16)) * decay
        inner = _dot(scores.astype(BF16), vb)
        state = s_ref[h]
        cross = _dot(qb, state.astype(BF16)) * jnp.exp((n_col + 1.0) * lg)
        k_dec = (kr * jnp.exp((L - 1.0 - n_col) * lg)).astype(BF16)
        s_ref[h] = state * jnp.exp(jnp.zeros((1, 1), F32) + L * lg) + _dot_tn(k_dec, vb)
        ret = inner + cross
        ret = ret - jnp.mean(ret, axis=-1, keepdims=True)
        ret = ret * lax.rsqrt(jnp.mean(ret * ret, axis=-1, keepdims=True) + EPS)
        o_ref[:, vs] = (_silu(g_ref[:, vs]) * ret).astype(o_ref.dtype)

    @pl.when(c == pl.num_programs(1) - 1)
    def _():
        sout_ref[0] = s_ref[...]


def retention(proj, cos, sin, log_gamma, *, nbatch, seq, row0, pos0, L, state0=None):
    nc = seq // L
    rb0 = row0 // L
    pb0 = pos0 // L
    has_state = state0 is not None
    wq = HA * DA
    assert HA * DVA == wq

    def col(k):
        return lambda b, c, lg: (rb0 + b * nc + c, k)

    in_specs = [
        pl.BlockSpec((L, wq), col(0)),
        pl.BlockSpec((L, wq), col(1)),
        pl.BlockSpec((L, wq), col(2)),
        pl.BlockSpec((L, wq), col(3)),
        pl.BlockSpec((L, DA // 2), lambda b, c, lg: (pb0 + c, 0)),
        pl.BlockSpec((L, DA // 2), lambda b, c, lg: (pb0 + c, 0)),
    ]
    args = [proj, proj, proj, proj, cos, sin]
    if has_state:
        in_specs.append(pl.BlockSpec((1, HA, DA, DVA), lambda b, c, lg: (b, 0, 0, 0)))
        args.append(state0)
    return pl.pallas_call(
        functools.partial(_retention_kernel, L=L, has_state=has_state),
        out_shape=(jax.ShapeDtypeStruct((nbatch * seq, HA * DVA), BF16),
                   jax.ShapeDtypeStruct((nbatch, HA, DA, DVA), F32)),
        grid_spec=pltpu.PrefetchScalarGridSpec(
            num_scalar_prefetch=1,
            grid=(nbatch, nc),
            in_specs=in_specs,
            out_specs=(pl.BlockSpec((L, wq), lambda b, c, lg: (b * nc + c, 0)),
                       pl.BlockSpec((1, HA, DA, DVA), lambda b, c, lg: (b, 0, 0, 0))),
            scratch_shapes=[pltpu.VMEM((HA, DA, DVA), F32)],
        ),
        compiler_params=_params("parallel", "arbitrary"),
        name="retention",
    )(log_gamma, *args)


def _lambda_value(l_ref, lam_init):
    lv = l_ref[...]
    a = jnp.sum(lv[0:1] * lv[1:2], axis=-1, keepdims=True)
    b = jnp.sum(lv[2:3] * lv[3:4], axis=-1, keepdims=True)
    return jnp.exp(a) - jnp.exp(b) + lam_init


def _head_norm_scale(o, scale):
    return o * lax.rsqrt(jnp.mean(o * o, axis=-1, keepdims=True) + EPS) * scale


def _diff_softmax_pv(q, key_parts, val_parts, masks, lam):
    w = None
    for c in range(2):
        qc = q[:, c * DB:(c + 1) * DB].astype(BF16)
        s = [_dot_nt(qc, k[:, c * DB:(c + 1) * DB]) for k in key_parts]
        s = [x if m is None else jnp.where(m, x, NEG_BIG) for x, m in zip(s, masks)]
        mx = functools.reduce(jnp.maximum, [jnp.max(x, axis=-1, keepdims=True) for x in s])
        p = [jnp.exp(x - mx) for x in s]
        inv = 1.0 / functools.reduce(lambda a, b: a + b, [jnp.sum(x, axis=-1, keepdims=True) for x in p])
        if c == 0:
            w = [x * inv for x in p]
        else:
            w = [a - lam * (x * inv) for a, x in zip(w, p)]
    outs = [_dot(a.astype(BF16), v) for a, v in zip(w, val_parts)]
    return functools.reduce(lambda a, b: a + b, outs)


def _dattn_prompt_kernel(q_ref, k_ref, v_ref, l_ref, o_ref, *, tq, nq, lam_init):
    i = pl.program_id(2)
    lam = _lambda_value(l_ref, lam_init)
    r_chunk = lax.broadcasted_iota(jnp.int32, (tq, tq), 0) // CHUNK
    c_chunk = lax.broadcasted_iota(jnp.int32, (tq, tq), 1) // CHUNK
    diag_mask = c_chunk <= r_chunk

    for n in range(nq):
        @pl.when(i == n)
        def _(n=n):
            q = q_ref[...] * (DB ** -0.5)
            lo = n * tq
            keys = [k_ref[lo:lo + tq, :]]
            vals = [v_ref[lo:lo + tq, :]]
            masks = [diag_mask]
            if n > 0:
                keys.insert(0, k_ref[0:lo, :])
                vals.insert(0, v_ref[0:lo, :])
                masks.insert(0, None)
            o = _diff_softmax_pv(q, keys, vals, masks, lam)
            o_ref[...] = _head_norm_scale(o, 1.0 - lam_init).astype(o_ref.dtype)


def diff_attention_prompt(proj, k_bf, v_bf, lam_params, *, nbatch, seq, lam_init, tq=ATTN_TILE):
    nq = seq // tq
    assert tq % CHUNK == 0
    return pl.pallas_call(
        functools.partial(_dattn_prompt_kernel, tq=tq, nq=nq, lam_init=lam_init),
        out_shape=jax.ShapeDtypeStruct((nbatch * seq, HB * DVB), BF16),
        grid=(nbatch, HB, nq),
        in_specs=[
            pl.BlockSpec((tq, 2 * DB), lambda b, h, i: (b * nq + i, 4 * HA + h)),
            pl.BlockSpec((seq, 2 * DB), lambda b, h, i: (b, h)),
            pl.BlockSpec((seq, DVB), lambda b, h, i: (b, h)),
            pl.BlockSpec((4, DB), lambda b, h, i: (0, 0)),
        ],
        out_specs=pl.BlockSpec((tq, DVB), lambda b, h, i: (b * nq + i, h)),
        compiler_params=_params("parallel", "parallel", "arbitrary"),
        name="diff_attention_prompt",
    )(proj, k_bf, v_bf, lam_params)


def _dattn_sample_kernel(q_ref, kn_ref, vn_ref, kc_ref, vc_ref, l_ref, o_ref, *, lam_init):
    q = q_ref[...] * (DB ** -0.5)
    lam = _lambda_value(l_ref, lam_init)
    keys = [kc_ref[0].astype(BF16), kn_ref[...]]
    vals = [vc_ref[0].astype(BF16), vn_ref[...]]
    o = _diff_softmax_pv(q, keys, vals, [None, None], lam)
    o_ref[...] = _head_norm_scale(o, 1.0 - lam_init).astype(o_ref.dtype)


def diff_attention_sample(proj, k_new, v_new, cache_k, cache_v, lam_params, *, nbatch, seq, row0, lam_init):
    past = cache_k.shape[1]
    assert seq == CHUNK and past % CHUNK == 0 and row0 % seq == 0
    rb0 = row0 // seq
    return pl.pallas_call(
        functools.partial(_dattn_sample_kernel, lam_init=lam_init),
        out_shape=jax.ShapeDtypeStruct((nbatch * seq, HB * DVB), BF16),
        grid=(nbatch, HB),
        in_specs=[
            pl.BlockSpec((seq, 2 * DB), lambda b, h: (rb0 + b, 4 * HA + h)),
            pl.BlockSpec((seq, 2 * DB), lambda b, h: (b, h)),
            pl.BlockSpec((seq, DVB), lambda b, h: (b, h)),
            pl.BlockSpec((1, past, 2 * DB), lambda b, h: (b, 0, h)),
            pl.BlockSpec((1, past, DVB), lambda b, h: (b, 0, h)),
            pl.BlockSpec((4, DB), lambda b, h: (0, 0)),
        ],
        out_specs=pl.BlockSpec((seq, DVB), lambda b, h: (b, h)),
        compiler_params=_params("parallel", "parallel"),
        name="diff_attention_sample",
    )(proj, k_new, v_new, cache_k, cache_v, lam_params)


def _split3_bf16(x):
    hi = x.astype(BF16)
    r1 = x - hi.astype(F32)
    mid = r1.astype(BF16)
    lo = (r1 - mid.astype(F32)).astype(BF16)
    return hi, mid, lo


def _gla_kernel(*refs, nsub, has_state):
    if has_state:
        q_ref, k_ref, v_ref, g_ref, a_ref, s0_ref, o_ref, sout_ref, s_ref = refs
    else:
        q_ref, k_ref, v_ref, g_ref, a_ref, o_ref, sout_ref, s_ref = refs
    c = pl.program_id(1)
    L = CHUNK

    @pl.when(c == 0)
    def _():
        for h in range(HC):
            if has_state:
                s_ref[h] = s0_ref[0, h].T
            else:
                s_ref[h] = jnp.zeros((DVC, DKC), F32)

    lb = nsub * L
    row = lax.broadcasted_iota(jnp.int32, (lb, lb), 0)
    colm = lax.broadcasted_iota(jnp.int32, (lb, lb), 1)
    causal = jnp.logical_and(row // L == colm // L, colm <= row)
    tril = jnp.where(causal, 1.0, 0.0).astype(BF16)

    hi, mid, lo = _split3_bf16(a_ref[...])
    b = _dot(tril, hi) + _dot(tril, mid) + _dot(tril, lo)

    def chunk_row(r):
        return jnp.concatenate(
            [jnp.broadcast_to(b[j * L + r:j * L + r + 1, :], (L, b.shape[1])) for j in range(nsub)], axis=0)

    ref = chunk_row(L // 2)
    b_last = chunk_row(L - 1)
    q = q_ref[...] * (DKC ** -0.5)
    k = k_ref[...]
    q_in = (q * jnp.exp(b - ref)).astype(BF16)
    k_in = (k * jnp.exp(ref - b)).astype(BF16)
    q_dec = (q * jnp.exp(b)).astype(BF16)
    k_out = (k * jnp.exp(b_last - b)).astype(BF16)
    vbs, inners = [], []
    for h in range(HC):
        ks = slice(h * DKC, (h + 1) * DKC)
        vb = v_ref[:, h * DVC:(h + 1) * DVC].astype(BF16)
        att = jnp.where(causal, _dot_nt(q_in[:, ks], k_in[:, ks]), 0.0)
        vbs.append(vb)
        inners.append(_dot(att.astype(BF16), vb))

    for j in range(nsub):
        rows = slice(j * L, (j + 1) * L)
        e_last = jnp.exp(b[j * L + L - 1:j * L + L, :])
        for h in range(HC):
            ks = slice(h * DKC, (h + 1) * DKC)
            vs = slice(h * DVC, (h + 1) * DVC)
            state_t = s_ref[h]
            cross = _dot_nt(q_dec[rows, ks], state_t.astype(BF16))
            s_ref[h] = e_last[:, ks] * state_t + _dot_tn(vbs[h][rows], k_out[rows, ks])
            o = inners[h][rows] + cross
            o = o * lax.rsqrt(jnp.mean(o * o, axis=-1, keepdims=True) + EPS)
            o_ref[rows, vs] = (_silu(g_ref[rows, vs]) * o).astype(o_ref.dtype)

    @pl.when(c == pl.num_programs(1) - 1)
    def _():
        for h in range(HC):
            sout_ref[0, h] = s_ref[h].T


def gla(proj, log_a, *, nbatch, seq, row0, rows_per_step, state0=None):
    lb = rows_per_step
    nc = seq // lb
    rb0 = row0 // lb
    has_state = state0 is not None
    wk = HC * DKC
    wv = HC * DVC
    assert wv == 2 * wk

    def rows(b, c):
        return rb0 + b * nc + c

    in_specs = [
        pl.BlockSpec((lb, wk), lambda b, c: (rows(b, c), 0)),
        pl.BlockSpec((lb, wk), lambda b, c: (rows(b, c), 1)),
        pl.BlockSpec((lb, wv), lambda b, c: (rows(b, c), 1)),
        pl.BlockSpec((lb, wv), lambda b, c: (rows(b, c), 2)),
        pl.BlockSpec((lb, wk), lambda b, c: (rows(b, c), 0)),
    ]
    args = [proj, proj, proj, proj, log_a]
    if has_state:
        in_specs.append(pl.BlockSpec((1, HC, DKC, DVC), lambda b, c: (b, 0, 0, 0)))
        args.append(state0)
    return pl.pallas_call(
        functools.partial(_gla_kernel, nsub=lb // CHUNK, has_state=has_state),
        out_shape=(jax.ShapeDtypeStruct((nbatch * seq, wv), BF16),
                   jax.ShapeDtypeStruct((nbatch, HC, DKC, DVC), F32)),
        grid=(nbatch, nc),
        in_specs=in_specs,
        out_specs=(pl.BlockSpec((lb, wv), lambda b, c: (b * nc + c, 0)),
                   pl.BlockSpec((1, HC, DKC, DVC), lambda b, c: (b, 0, 0, 0))),
        scratch_shapes=[pltpu.VMEM((HC, DVC, DKC), F32)],
        compiler_params=_params("parallel", "arbitrary"),
        name="gla",
    )(*args)


def _xattn_kernel(q_ref, k_ref, v_ref, o_ref):
    for h in range(HX):
        sl = slice(h * DX, (h + 1) * DX)
        q = q_ref[:, sl]
        k = k_ref[0, :, sl].astype(BF16)
        v = v_ref[0, :, sl].astype(BF16)
        s = _dot_nt(q, k) * (DX ** -0.5)
        m = jnp.max(s, axis=-1, keepdims=True)
        p = jnp.exp(s - m)
        p = p / jnp.sum(p, axis=-1, keepdims=True)
        o_ref[:, sl] = _dot(p.astype(BF16), v).astype(o_ref.dtype)


def cross_attention(q, mem_k, mem_v, *, nbatch, seq, row0, tq):
    d = q.shape[1]
    nq = seq // tq
    rb0 = row0 // tq
    return pl.pallas_call(
        _xattn_kernel,
        out_shape=jax.ShapeDtypeStruct((nbatch * seq, d), BF16),
        grid=(nbatch, nq),
        in_specs=[
            pl.BlockSpec((tq, d), lambda b, i: (rb0 + b * nq + i, 0)),
            pl.BlockSpec((1, N_MEM, d), lambda b, i: (b, 0, 0)),
            pl.BlockSpec((1, N_MEM, d), lambda b, i: (b, 0, 0)),
        ],
        out_specs=pl.BlockSpec((tq, d), lambda b, i: (b * nq + i, 0)),
        compiler_params=_params("parallel", "arbitrary"),
        name="cross_attention",
    )(q, mem_k, mem_v)


def _routing_tables(route, tm):
    n_tok = route.shape[0]
    n_rows = n_tok * TOP_K
    nb = (n_rows + N_EXPERTS * (tm - 1) + tm - 1) // tm
    flat_e = route[:, :TOP_K].astype(jnp.int32).reshape(-1)
    onehot = (flat_e[:, None] == jnp.arange(N_EXPERTS, dtype=jnp.int32)[None, :]).astype(jnp.int32)
    csum = jnp.cumsum(onehot, axis=0)
    rank = jnp.sum(csum * onehot, axis=1) - 1
    counts = csum[-1]
    padded = (counts + tm - 1) // tm * tm
    pad_end = jnp.cumsum(padded)
    pad_start = pad_end - padded
    dest = jnp.sum(onehot * pad_start[None, :], axis=1) + rank
    n_valid = (pad_end[-1] // tm).astype(jnp.int32)
    blk = jnp.minimum(jnp.arange(nb, dtype=jnp.int32), n_valid - 1) * tm
    block_expert = jnp.minimum(jnp.searchsorted(pad_end, blk, side='right'), N_EXPERTS - 1).astype(jnp.int32)
    dest2 = dest.reshape(n_tok, TOP_K).astype(jnp.int32)
    pad_lo = (pad_start + counts).astype(jnp.int32)
    pad_hi = pad_end.astype(jnp.int32).at[N_EXPERTS - 1].set(nb * tm)
    blk_all = jnp.arange(nb, dtype=jnp.int32) * tm
    rows_used = jnp.clip(pad_lo[block_expert] - blk_all, 0, tm).astype(jnp.int32)
    block_info = jnp.concatenate([n_valid.reshape(1), rows_used])
    return block_expert, block_info, dest2[:, 0], dest2[:, 1], pad_lo, pad_hi, nb * tm


def kernel(x_prompt, x_sample, mem_prompt, cache_diff_k, cache_diff_v, state_ret, state_gla, cache_mem_k, cache_mem_v, g_mix, g_xattn, g_mem, g_ffn, g_final, w_in_even, w_out_even, lambda_q1, lambda_k1, lambda_q2, lambda_k2, w_in_odd, w_gate_lr, b_gate_lr, w_out_odd, w_xq, w_xkv, w_xo, w_ffn_gu, w_ffn_dn, w_router, w_moe_gu, w_moe_dn):
    d = D_MODEL
    bp, tp, _ = x_prompt.shape
    bs, ts, _ = x_sample.shape
    past = cache_diff_k.shape[2]
    np_tok = bp * tp
    ns_tok = bs * ts
    n_tok = np_tok + ns_tok
    depth = g_mix.shape[0]

    x = (x_prompt.reshape(np_tok, d), x_sample.reshape(ns_tok, d))

    half = DA // 2
    inv = 1.0 / (ROPE_BASE ** jnp.linspace(0.0, 1.0, half, dtype=F32))
    pos = jnp.arange(max(tp, past + ts), dtype=jnp.int32).astype(F32)
    ang = pos[:, None] * inv[None, :]
    cos_t, sin_t = jnp.cos(ang), jnp.sin(ang)
    log_gamma = jnp.log1p(-jnp.power(2.0, -5.0 - jnp.arange(HA, dtype=F32)))


    ret_p, ret_s, gla_p, gla_s = [], [], [], []
    dk_p, dv_p, dk_s, dv_s = [], [], [], []
    mk_p, mv_p = [], []
    y_rows = route = d0 = d1 = None

    for i in range(depth):
        j = i // 2
        g_i = g_mix[i].reshape(1, d)
        if i % 2 == 0:
            w_in = w_in_even[j].astype(BF16)
            c_dk = 4 * HA * DA + HB * 2 * DB
            w_kv = HB * 2 * DB
            assert HB * DVB == w_kv
            if isinstance(x, tuple):
                (x_p, x_s), off_s = x, 0
                proj_p = norm_matmul(x_p, g_i, w_in, col0=0, ncols=c_dk, tn=1024, out_dtype=F32,
                                     tm=PROMPT_PROJ_ROW_TILE)
                proj_s = norm_matmul(x_s, g_i, w_in, col0=0, ncols=c_dk, tn=1024, out_dtype=F32)
            else:
                x_p, x_s, off_s = x, x, np_tok
                proj_p = proj_s = norm_matmul(x, g_i, w_in, col0=0, ncols=c_dk, tn=1024, out_dtype=F32,
                                              tm=PROJ_ROW_TILE)
            dk_f_p, dk_b_p, dv_f_p, dv_b_p = norm_matmul_groups(
                x_p, g_i, w_in, row0=0, nrows=np_tok, col0=c_dk, group_cols=w_kv, n_groups=2,
                out_dtypes=(F32, BF16))
            dk_f_s, dk_b_s, dv_f_s, dv_b_s = norm_matmul_groups(
                x_s, g_i, w_in, row0=off_s, nrows=ns_tok, col0=c_dk, group_cols=w_kv, n_groups=2,
                out_dtypes=(F32, BF16))
            lam_init = 0.8 - 0.6 * math.exp(-0.3 * i)
            lam_params = jnp.stack([lambda_q1[j], lambda_k1[j], lambda_q2[j], lambda_k2[j]]).astype(F32)
            a_p, s_p = retention(proj_p, cos_t, sin_t, log_gamma, nbatch=bp, seq=tp, row0=0, pos0=0, L=RET_CHUNK)
            a_s, s_s = retention(proj_s, cos_t, sin_t, log_gamma, nbatch=bs, seq=ts, row0=off_s, pos0=past,
                                 L=CHUNK, state0=state_ret[j])
            ret_p.append(s_p)
            ret_s.append(s_s)
            b_p = diff_attention_prompt(proj_p, dk_b_p, dv_b_p, lam_params, nbatch=bp, seq=tp, lam_init=lam_init)
            b_s = diff_attention_sample(
                proj_s, dk_b_s, dv_b_s, cache_diff_k[j].reshape(bs, past, HB * 2 * DB),
                cache_diff_v[j].reshape(bs, past, HB * DVB), lam_params, nbatch=bs, seq=ts, row0=off_s,
                lam_init=lam_init)
            dk_p.append(dk_f_p.reshape(bp, tp, HB, 2, DB))
            dk_s.append(dk_f_s.reshape(bs, ts, HB, 2, DB))
            dv_p.append(dv_f_p.reshape(bp, tp, HB, DVB))
            dv_s.append(dv_f_s.reshape(bs, ts, HB, DVB))
            x = matmul_residual([(a_p, a_s), (b_p, b_s)], w_out_even[j].astype(BF16), x)
        else:
            n_main = 2 * HC * DKC + 2 * HC * DVC
            w_in = w_in_odd[j]
            proj = norm_matmul(x, g_i, w_in[:, :n_main].astype(BF16), col0=0, ncols=n_main, tn=1024, out_dtype=F32,
                               tm=PROJ_ROW_TILE)
            w_lr = jnp.zeros((d, LANES), BF16).at[:, :GATE_RANK].set(w_in[:, n_main:].astype(BF16))
            w_a2 = jnp.zeros((LANES, HC * DKC), BF16).at[:GATE_RANK].set(w_gate_lr[j].astype(BF16))
            log_a = gla_gate(x, g_i, w_lr, w_a2, b_gate_lr[j].reshape(1, -1).astype(F32))
            o_p, s_p = gla(proj, log_a, nbatch=bp, seq=tp, row0=0, rows_per_step=GLA_BLOCK)
            o_s, s_s = gla(proj, log_a, nbatch=bs, seq=ts, row0=np_tok, rows_per_step=ts, state0=state_gla[j])
            gla_p.append(s_p)
            gla_s.append(s_s)
            x = matmul_residual([(o_p, o_s)], w_out_odd[j].astype(BF16), x)

        (q,) = norm_matmul_groups(x, g_xattn[i].reshape(1, d), w_xq[i].astype(BF16), row0=0, nrows=n_tok, col0=0,
                                  group_cols=HX * DX, n_groups=1, out_dtypes=(BF16,))
        mem2d = mem_prompt.reshape(bp * N_MEM, d)
        mk, mv = norm_matmul_groups(mem2d, g_mem[i].reshape(1, d), w_xkv[i].astype(BF16), row0=0,
                                    nrows=bp * N_MEM, col0=0, group_cols=HX * DX, n_groups=2, out_dtypes=(F32,),
                                    tm=MEM_ROW_TILE)
        mk_p.append(mk.reshape(bp, N_MEM, HX, DX))
        mv_p.append(mv.reshape(bp, N_MEM, HX, DX))
        o_p = cross_attention(q, mk.reshape(bp, N_MEM, HX * DX), mv.reshape(bp, N_MEM, HX * DX),
                              nbatch=bp, seq=tp, row0=0, tq=512)
        o_s = cross_attention(q, cache_mem_k[i].reshape(bs, N_MEM, HX * DX),
                              cache_mem_v[i].reshape(bs, N_MEM, HX * DX), nbatch=bs, seq=ts, row0=np_tok, tq=ts)
        x = matmul_residual([(o_p, o_s)], w_xo[i].astype(BF16), x)

        g_f = g_ffn[i].reshape(1, d)
        if i % 2 == 0:
            dense_blocks = n_tok // DENSE_FFN_ROW_TILE
            x = swiglu_blocks(x, g_f, w_ffn_gu[j][None], w_ffn_dn[j][None],
                              jnp.zeros((dense_blocks,), jnp.int32),
                              jnp.full((1 + dense_blocks,), dense_blocks, jnp.int32),
                              residual=True, tm=DENSE_FFN_ROW_TILE)
        else:
            w_r = jnp.zeros((d, LANES), BF16).at[:, :N_EXPERTS].set(w_router[j].astype(BF16))
            route = router(x, g_f, w_r)
            block_expert, n_valid, d0, d1, pad_lo, pad_hi, n_rows = _routing_tables(route, MOE_ROW_TILE)
            xs = moe_dispatch(x, g_f, d0, d1, pad_lo, pad_hi, n_rows)
            y_rows = swiglu_blocks(xs, g_f, w_moe_gu[j], w_moe_dn[j], block_expert, n_valid, residual=False,
                                   tm=MOE_ROW_TILE)
            if i != depth - 1:
                raise NotImplementedError("MoE layer must be the last layer")

    g_fin = g_final.reshape(1, d)
    y_p = moe_combine_norm(x, route, g_fin, y_rows, d0, d1, row0=0, nrows=np_tok)
    y_s = moe_combine_norm(x, route, g_fin, y_rows, d0, d1, row0=np_tok, nrows=ns_tok)

    return (y_p.reshape(bp, tp, d), y_s.reshape(bs, ts, d),
            jnp.stack(dk_p), jnp.stack(dv_p), jnp.stack(ret_p), jnp.stack(gla_p),
            jnp.stack(mk_p), jnp.stack(mv_p),
            jnp.stack(dk_s), jnp.stack(dv_s), jnp.stack(ret_s), jnp.stack(gla_s))
```

```python
import functools
import math

import jax
import jax.numpy as jnp
import numpy as np
from jax import lax
from jax.experimental import pallas as pl
from jax.experimental.pallas import tpu as pltpu

F32 = jnp.float32
BF16 = jnp.bfloat16

D_MODEL = 2048
CHUNK = 64
N_MEM = 256
EPS = 1e-6
HA, DA, DVA = 4, 256, 256
ROPE_BASE = 10000.0
HB, DB, DVB = 4, 128, 256
HC, DKC, DVC = 4, 256, 512
GATE_RANK = 16
GATE_TAU = 16.0
HX, DX = 4, 512
D_FF = 5632
N_EXPERTS = 8
TOP_K = 2
NEG_BIG = -1e30

VMEM_LIMIT_BYTES = 56 * 1024 * 1024
LANES = 128
SUBLANES = 8

ROW_TILE = 512
PROJ_ROW_TILE = 1408
PROMPT_PROJ_ROW_TILE = 1024
MEM_ROW_TILE = 256
FF_TILE = 256
DENSE_FFN_ROW_TILE = 1056
PARTIAL_BLOCK_PARTS = 4
MOE_ROW_TILE = 1024
COMBINE_TILE = 256
DMA_ISSUE_UNROLL = 8
ZERO_FILL_ROWS = 64
ATTN_TILE = 256
RET_CHUNK = 256
GLA_BLOCK = 256


def _params(*sem):
    return pltpu.CompilerParams(dimension_semantics=sem, vmem_limit_bytes=VMEM_LIMIT_BYTES)


def _resident_spec(block_shape, index_map):
    return pl.BlockSpec(block_shape, index_map, pipeline_mode=pl.Buffered(1))


def _rms_bf16(x, g):
    ms = jnp.mean(x * x, axis=-1, keepdims=True)
    return (x * lax.rsqrt(ms + EPS) * g).astype(BF16)


def _silu(x):
    return x / (1.0 + jnp.exp(-x))


def _dot(a, b):
    return jnp.dot(a, b, preferred_element_type=F32)


def _dot_nt(a, b):
    return lax.dot_general(a, b, (((1,), (1,)), ((), ())), preferred_element_type=F32)


def _dot_tn(a, b):
    return lax.dot_general(a, b, (((0,), (0,)), ((), ())), preferred_element_type=F32)


def _norm_mm_kernel(x_ref, g_ref, w_ref, o_ref, h_ref):
    @pl.when(pl.program_id(1) == 0)
    def _():
        h_ref[...] = _rms_bf16(x_ref[...], g_ref[...])

    o_ref[...] = _dot(h_ref[...], w_ref[...]).astype(o_ref.dtype)


def norm_matmul(x, g, w, *, col0, ncols, tn, out_dtype, tm=ROW_TILE):
    m, d = x.shape
    assert m % tm == 0 and ncols % tn == 0 and col0 % tn == 0
    cb0 = col0 // tn
    return pl.pallas_call(
        _norm_mm_kernel,
        out_shape=jax.ShapeDtypeStruct((m, ncols), out_dtype),
        grid=(m // tm, ncols // tn),
        in_specs=[
            pl.BlockSpec((tm, d), lambda i, j: (i, 0)),
            pl.BlockSpec((1, d), lambda i, j: (0, 0)),
            pl.BlockSpec((d, tn), lambda i, j: (0, cb0 + j)),
        ],
        out_specs=pl.BlockSpec((tm, tn), lambda i, j: (i, j)),
        scratch_shapes=[pltpu.VMEM((tm, d), BF16)],
        compiler_params=_params("parallel", "arbitrary"),
        name="norm_matmul",
    )(x, g, w)


def _norm_mm_groups_kernel(*refs, n_groups):
    x_ref, g_ref = refs[:2]
    w_refs = refs[2:2 + n_groups]
    o_refs = refs[2 + n_groups:]
    h = _rms_bf16(x_ref[...], g_ref[...])
    outs_per_group = len(o_refs) // n_groups
    for k, w_ref in enumerate(w_refs):
        y = _dot(h, w_ref[...])
        for o_ref in o_refs[k * outs_per_group:(k + 1) * outs_per_group]:
            o_ref[...] = y.astype(o_ref.dtype)


def norm_matmul_groups(x, g, w, *, row0, nrows, col0, group_cols, n_groups, out_dtypes, tm=ROW_TILE):
    d = x.shape[1]
    assert nrows % tm == 0 and row0 % tm == 0 and col0 % group_cols == 0
    rb0 = row0 // tm
    cb0 = col0 // group_cols
    in_specs = [pl.BlockSpec((tm, d), lambda i: (rb0 + i, 0)), pl.BlockSpec((1, d), lambda i: (0, 0))]
    for k in range(n_groups):
        in_specs.append(_resident_spec((d, group_cols), lambda i, k=k: (0, cb0 + k)))
    out_shape, out_specs = [], []
    for k in range(n_groups):
        for dt in out_dtypes:
            out_shape.append(jax.ShapeDtypeStruct((nrows, group_cols), dt))
            out_specs.append(pl.BlockSpec((tm, group_cols), lambda i: (i, 0)))
    return pl.pallas_call(
        functools.partial(_norm_mm_groups_kernel, n_groups=n_groups),
        out_shape=out_shape,
        grid=(nrows // tm,),
        in_specs=in_specs,
        out_specs=out_specs,
        compiler_params=_params("parallel"),
        name="norm_matmul_groups",
    )(x, g, *([w] * n_groups))


def _mm_res_kernel(*refs, n_lhs, n_head_blocks):
    head_refs = refs[:n_lhs]
    tail_refs = refs[n_lhs:2 * n_lhs]
    w_refs = refs[2 * n_lhs:3 * n_lhs]
    r_refs = refs[3 * n_lhs:-1]
    o_ref = refs[-1]

    def compute(a_refs, r_ref):
        acc = r_ref[...]
        for a_ref, w_ref in zip(a_refs, w_refs):
            acc = acc + _dot(a_ref[...], w_ref[...])
        o_ref[...] = acc

    is_head = pl.program_id(0) < n_head_blocks

    @pl.when(is_head)
    def _():
        compute(head_refs, r_refs[0])

    @pl.when(jnp.logical_not(is_head))
    def _():
        compute(tail_refs, r_refs[-1])


def matmul_residual(lhs_pairs, w, res, *, tm=ROW_TILE):
    res_parts = tuple(res) if isinstance(res, (tuple, list)) else (res,)
    m = sum(r.shape[0] for r in res_parts)
    n = res_parts[0].shape[1]
    n_lhs = len(lhs_pairs)
    nhb = lhs_pairs[0][0].shape[0] // tm
    in_specs = []
    for head, tail in lhs_pairs:
        assert head.shape[0] == nhb * tm and tail.shape[0] == tm and (nhb + 1) * tm == m
        in_specs.append(pl.BlockSpec((tm, head.shape[1]), lambda i: (jnp.minimum(i, nhb - 1), 0)))
    for head, tail in lhs_pairs:
        in_specs.append(pl.BlockSpec((tm, tail.shape[1]), lambda i: (0, 0)))
    row = 0
    for head, _ in lhs_pairs:
        kk = head.shape[1]
        assert row % kk == 0
        rb = row // kk
        in_specs.append(_resident_spec((kk, n), lambda i, rb=rb: (rb, 0)))
        row += kk
    if len(res_parts) == 1:
        in_specs.append(pl.BlockSpec((tm, n), lambda i: (i, 0)))
    else:
        assert res_parts[0].shape[0] == nhb * tm and res_parts[1].shape[0] == tm
        in_specs.append(pl.BlockSpec((tm, n), lambda i: (jnp.minimum(i, nhb - 1), 0)))
        in_specs.append(pl.BlockSpec((tm, n), lambda i: (0, 0)))
    heads = [p[0] for p in lhs_pairs]
    tails = [p[1] for p in lhs_pairs]
    return pl.pallas_call(
        functools.partial(_mm_res_kernel, n_lhs=n_lhs, n_head_blocks=nhb),
        out_shape=jax.ShapeDtypeStruct((m, n), F32),
        grid=(m // tm,),
        in_specs=in_specs,
        out_specs=pl.BlockSpec((tm, n), lambda i: (i, 0)),
        compiler_params=_params("parallel"),
        name="matmul_residual",
    )(*heads, *tails, *([w] * n_lhs), *res_parts)


def _ffn_kernel(be_ref, nv_ref, x_ref, g_ref, wg_ref, wu_ref, wd_ref, o_ref, h_ref, *, residual, packed,
                half_blocks):
    i = pl.program_id(0)
    f = pl.program_id(1)
    valid = i < nv_ref[0]

    @pl.when(f == 0)
    def _():
        if packed:
            h_ref[...] = _unpack_bf16_pairs(x_ref[...])
            o_ref[...] = jnp.zeros_like(o_ref)
        else:
            x = x_ref[...]
            h_ref[...] = _rms_bf16(x, g_ref[...])
            o_ref[...] = x if residual else jnp.zeros_like(x)

    def accumulate(n_rows):
        h = h_ref[0:n_rows, :]
        a = _dot(h, wg_ref[0].astype(BF16))
        u = _dot(h, wu_ref[0].astype(BF16))
        act = (_silu(a) * u).astype(BF16)
        o_ref[0:n_rows, :] += _dot(act, wd_ref[0].astype(BF16))

    tm = h_ref.shape[0]
    if half_blocks:
        quarter = tm // PARTIAL_BLOCK_PARTS
        parts_used = (nv_ref[1 + i] + quarter - 1) // quarter
        for parts in range(1, PARTIAL_BLOCK_PARTS + 1):
            @pl.when(jnp.logical_and(valid, parts_used == parts))
            def _(parts=parts):
                accumulate(parts * quarter)
    else:
        @pl.when(valid)
        def _():
            accumulate(tm)


def swiglu_blocks(x, g, w_gu, w_dn, block_expert, n_valid, *, residual, tm=ROW_TILE, tf=FF_TILE):
    m = x.shape[0]
    d = w_gu.shape[1]
    packed = x.dtype == jnp.uint32
    assert x.shape[1] == (d // 2 if packed else d) and not (packed and residual)
    ff = w_dn.shape[1]
    nf = ff // tf
    nb = m // tm

    def _x_map(i, f, be, nv):
        return (jnp.minimum(i, nv[0] - 1), 0)

    def _f_eff(i, f, nv):
        return jnp.where(i < nv[0], f, nf - 1)

    return pl.pallas_call(
        functools.partial(_ffn_kernel, residual=residual, packed=packed, half_blocks=packed),
        out_shape=jax.ShapeDtypeStruct((m, d), F32),
        grid_spec=pltpu.PrefetchScalarGridSpec(
            num_scalar_prefetch=2,
            grid=(nb, nf),
            in_specs=[
                pl.BlockSpec((tm, x.shape[1]), _x_map),
                pl.BlockSpec((1, d), lambda i, f, be, nv: (0, 0)),
                pl.BlockSpec((1, d, tf), lambda i, f, be, nv: (be[i], 0, _f_eff(i, f, nv))),
                pl.BlockSpec((1, d, tf), lambda i, f, be, nv: (be[i], 0, nf + _f_eff(i, f, nv))),
                pl.BlockSpec((1, tf, d), lambda i, f, be, nv: (be[i], _f_eff(i, f, nv), 0)),
            ],
            out_specs=pl.BlockSpec((tm, d), lambda i, f, be, nv: (i, 0)),
            scratch_shapes=[pltpu.VMEM((tm, d), BF16)],
        ),
        compiler_params=_params("parallel", "arbitrary"),
        name="swiglu_blocks",
    )(block_expert, n_valid, x, g, w_gu, w_gu, w_dn)


def _router_kernel(x_ref, g_ref, w_ref, o_ref):
    h = _rms_bf16(x_ref[...], g_ref[...])
    logits = _dot(h, w_ref[...])
    lane = lax.broadcasted_iota(jnp.int32, logits.shape, 1).astype(F32)
    l1 = jnp.where(lane < N_EXPERTS, logits, NEG_BIG)
    m1 = jnp.max(l1, axis=-1, keepdims=True)
    i1 = jnp.min(jnp.where(l1 == m1, lane, float(LANES)), axis=-1, keepdims=True)
    l2 = jnp.where(lane == i1, NEG_BIG, l1)
    m2 = jnp.max(l2, axis=-1, keepdims=True)
    i2 = jnp.min(jnp.where(l2 == m2, lane, float(LANES)), axis=-1, keepdims=True)
    e = jnp.exp(m2 - m1)
    g1 = 1.0 / (1.0 + e)
    g2 = e / (1.0 + e)
    out = jnp.where(lane == 0.0, i1,
                    jnp.where(lane == 1.0, i2,
                              jnp.where(lane == 2.0, g1, jnp.where(lane == 3.0, g2, 0.0))))
    o_ref[...] = out


def router(x, g, w_pad, *, tm=ROW_TILE):
    m, d = x.shape
    return pl.pallas_call(
        _router_kernel,
        out_shape=jax.ShapeDtypeStruct((m, LANES), F32),
        grid=(m // tm,),
        in_specs=[
            pl.BlockSpec((tm, d), lambda i: (i, 0)),
            pl.BlockSpec((1, d), lambda i: (0, 0)),
            pl.BlockSpec((d, LANES), lambda i: (0, 0)),
        ],
        out_specs=pl.BlockSpec((tm, LANES), lambda i: (i, 0)),
        compiler_params=_params("parallel"),
        name="router",
    )(x, g, w_pad)


def _row_copy(src, dst, src_row, dst_row, sem):
    return pltpu.make_async_copy(src.at[pl.ds(src_row, 1)], dst.at[pl.ds(dst_row, 1)], sem)


def _pack_bf16_pairs(h):
    half = h.shape[-1] // 2
    lo = lax.bitcast_convert_type(h[:, :half].astype(F32), jnp.uint32) >> 16
    hi = lax.bitcast_convert_type(h[:, half:].astype(F32), jnp.uint32) & jnp.uint32(0xFFFF0000)
    return lo | hi


def _unpack_bf16_pairs(u):
    lo = lax.bitcast_convert_type(u << 16, F32)
    hi = lax.bitcast_convert_type(u & jnp.uint32(0xFFFF0000), F32)
    return jnp.concatenate([lo, hi], axis=-1).astype(BF16)


def _dispatch_kernel(d0_ref, d1_ref, plo_ref, phi_ref, x_ref, g_ref, xs_hbm, hbuf, zrow, sem, zsem, *, tm):
    i = pl.program_id(0)
    last = pl.num_programs(0) - 1
    slot = i % 2
    base = i * tm

    zr = zrow.shape[0]

    def zero_copy(r, n):
        return pltpu.make_async_copy(zrow.at[pl.ds(0, n)], xs_hbm.at[pl.ds(r, n)], zsem)

    def for_each_unrouted_span(fn):
        for e in range(N_EXPERTS):
            lo = plo_ref[e]
            hi = phi_ref[e]
            lo_al = jnp.minimum((lo + SUBLANES - 1) // SUBLANES * SUBLANES, hi)
            n_big = (hi - lo_al) // zr

            def big(k, c, lo_al=lo_al):
                fn(pl.multiple_of(lo_al + k * zr, SUBLANES), zr)
                return c

            def small(r, c):
                fn(r, 1)
                return c

            lax.fori_loop(lo, lo_al, small, 0)
            lax.fori_loop(0, n_big, big, 0)
            lax.fori_loop(lo_al + n_big * zr, hi, small, 0)

    @pl.when(i == 0)
    def _():
        zrow[...] = jnp.zeros_like(zrow)
        for_each_unrouted_span(lambda r, n: zero_copy(r, n).start())

    hbuf[slot] = _pack_bf16_pairs(_rms_bf16(x_ref[...], g_ref[...]))

    def start(r, c):
        _row_copy(hbuf.at[slot], xs_hbm, r, d0_ref[base + r], sem.at[slot]).start()
        _row_copy(hbuf.at[slot], xs_hbm, r, d1_ref[base + r], sem.at[slot]).start()
        return c

    lax.fori_loop(0, tm, start, 0, unroll=DMA_ISSUE_UNROLL)

    def wait_slot(s):
        for _ in range(TOP_K):
            pltpu.make_async_copy(hbuf.at[s], xs_hbm.at[pl.ds(0, tm)], sem.at[s]).wait()

    @pl.when(i >= 1)
    def _():
        wait_slot(1 - slot)

    @pl.when(i == last)
    def _():
        wait_slot(slot)
        for_each_unrouted_span(lambda r, n: zero_copy(r, n).wait())


def moe_dispatch(x, g, d0, d1, pad_lo, pad_hi, n_rows, *, tm=ROW_TILE):
    m, d = x.shape
    return pl.pallas_call(
        functools.partial(_dispatch_kernel, tm=tm),
        out_shape=jax.ShapeDtypeStruct((n_rows, d // 2), jnp.uint32),
        grid_spec=pltpu.PrefetchScalarGridSpec(
            num_scalar_prefetch=4,
            grid=(m // tm,),
            in_specs=[
                pl.BlockSpec((tm, d), lambda i, *_: (i, 0)),
                pl.BlockSpec((1, d), lambda i, *_: (0, 0)),
            ],
            out_specs=pl.BlockSpec(memory_space=pl.ANY),
            scratch_shapes=[pltpu.VMEM((2, tm, d // 2), jnp.uint32), pltpu.VMEM((ZERO_FILL_ROWS, d // 2), jnp.uint32),
                            pltpu.SemaphoreType.DMA((2,)), pltpu.SemaphoreType.DMA],
        ),
        compiler_params=_params("arbitrary"),
        name="moe_dispatch",
    )(d0, d1, pad_lo, pad_hi, x, g)


def _combine_kernel(d0_ref, d1_ref, x_ref, r_ref, g_ref, y_hbm, o_ref, buf, sem, *, tm, row0):
    i = pl.program_id(0)
    slot = i % 2

    def issue(block, slot_):
        base = row0 + block * tm

        def start(r, c):
            _row_copy(y_hbm, buf.at[slot_], d0_ref[base + r], r, sem.at[slot_]).start()
            _row_copy(y_hbm, buf.at[slot_], d1_ref[base + r], tm + r, sem.at[slot_]).start()
            return c

        lax.fori_loop(0, tm, start, 0, unroll=DMA_ISSUE_UNROLL)

    @pl.when(i == 0)
    def _():
        issue(0, 0)

    @pl.when(i + 1 < pl.num_programs(0))
    def _():
        issue(i + 1, 1 - slot)

    pltpu.make_async_copy(y_hbm.at[pl.ds(0, 2 * tm)], buf.at[slot], sem.at[slot]).wait()
    rt = r_ref[...]
    g0 = rt[:, 2:3]
    g1 = rt[:, 3:4]
    x = x_ref[...] + (buf[slot, 0:tm, :] * g0 + buf[slot, tm:2 * tm, :] * g1)
    ms = jnp.mean(x * x, axis=-1, keepdims=True)
    o_ref[...] = x * lax.rsqrt(ms + EPS) * g_ref[...]


def moe_combine_norm(x, route, g, y_rows, d0, d1, *, row0, nrows, tm=COMBINE_TILE):
    d = x.shape[1]
    rb0 = row0 // tm
    return pl.pallas_call(
        functools.partial(_combine_kernel, tm=tm, row0=row0),
        out_shape=jax.ShapeDtypeStruct((nrows, d), F32),
        grid_spec=pltpu.PrefetchScalarGridSpec(
            num_scalar_prefetch=2,
            grid=(nrows // tm,),
            in_specs=[
                pl.BlockSpec((tm, d), lambda i, a, b: (rb0 + i, 0)),
                pl.BlockSpec((tm, LANES), lambda i, a, b: (rb0 + i, 0)),
                pl.BlockSpec((1, d), lambda i, a, b: (0, 0)),
                pl.BlockSpec(memory_space=pl.ANY),
            ],
            out_specs=pl.BlockSpec((tm, d), lambda i, a, b: (i, 0)),
            scratch_shapes=[pltpu.VMEM((2, 2 * tm, d), F32), pltpu.SemaphoreType.DMA((2,))],
        ),
        compiler_params=_params("arbitrary"),
        name="moe_combine",
    )(d0, d1, x, route, g, y_rows)


def _gate_kernel(x_ref, g_ref, wlr_ref, wa2_ref, b_ref, o_ref):
    h = _rms_bf16(x_ref[...], g_ref[...])
    a_lr = _dot(h, wlr_ref[...])
    z = _dot(a_lr.astype(BF16), wa2_ref[...]) + b_ref[...]
    o_ref[...] = (jnp.minimum(z, 0.0) - jnp.log(1.0 + jnp.exp(-jnp.abs(z)))) * (1.0 / GATE_TAU)


def gla_gate(x, g, w_lr_pad, w_a2_pad, b_a, *, tm=ROW_TILE):
    m, d = x.shape
    n = w_a2_pad.shape[1]
    return pl.pallas_call(
        _gate_kernel,
        out_shape=jax.ShapeDtypeStruct((m, n), F32),
        grid=(m // tm,),
        in_specs=[
            pl.BlockSpec((tm, d), lambda i: (i, 0)),
            pl.BlockSpec((1, d), lambda i: (0, 0)),
            pl.BlockSpec((d, LANES), lambda i: (0, 0)),
            pl.BlockSpec((LANES, n), lambda i: (0, 0)),
            pl.BlockSpec((1, n), lambda i: (0, 0)),
        ],
        out_specs=pl.BlockSpec((tm, n), lambda i: (i, 0)),
        compiler_params=_params("parallel"),
        name="gla_gate",
    )(x, g, w_lr_pad, w_a2_pad, b_a)


def _rotary(x, cos, sin):
    half = x.shape[-1] // 2
    x1, x2 = x[:, :half], x[:, half:]
    return jnp.concatenate([x1 * cos - x2 * sin, x1 * sin + x2 * cos], axis=-1)


def _retention_kernel(lg_ref, *refs, L, has_state):
    if has_state:
        q_ref, k_ref, v_ref, g_ref, cos_ref, sin_ref, s0_ref, o_ref, sout_ref, s_ref = refs
    else:
        q_ref, k_ref, v_ref, g_ref, cos_ref, sin_ref, o_ref, sout_ref, s_ref = refs
    c = pl.program_id(1)

    @pl.when(c == 0)
    def _():
        if has_state:
            s_ref[...] = s0_ref[0]
        else:
            s_ref[...] = jnp.zeros_like(s_ref)

    cos = cos_ref[...]
    sin = sin_ref[...]
    n_col = lax.broadcasted_iota(jnp.int32, (L, 1), 0).astype(F32)
    n_row = lax.broadcasted_iota(jnp.int32, (1, L), 1).astype(F32)
    diff = n_col - n_row
    for h in range(HA):
        lg = lg_ref[h]
        qs = slice(h * DA, (h + 1) * DA)
        vs = slice(h * DVA, (h + 1) * DVA)
        qr = _rotary(q_ref[:, qs], cos, sin)
        kr = _rotary(k_ref[:, qs], cos, sin) * (DA ** -0.5)
        vb = v_ref[:, vs].astype(BF16)
        decay = jnp.where(diff >= 0.0, jnp.exp(jnp.maximum(diff, 0.0) * lg), 0.0)
        qb = qr.astype(BF16)
        scores = _dot_nt(qb, kr.astype(BF16)) * decay
        inner = _dot(scores.astype(BF16), vb)
        state = s_ref[h]
        cross = _dot(qb, state.astype(BF16)) * jnp.exp((n_col + 1.0) * lg)
        k_dec = (kr * jnp.exp((L - 1.0 - n_col) * lg)).astype(BF16)
        s_ref[h] = state * jnp.exp(jnp.zeros((1, 1), F32) + L * lg) + _dot_tn(k_dec, vb)
        ret = inner + cross
        ret = ret - jnp.mean(ret, axis=-1, keepdims=True)
        ret = ret * lax.rsqrt(jnp.mean(ret * ret, axis=-1, keepdims=True) + EPS)
        o_ref[:, vs] = (_silu(g_ref[:, vs]) * ret).astype(o_ref.dtype)

    @pl.when(c == pl.num_programs(1) - 1)
    def _():
        sout_ref[0] = s_ref[...]


def retention(proj, cos, sin, log_gamma, *, nbatch, seq, row0, pos0, L, state0=None):
    nc = seq // L
    rb0 = row0 // L
    pb0 = pos0 // L
    has_state = state0 is not None
    wq = HA * DA
    assert HA * DVA == wq

    def col(k):
        return lambda b, c, lg: (rb0 + b * nc + c, k)

    in_specs = [
        pl.BlockSpec((L, wq), col(0)),
        pl.BlockSpec((L, wq), col(1)),
        pl.BlockSpec((L, wq), col(2)),
        pl.BlockSpec((L, wq), col(3)),
        pl.BlockSpec((L, DA // 2), lambda b, c, lg: (pb0 + c, 0)),
        pl.BlockSpec((L, DA // 2), lambda b, c, lg: (pb0 + c, 0)),
    ]
    args = [proj, proj, proj, proj, cos, sin]
    if has_state:
        in_specs.append(pl.BlockSpec((1, HA, DA, DVA), lambda b, c, lg: (b, 0, 0, 0)))
        args.append(state0)
    return pl.pallas_call(
        functools.partial(_retention_kernel, L=L, has_state=has_state),
        out_shape=(jax.ShapeDtypeStruct((nbatch * seq, HA * DVA), BF16),
                   jax.ShapeDtypeStruct((nbatch, HA, DA, DVA), F32)),
        grid_spec=pltpu.PrefetchScalarGridSpec(
            num_scalar_prefetch=1,
            grid=(nbatch, nc),
            in_specs=in_specs,
            out_specs=(pl.BlockSpec((L, wq), lambda b, c, lg: (b * nc + c, 0)),
                       pl.BlockSpec((1, HA, DA, DVA), lambda b, c, lg: (b, 0, 0, 0))),
            scratch_shapes=[pltpu.VMEM((HA, DA, DVA), F32)],
        ),
        compiler_params=_params("parallel", "arbitrary"),
        name="retention",
    )(log_gamma, *args)


def _lambda_value(l_ref, lam_init):
    lv = l_ref[...]
    a = jnp.sum(lv[0:1] * lv[1:2], axis=-1, keepdims=True)
    b = jnp.sum(lv[2:3] * lv[3:4], axis=-1, keepdims=True)
    return jnp.exp(a) - jnp.exp(b) + lam_init


def _head_norm_scale(o, scale):
    return o * lax.rsqrt(jnp.mean(o * o, axis=-1, keepdims=True) + EPS) * scale


def _diff_softmax_pv(q, key_parts, val_parts, masks, lam):
    w = None
    for c in range(2):
        qc = q[:, c * DB:(c + 1) * DB].astype(BF16)
        s = [_dot_nt(qc, k[:, c * DB:(c + 1) * DB]) for k in key_parts]
        s = [x if m is None else jnp.where(m, x, NEG_BIG) for x, m in zip(s, masks)]
        mx = functools.reduce(jnp.maximum, [jnp.max(x, axis=-1, keepdims=True) for x in s])
        p = [jnp.exp(x - mx) for x in s]
        inv = 1.0 / functools.reduce(lambda a, b: a + b, [jnp.sum(x, axis=-1, keepdims=True) for x in p])
        if c == 0:
            w = [x * inv for x in p]
        else:
            w = [a - lam * (x * inv) for a, x in zip(w, p)]
    outs = [_dot(a.astype(BF16), v) for a, v in zip(w, val_parts)]
    return functools.reduce(lambda a, b: a + b, outs)


def _dattn_prompt_kernel(q_ref, k_ref, v_ref, l_ref, o_ref, *, tq, nq, lam_init):
    i = pl.program_id(2)
    lam = _lambda_value(l_ref, lam_init)
    r_chunk = lax.broadcasted_iota(jnp.int32, (tq, tq), 0) // CHUNK
    c_chunk = lax.broadcasted_iota(jnp.int32, (tq, tq), 1) // CHUNK
    diag_mask = c_chunk <= r_chunk

    for n in range(nq):
        @pl.when(i == n)
        def _(n=n):
            q = q_ref[...] * (DB ** -0.5)
            lo = n * tq
            keys = [k_ref[lo:lo + tq, :]]
            vals = [v_ref[lo:lo + tq, :]]
            masks = [diag_mask]
            if n > 0:
                keys.insert(0, k_ref[0:lo, :])
                vals.insert(0, v_ref[0:lo, :])
                masks.insert(0, None)
            o = _diff_softmax_pv(q, keys, vals, masks, lam)
            o_ref[...] = _head_norm_scale(o, 1.0 - lam_init).astype(o_ref.dtype)


def diff_attention_prompt(proj, k_bf, v_bf, lam_params, *, nbatch, seq, lam_init, tq=ATTN_TILE):
    nq = seq // tq
    assert tq % CHUNK == 0
    return pl.pallas_call(
        functools.partial(_dattn_prompt_kernel, tq=tq, nq=nq, lam_init=lam_init),
        out_shape=jax.ShapeDtypeStruct((nbatch * seq, HB * DVB), BF16),
        grid=(nbatch, HB, nq),
        in_specs=[
            pl.BlockSpec((tq, 2 * DB), lambda b, h, i: (b * nq + i, 4 * HA + h)),
            pl.BlockSpec((seq, 2 * DB), lambda b, h, i: (b, h)),
            pl.BlockSpec((seq, DVB), lambda b, h, i: (b, h)),
            pl.BlockSpec((4, DB), lambda b, h, i: (0, 0)),
        ],
        out_specs=pl.BlockSpec((tq, DVB), lambda b, h, i: (b * nq + i, h)),
        compiler_params=_params("parallel", "parallel", "arbitrary"),
        name="diff_attention_prompt",
    )(proj, k_bf, v_bf, lam_params)


def _dattn_sample_kernel(q_ref, kn_ref, vn_ref, kc_ref, vc_ref, l_ref, o_ref, *, lam_init):
    q = q_ref[...] * (DB ** -0.5)
    lam = _lambda_value(l_ref, lam_init)
    keys = [kc_ref[0].astype(BF16), kn_ref[...]]
    vals = [vc_ref[0].astype(BF16), vn_ref[...]]
    o = _diff_softmax_pv(q, keys, vals, [None, None], lam)
    o_ref[...] = _head_norm_scale(o, 1.0 - lam_init).astype(o_ref.dtype)


def diff_attention_sample(proj, k_new, v_new, cache_k, cache_v, lam_params, *, nbatch, seq, row0, lam_init):
    past = cache_k.shape[1]
    assert seq == CHUNK and past % CHUNK == 0 and row0 % seq == 0
    rb0 = row0 // seq
    return pl.pallas_call(
        functools.partial(_dattn_sample_kernel, lam_init=lam_init),
        out_shape=jax.ShapeDtypeStruct((nbatch * seq, HB * DVB), BF16),
        grid=(nbatch, HB),
        in_specs=[
            pl.BlockSpec((seq, 2 * DB), lambda b, h: (rb0 + b, 4 * HA + h)),
            pl.BlockSpec((seq, 2 * DB), lambda b, h: (b, h)),
            pl.BlockSpec((seq, DVB), lambda b, h: (b, h)),
            pl.BlockSpec((1, past, 2 * DB), lambda b, h: (b, 0, h)),
            pl.BlockSpec((1, past, DVB), lambda b, h: (b, 0, h)),
            pl.BlockSpec((4, DB), lambda b, h: (0, 0)),
        ],
        out_specs=pl.BlockSpec((seq, DVB), lambda b, h: (b, h)),
        compiler_params=_params("parallel", "parallel"),
        name="diff_attention_sample",
    )(proj, k_new, v_new, cache_k, cache_v, lam_params)


def _split3_bf16(x):
    hi = x.astype(BF16)
    r1 = x - hi.astype(F32)
    mid = r1.astype(BF16)
    lo = (r1 - mid.astype(F32)).astype(BF16)
    return hi, mid, lo


def _gla_kernel(*refs, nsub, has_state):
    if has_state:
        q_ref, k_ref, v_ref, g_ref, a_ref, s0_ref, o_ref, sout_ref, s_ref = refs
    else:
        q_ref, k_ref, v_ref, g_ref, a_ref, o_ref, sout_ref, s_ref = refs
    c = pl.program_id(1)
    L = CHUNK

    @pl.when(c == 0)
    def _():
        for h in range(HC):
            if has_state:
                s_ref[h] = s0_ref[0, h].T
            else:
                s_ref[h] = jnp.zeros((DVC, DKC), F32)

    lb = nsub * L
    row = lax.broadcasted_iota(jnp.int32, (lb, lb), 0)
    colm = lax.broadcasted_iota(jnp.int32, (lb, lb), 1)
    causal = jnp.logical_and(row // L == colm // L, colm <= row)
    tril = jnp.where(causal, 1.0, 0.0).astype(BF16)

    hi, mid, lo = _split3_bf16(a_ref[...])
    b = _dot(tril, hi) + _dot(tril, mid) + _dot(tril, lo)

    def chunk_row(r):
        return jnp.concatenate(
            [jnp.broadcast_to(b[j * L + r:j * L + r + 1, :], (L, b.shape[1])) for j in range(nsub)], axis=0)

    ref = chunk_row(L // 2)
    b_last = chunk_row(L - 1)
    q = q_ref[...] * (DKC ** -0.5)
    k = k_ref[...]
    q_in = (q * jnp.exp(b - ref)).astype(BF16)
    k_in = (k * jnp.exp(ref - b)).astype(BF16)
    q_dec = (q * jnp.exp(b)).astype(BF16)
    k_out = (k * jnp.exp(b_last - b)).astype(BF16)
    vbs, inners = [], []
    for h in range(HC):
        ks = slice(h * DKC, (h + 1) * DKC)
        vb = v_ref[:, h * DVC:(h + 1) * DVC].astype(BF16)
        att = jnp.where(causal, _dot_nt(q_in[:, ks], k_in[:, ks]), 0.0)
        vbs.append(vb)
        inners.append(_dot(att.astype(BF16), vb))

    for j in range(nsub):
        rows = slice(j * L, (j + 1) * L)
        e_last = jnp.exp(b[j * L + L - 1:j * L + L, :])
        for h in range(HC):
            ks = slice(h * DKC, (h + 1) * DKC)
            vs = slice(h * DVC, (h + 1) * DVC)
            state_t = s_ref[h]
            cross = _dot_nt(q_dec[rows, ks], state_t.astype(BF16))
            s_ref[h] = e_last[:, ks] * state_t + _dot_tn(vbs[h][rows], k_out[rows, ks])
            o = inners[h][rows] + cross
            o = o * lax.rsqrt(jnp.mean(o * o, axis=-1, keepdims=True) + EPS)
            o_ref[rows, vs] = (_silu(g_ref[rows, vs]) * o).astype(o_ref.dtype)

    @pl.when(c == pl.num_programs(1) - 1)
    def _():
        for h in range(HC):
            sout_ref[0, h] = s_ref[h].T


def gla(proj, log_a, *, nbatch, seq, row0, rows_per_step, state0=None):
    lb = rows_per_step
    nc = seq // lb
    rb0 = row0 // lb
    has_state = state0 is not None
    wk = HC * DKC
    wv = HC * DVC
    assert wv == 2 * wk

    def rows(b, c):
        return rb0 + b * nc + c

    in_specs = [
        pl.BlockSpec((lb, wk), lambda b, c: (rows(b, c), 0)),
        pl.BlockSpec((lb, wk), lambda b, c: (rows(b, c), 1)),
        pl.BlockSpec((lb, wv), lambda b, c: (rows(b, c), 1)),
        pl.BlockSpec((lb, wv), lambda b, c: (rows(b, c), 2)),
        pl.BlockSpec((lb, wk), lambda b, c: (rows(b, c), 0)),
    ]
    args = [proj, proj, proj, proj, log_a]
    if has_state:
        in_specs.append(pl.BlockSpec((1, HC, DKC, DVC), lambda b, c: (b, 0, 0, 0)))
        args.append(state0)
    return pl.pallas_call(
        functools.partial(_gla_kernel, nsub=lb // CHUNK, has_state=has_state),
        out_shape=(jax.ShapeDtypeStruct((nbatch * seq, wv), BF16),
                   jax.ShapeDtypeStruct((nbatch, HC, DKC, DVC), F32)),
        grid=(nbatch, nc),
        in_specs=in_specs,
        out_specs=(pl.BlockSpec((lb, wv), lambda b, c: (b * nc + c, 0)),
                   pl.BlockSpec((1, HC, DKC, DVC), lambda b, c: (b, 0, 0, 0))),
        scratch_shapes=[pltpu.VMEM((HC, DVC, DKC), F32)],
        compiler_params=_params("parallel", "arbitrary"),
        name="gla",
    )(*args)


def _xattn_kernel(q_ref, k_ref, v_ref, o_ref):
    for h in range(HX):
        sl = slice(h * DX, (h + 1) * DX)
        q = q_ref[:, sl]
        k = k_ref[0, :, sl].astype(BF16)
        v = v_ref[0, :, sl].astype(BF16)
        s = _dot_nt(q, k) * (DX ** -0.5)
        m = jnp.max(s, axis=-1, keepdims=True)
        p = jnp.exp(s - m)
        p = p / jnp.sum(p, axis=-1, keepdims=True)
        o_ref[:, sl] = _dot(p.astype(BF16), v).astype(o_ref.dtype)


def cross_attention(q, mem_k, mem_v, *, nbatch, seq, row0, tq):
    d = q.shape[1]
    nq = seq // tq
    rb0 = row0 // tq
    return pl.pallas_call(
        _xattn_kernel,
        out_shape=jax.ShapeDtypeStruct((nbatch * seq, d), BF16),
        grid=(nbatch, nq),
        in_specs=[
            pl.BlockSpec((tq, d), lambda b, i: (rb0 + b * nq + i, 0)),
            pl.BlockSpec((1, N_MEM, d), lambda b, i: (b, 0, 0)),
            pl.BlockSpec((1, N_MEM, d), lambda b, i: (b, 0, 0)),
        ],
        out_specs=pl.BlockSpec((tq, d), lambda b, i: (b * nq + i, 0)),
        compiler_params=_params("parallel", "arbitrary"),
        name="cross_attention",
    )(q, mem_k, mem_v)


def _routing_tables(route, tm):
    n_tok = route.shape[0]
    n_rows = n_tok * TOP_K
    nb = (n_rows + N_EXPERTS * (tm - 1) + tm - 1) // tm
    flat_e = route[:, :TOP_K].astype(jnp.int32).reshape(-1)
    onehot = (flat_e[:, None] == jnp.arange(N_EXPERTS, dtype=jnp.int32)[None, :]).astype(jnp.int32)
    csum = jnp.cumsum(onehot, axis=0)
    rank = jnp.sum(csum * onehot, axis=1) - 1
    counts = csum[-1]
    padded = (counts + tm - 1) // tm * tm
    pad_end = jnp.cumsum(padded)
    pad_start = pad_end - padded
    dest = jnp.sum(onehot * pad_start[None, :], axis=1) + rank
    n_valid = (pad_end[-1] // tm).astype(jnp.int32)
    blk = jnp.minimum(jnp.arange(nb, dtype=jnp.int32), n_valid - 1) * tm
    block_expert = jnp.minimum(jnp.searchsorted(pad_end, blk, side='right'), N_EXPERTS - 1).astype(jnp.int32)
    dest2 = dest.reshape(n_tok, TOP_K).astype(jnp.int32)
    pad_lo = (pad_start + counts).astype(jnp.int32)
    pad_hi = pad_end.astype(jnp.int32).at[N_EXPERTS - 1].set(nb * tm)
    blk_all = jnp.arange(nb, dtype=jnp.int32) * tm
    rows_used = jnp.clip(pad_lo[block_expert] - blk_all, 0, tm).astype(jnp.int32)
    block_info = jnp.concatenate([n_valid.reshape(1), rows_used])
    return block_expert, block_info, dest2[:, 0], dest2[:, 1], pad_lo, pad_hi, nb * tm


def kernel(x_prompt, x_sample, mem_prompt, cache_diff_k, cache_diff_v, state_ret, state_gla, cache_mem_k, cache_mem_v, g_mix, g_xattn, g_mem, g_ffn, g_final, w_in_even, w_out_even, lambda_q1, lambda_k1, lambda_q2, lambda_k2, w_in_odd, w_gate_lr, b_gate_lr, w_out_odd, w_xq, w_xkv, w_xo, w_ffn_gu, w_ffn_dn, w_router, w_moe_gu, w_moe_dn):
    d = D_MODEL
    bp, tp, _ = x_prompt.shape
    bs, ts, _ = x_sample.shape
    past = cache_diff_k.shape[2]
    np_tok = bp * tp
    ns_tok = bs * ts
    n_tok = np_tok + ns_tok
    depth = g_mix.shape[0]

    x = (x_prompt.reshape(np_tok, d), x_sample.reshape(ns_tok, d))

    half = DA // 2
    inv = 1.0 / (ROPE_BASE ** jnp.linspace(0.0, 1.0, half, dtype=F32))
    pos = jnp.arange(max(tp, past + ts), dtype=jnp.int32).astype(F32)
    ang = pos[:, None] * inv[None, :]
    cos_t, sin_t = jnp.cos(ang), jnp.sin(ang)
    log_gamma = jnp.log1p(-jnp.power(2.0, -5.0 - jnp.arange(HA, dtype=F32)))


    ret_p, ret_s, gla_p, gla_s = [], [], [], []
    dk_p, dv_p, dk_s, dv_s = [], [], [], []
    mk_p, mv_p = [], []
    y_rows = route = d0 = d1 = None

    for i in range(depth):
        j = i // 2
        g_i = g_mix[i].reshape(1, d)
        if i % 2 == 0:
            w_in = w_in_even[j].astype(BF16)
            c_dk = 4 * HA * DA + HB * 2 * DB
            w_kv = HB * 2 * DB
            assert HB * DVB == w_kv
            if isinstance(x, tuple):
                (x_p, x_s), off_s = x, 0
                proj_p = norm_matmul(x_p, g_i, w_in, col0=0, ncols=c_dk, tn=1024, out_dtype=F32,
                                     tm=PROMPT_PROJ_ROW_TILE)
                proj_s = norm_matmul(x_s, g_i, w_in, col0=0, ncols=c_dk, tn=1024, out_dtype=F32)
            else:
                x_p, x_s, off_s = x, x, np_tok
                proj_p = proj_s = norm_matmul(x, g_i, w_in, col0=0, ncols=c_dk, tn=1024, out_dtype=F32,
                                              tm=PROJ_ROW_TILE)
            dk_f_p, dk_b_p, dv_f_p, dv_b_p = norm_matmul_groups(
                x_p, g_i, w_in, row0=0, nrows=np_tok, col0=c_dk, group_cols=w_kv, n_groups=2,
                out_dtypes=(F32, BF16))
            dk_f_s, dk_b_s, dv_f_s, dv_b_s = norm_matmul_groups(
                x_s, g_i, w_in, row0=off_s, nrows=ns_tok, col0=c_dk, group_cols=w_kv, n_groups=2,
                out_dtypes=(F32, BF16))
            lam_init = 0.8 - 0.6 * math.exp(-0.3 * i)
            lam_params = jnp.stack([lambda_q1[j], lambda_k1[j], lambda_q2[j], lambda_k2[j]]).astype(F32)
            a_p, s_p = retention(proj_p, cos_t, sin_t, log_gamma, nbatch=bp, seq=tp, row0=0, pos0=0, L=RET_CHUNK)
            a_s, s_s = retention(proj_s, cos_t, sin_t, log_gamma, nbatch=bs, seq=ts, row0=off_s, pos0=past,
                                 L=CHUNK, state0=state_ret[j])
            ret_p.append(s_p)
            ret_s.append(s_s)
            b_p = diff_attention_prompt(proj_p, dk_b_p, dv_b_p, lam_params, nbatch=bp, seq=tp, lam_init=lam_init)
            b_s = diff_attention_sample(
                proj_s, dk_b_s, dv_b_s, cache_diff_k[j].reshape(bs, past, HB * 2 * DB),
                cache_diff_v[j].reshape(bs, past, HB * DVB), lam_params, nbatch=bs, seq=ts, row0=off_s,
                lam_init=lam_init)
            dk_p.append(dk_f_p.reshape(bp, tp, HB, 2, DB))
            dk_s.append(dk_f_s.reshape(bs, ts, HB, 2, DB))
            dv_p.append(dv_f_p.reshape(bp, tp, HB, DVB))
            dv_s.append(dv_f_s.reshape(bs, ts, HB, DVB))
            x = matmul_residual([(a_p, a_s), (b_p, b_s)], w_out_even[j].astype(BF16), x)
        else:
            n_main = 2 * HC * DKC + 2 * HC * DVC
            w_in = w_in_odd[j]
            proj = norm_matmul(x, g_i, w_in[:, :n_main].astype(BF16), col0=0, ncols=n_main, tn=1024, out_dtype=F32,
                               tm=PROJ_ROW_TILE)
            w_lr = jnp.zeros((d, LANES), BF16).at[:, :GATE_RANK].set(w_in[:, n_main:].astype(BF16))
            w_a2 = jnp.zeros((LANES, HC * DKC), BF16).at[:GATE_RANK].set(w_gate_lr[j].astype(BF16))
            log_a = gla_gate(x, g_i, w_lr, w_a2, b_gate_lr[j].reshape(1, -1).astype(F32))
            o_p, s_p = gla(proj, log_a, nbatch=bp, seq=tp, row0=0, rows_per_step=GLA_BLOCK)
            o_s, s_s = gla(proj, log_a, nbatch=bs, seq=ts, row0=np_tok, rows_per_step=ts, state0=state_gla[j])
            gla_p.append(s_p)
            gla_s.append(s_s)
            x = matmul_residual([(o_p, o_s)], w_out_odd[j].astype(BF16), x)

        (q,) = norm_matmul_groups(x, g_xattn[i].reshape(1, d), w_xq[i].astype(BF16), row0=0, nrows=n_tok, col0=0,
                                  group_cols=HX * DX, n_groups=1, out_dtypes=(BF16,))
        mem2d = mem_prompt.reshape(bp * N_MEM, d)
        mk, mv = norm_matmul_groups(mem2d, g_mem[i].reshape(1, d), w_xkv[i].astype(BF16), row0=0,
                                    nrows=bp * N_MEM, col0=0, group_cols=HX * DX, n_groups=2, out_dtypes=(F32,),
                                    tm=MEM_ROW_TILE)
        mk_p.append(mk.reshape(bp, N_MEM, HX, DX))
        mv_p.append(mv.reshape(bp, N_MEM, HX, DX))
        o_p = cross_attention(q, mk.reshape(bp, N_MEM, HX * DX), mv.reshape(bp, N_MEM, HX * DX),
                              nbatch=bp, seq=tp, row0=0, tq=512)
        o_s = cross_attention(q, cache_mem_k[i].reshape(bs, N_MEM, HX * DX),
                              cache_mem_v[i].reshape(bs, N_MEM, HX * DX), nbatch=bs, seq=ts, row0=np_tok, tq=ts)
        x = matmul_residual([(o_p, o_s)], w_xo[i].astype(BF16), x)

        g_f = g_ffn[i].reshape(1, d)
        if i % 2 == 0:
            dense_blocks = n_tok // DENSE_FFN_ROW_TILE
            x = swiglu_blocks(x, g_f, w_ffn_gu[j][None], w_ffn_dn[j][None],
                              jnp.zeros((dense_blocks,), jnp.int32),
                              jnp.full((1 + dense_blocks,), dense_blocks, jnp.int32),
                              residual=True, tm=DENSE_FFN_ROW_TILE)
        else:
            w_r = jnp.zeros((d, LANES), BF16).at[:, :N_EXPERTS].set(w_router[j].astype(BF16))
            route = router(x, g_f, w_r)
            block_expert, n_valid, d0, d1, pad_lo, pad_hi, n_rows = _routing_tables(route, MOE_ROW_TILE)
            xs = moe_dispatch(x, g_f, d0, d1, pad_lo, pad_hi, n_rows)
            y_rows = swiglu_blocks(xs, g_f, w_moe_gu[j], w_moe_dn[j], block_expert, n_valid, residual=False,
                                   tm=MOE_ROW_TILE)
            if i != depth - 1:
                raise NotImplementedError("MoE layer must be the last layer")

    g_fin = g_final.reshape(1, d)
    y_p = moe_combine_norm(x, route, g_fin, y_rows, d0, d1, row0=0, nrows=np_tok)
    y_s = moe_combine_norm(x, route, g_fin, y_rows, d0, d1, row0=np_tok, nrows=ns_tok)

    return (y_p.reshape(bp, tp, d), y_s.reshape(bs, ts, d),
            jnp.stack(dk_p), jnp.stack(dv_p), jnp.stack(ret_p), jnp.stack(gla_p),
            jnp.stack(mk_p), jnp.stack(mv_p),
            jnp.stack(dk_s), jnp.stack(dv_s), jnp.stack(ret_s), jnp.stack(gla_s))
```

```python
import functools
import math

import jax
import jax.numpy as jnp
import numpy as np
from jax import lax
from jax.experimental import pallas as pl
from jax.experimental.pallas import tpu as pltpu

F32 = jnp.float32
BF16 = jnp.bfloat16
PROJ_DTYPE = BF16

D_MODEL = 2048
CHUNK = 64
N_MEM = 256
EPS = 1e-6
HA, DA, DVA = 4, 256, 256
ROPE_BASE = 10000.0
HB, DB, DVB = 4, 128, 256
HC, DKC, DVC = 4, 256, 512
GATE_RANK = 16
GATE_TAU = 16.0
HX, DX = 4, 512
D_FF = 5632
N_EXPERTS = 8
TOP_K = 2
NEG_BIG = -1e30

VMEM_LIMIT_BYTES = 56 * 1024 * 1024
LANES = 128
SUBLANES = 8

ROW_TILE = 512
PROJ_ROW_TILE = 1408
PROMPT_PROJ_ROW_TILE = 1024
MEM_ROW_TILE = 256
FF_TILE = 256
DENSE_FFN_ROW_TILE = 1056
PARTIAL_BLOCK_PARTS = 4
MOE_ROW_TILE = 1024
COMBINE_TILE = 256
DMA_ISSUE_UNROLL = 8
ZERO_FILL_ROWS = 64
ATTN_TILE = 256
RET_CHUNK = 256
GLA_BLOCK = 256


def _params(*sem):
    return pltpu.CompilerParams(dimension_semantics=sem, vmem_limit_bytes=VMEM_LIMIT_BYTES)


def _resident_spec(block_shape, index_map):
    return pl.BlockSpec(block_shape, index_map, pipeline_mode=pl.Buffered(1))


def _rms_bf16(x, g):
    ms = jnp.mean(x * x, axis=-1, keepdims=True)
    return (x * lax.rsqrt(ms + EPS) * g).astype(BF16)


def _silu(x):
    return x / (1.0 + jnp.exp(-x))


def _dot(a, b):
    return jnp.dot(a, b, preferred_element_type=F32)


def _dot_nt(a, b):
    return lax.dot_general(a, b, (((1,), (1,)), ((), ())), preferred_element_type=F32)


def _dot_tn(a, b):
    return lax.dot_general(a, b, (((0,), (0,)), ((), ())), preferred_element_type=F32)


def _norm_mm_kernel(x_ref, g_ref, w_ref, o_ref, h_ref):
    @pl.when(pl.program_id(1) == 0)
    def _():
        h_ref[...] = _rms_bf16(x_ref[...], g_ref[...])

    o_ref[...] = _dot(h_ref[...], w_ref[...]).astype(o_ref.dtype)


def norm_matmul(x, g, w, *, col0, ncols, tn, out_dtype, tm=ROW_TILE):
    m, d = x.shape
    assert m % tm == 0 and ncols % tn == 0 and col0 % tn == 0
    cb0 = col0 // tn
    return pl.pallas_call(
        _norm_mm_kernel,
        out_shape=jax.ShapeDtypeStruct((m, ncols), out_dtype),
        grid=(m // tm, ncols // tn),
        in_specs=[
            pl.BlockSpec((tm, d), lambda i, j: (i, 0)),
            pl.BlockSpec((1, d), lambda i, j: (0, 0)),
            pl.BlockSpec((d, tn), lambda i, j: (0, cb0 + j)),
        ],
        out_specs=pl.BlockSpec((tm, tn), lambda i, j: (i, j)),
        scratch_shapes=[pltpu.VMEM((tm, d), BF16)],
        compiler_params=_params("parallel", "arbitrary"),
        name="norm_matmul",
    )(x, g, w)


def _norm_mm_groups_kernel(*refs, n_groups):
    x_ref, g_ref = refs[:2]
    w_refs = refs[2:2 + n_groups]
    o_refs = refs[2 + n_groups:]
    h = _rms_bf16(x_ref[...], g_ref[...])
    outs_per_group = len(o_refs) // n_groups
    for k, w_ref in enumerate(w_refs):
        y = _dot(h, w_ref[...])
        for o_ref in o_refs[k * outs_per_group:(k + 1) * outs_per_group]:
            o_ref[...] = y.astype(o_ref.dtype)


def norm_matmul_groups(x, g, w, *, row0, nrows, col0, group_cols, n_groups, out_dtypes, tm=ROW_TILE):
    d = x.shape[1]
    assert nrows % tm == 0 and row0 % tm == 0 and col0 % group_cols == 0
    rb0 = row0 // tm
    cb0 = col0 // group_cols
    in_specs = [pl.BlockSpec((tm, d), lambda i: (rb0 + i, 0)), pl.BlockSpec((1, d), lambda i: (0, 0))]
    for k in range(n_groups):
        in_specs.append(_resident_spec((d, group_cols), lambda i, k=k: (0, cb0 + k)))
    out_shape, out_specs = [], []
    for k in range(n_groups):
        for dt in out_dtypes:
            out_shape.append(jax.ShapeDtypeStruct((nrows, group_cols), dt))
            out_specs.append(pl.BlockSpec((tm, group_cols), lambda i: (i, 0)))
    return pl.pallas_call(
        functools.partial(_norm_mm_groups_kernel, n_groups=n_groups),
        out_shape=out_shape,
        grid=(nrows // tm,),
        in_specs=in_specs,
        out_specs=out_specs,
        compiler_params=_params("parallel"),
        name="norm_matmul_groups",
    )(x, g, *([w] * n_groups))


def _mm_res_kernel(*refs, n_lhs, n_head_blocks):
    head_refs = refs[:n_lhs]
    tail_refs = refs[n_lhs:2 * n_lhs]
    w_refs = refs[2 * n_lhs:3 * n_lhs]
    r_refs = refs[3 * n_lhs:-1]
    o_ref = refs[-1]

    def compute(a_refs, r_ref):
        acc = r_ref[...]
        for a_ref, w_ref in zip(a_refs, w_refs):
            acc = acc + _dot(a_ref[...], w_ref[...])
        o_ref[...] = acc

    is_head = pl.program_id(0) < n_head_blocks

    @pl.when(is_head)
    def _():
        compute(head_refs, r_refs[0])

    @pl.when(jnp.logical_not(is_head))
    def _():
        compute(tail_refs, r_refs[-1])


def matmul_residual(lhs_pairs, w, res, *, tm=ROW_TILE):
    res_parts = tuple(res) if isinstance(res, (tuple, list)) else (res,)
    m = sum(r.shape[0] for r in res_parts)
    n = res_parts[0].shape[1]
    n_lhs = len(lhs_pairs)
    nhb = lhs_pairs[0][0].shape[0] // tm
    in_specs = []
    for head, tail in lhs_pairs:
        assert head.shape[0] == nhb * tm and tail.shape[0] == tm and (nhb + 1) * tm == m
        in_specs.append(pl.BlockSpec((tm, head.shape[1]), lambda i: (jnp.minimum(i, nhb - 1), 0)))
    for head, tail in lhs_pairs:
        in_specs.append(pl.BlockSpec((tm, tail.shape[1]), lambda i: (0, 0)))
    row = 0
    for head, _ in lhs_pairs:
        kk = head.shape[1]
        assert row % kk == 0
        rb = row // kk
        in_specs.append(_resident_spec((kk, n), lambda i, rb=rb: (rb, 0)))
        row += kk
    if len(res_parts) == 1:
        in_specs.append(pl.BlockSpec((tm, n), lambda i: (i, 0)))
    else:
        assert res_parts[0].shape[0] == nhb * tm and res_parts[1].shape[0] == tm
        in_specs.append(pl.BlockSpec((tm, n), lambda i: (jnp.minimum(i, nhb - 1), 0)))
        in_specs.append(pl.BlockSpec((tm, n), lambda i: (0, 0)))
    heads = [p[0] for p in lhs_pairs]
    tails = [p[1] for p in lhs_pairs]
    return pl.pallas_call(
        functools.partial(_mm_res_kernel, n_lhs=n_lhs, n_head_blocks=nhb),
        out_shape=jax.ShapeDtypeStruct((m, n), F32),
        grid=(m // tm,),
        in_specs=in_specs,
        out_specs=pl.BlockSpec((tm, n), lambda i: (i, 0)),
        compiler_params=_params("parallel"),
        name="matmul_residual",
    )(*heads, *tails, *([w] * n_lhs), *res_parts)


def _ffn_kernel(be_ref, nv_ref, x_ref, g_ref, wg_ref, wu_ref, wd_ref, o_ref, h_ref, *, residual, packed,
                half_blocks):
    i = pl.program_id(0)
    f = pl.program_id(1)
    valid = i < nv_ref[0]

    @pl.when(f == 0)
    def _():
        if packed:
            h_ref[...] = _unpack_bf16_pairs(x_ref[...])
            o_ref[...] = jnp.zeros_like(o_ref)
        else:
            x = x_ref[...]
            h_ref[...] = _rms_bf16(x, g_ref[...])
            o_ref[...] = x if residual else jnp.zeros_like(x)

    def accumulate(n_rows):
        h = h_ref[0:n_rows, :]
        a = _dot(h, wg_ref[0].astype(BF16))
        u = _dot(h, wu_ref[0].astype(BF16))
        act = (_silu(a) * u).astype(BF16)
        o_ref[0:n_rows, :] += _dot(act, wd_ref[0].astype(BF16))

    tm = h_ref.shape[0]
    if half_blocks:
        quarter = tm // PARTIAL_BLOCK_PARTS
        parts_used = (nv_ref[1 + i] + quarter - 1) // quarter
        for parts in range(1, PARTIAL_BLOCK_PARTS + 1):
            @pl.when(jnp.logical_and(valid, parts_used == parts))
            def _(parts=parts):
                accumulate(parts * quarter)
    else:
        @pl.when(valid)
        def _():
            accumulate(tm)


def swiglu_blocks(x, g, w_gu, w_dn, block_expert, n_valid, *, residual, tm=ROW_TILE, tf=FF_TILE):
    m = x.shape[0]
    d = w_gu.shape[1]
    packed = x.dtype == jnp.uint32
    assert x.shape[1] == (d // 2 if packed else d) and not (packed and residual)
    ff = w_dn.shape[1]
    nf = ff // tf
    nb = m // tm

    def _x_map(i, f, be, nv):
        return (jnp.minimum(i, nv[0] - 1), 0)

    def _f_eff(i, f, nv):
        return jnp.where(i < nv[0], f, nf - 1)

    return pl.pallas_call(
        functools.partial(_ffn_kernel, residual=residual, packed=packed, half_blocks=packed),
        out_shape=jax.ShapeDtypeStruct((m, d), F32),
        grid_spec=pltpu.PrefetchScalarGridSpec(
            num_scalar_prefetch=2,
            grid=(nb, nf),
            in_specs=[
                pl.BlockSpec((tm, x.shape[1]), _x_map),
                pl.BlockSpec((1, d), lambda i, f, be, nv: (0, 0)),
                pl.BlockSpec((1, d, tf), lambda i, f, be, nv: (be[i], 0, _f_eff(i, f, nv))),
                pl.BlockSpec((1, d, tf), lambda i, f, be, nv: (be[i], 0, nf + _f_eff(i, f, nv))),
                pl.BlockSpec((1, tf, d), lambda i, f, be, nv: (be[i], _f_eff(i, f, nv), 0)),
            ],
            out_specs=pl.BlockSpec((tm, d), lambda i, f, be, nv: (i, 0)),
            scratch_shapes=[pltpu.VMEM((tm, d), BF16)],
        ),
        compiler_params=_params("parallel", "arbitrary"),
        name="swiglu_blocks",
    )(block_expert, n_valid, x, g, w_gu, w_gu, w_dn)


def _router_kernel(x_ref, g_ref, w_ref, o_ref):
    h = _rms_bf16(x_ref[...], g_ref[...])
    logits = _dot(h, w_ref[...])
    lane = lax.broadcasted_iota(jnp.int32, logits.shape, 1).astype(F32)
    l1 = jnp.where(lane < N_EXPERTS, logits, NEG_BIG)
    m1 = jnp.max(l1, axis=-1, keepdims=True)
    i1 = jnp.min(jnp.where(l1 == m1, lane, float(LANES)), axis=-1, keepdims=True)
    l2 = jnp.where(lane == i1, NEG_BIG, l1)
    m2 = jnp.max(l2, axis=-1, keepdims=True)
    i2 = jnp.min(jnp.where(l2 == m2, lane, float(LANES)), axis=-1, keepdims=True)
    e = jnp.exp(m2 - m1)
    g1 = 1.0 / (1.0 + e)
    g2 = e / (1.0 + e)
    out = jnp.where(lane == 0.0, i1,
                    jnp.where(lane == 1.0, i2,
                              jnp.where(lane == 2.0, g1, jnp.where(lane == 3.0, g2, 0.0))))
    o_ref[...] = out


def router(x, g, w_pad, *, tm=ROW_TILE):
    m, d = x.shape
    return pl.pallas_call(
        _router_kernel,
        out_shape=jax.ShapeDtypeStruct((m, LANES), F32),
        grid=(m // tm,),
        in_specs=[
            pl.BlockSpec((tm, d), lambda i: (i, 0)),
            pl.BlockSpec((1, d), lambda i: (0, 0)),
            pl.BlockSpec((d, LANES), lambda i: (0, 0)),
        ],
        out_specs=pl.BlockSpec((tm, LANES), lambda i: (i, 0)),
        compiler_params=_params("parallel"),
        name="router",
    )(x, g, w_pad)


def _row_copy(src, dst, src_row, dst_row, sem):
    return pltpu.make_async_copy(src.at[pl.ds(src_row, 1)], dst.at[pl.ds(dst_row, 1)], sem)


def _pack_bf16_pairs(h):
    half = h.shape[-1] // 2
    lo = lax.bitcast_convert_type(h[:, :half].astype(F32), jnp.uint32) >> 16
    hi = lax.bitcast_convert_type(h[:, half:].astype(F32), jnp.uint32) & jnp.uint32(0xFFFF0000)
    return lo | hi


def _unpack_bf16_pairs(u):
    lo = lax.bitcast_convert_type(u << 16, F32)
    hi = lax.bitcast_convert_type(u & jnp.uint32(0xFFFF0000), F32)
    return jnp.concatenate([lo, hi], axis=-1).astype(BF16)


def _dispatch_kernel(d0_ref, d1_ref, plo_ref, phi_ref, x_ref, g_ref, xs_hbm, hbuf, zrow, sem, zsem, *, tm):
    i = pl.program_id(0)
    last = pl.num_programs(0) - 1
    slot = i % 2
    base = i * tm

    zr = zrow.shape[0]

    def zero_copy(r, n):
        return pltpu.make_async_copy(zrow.at[pl.ds(0, n)], xs_hbm.at[pl.ds(r, n)], zsem)

    def for_each_unrouted_span(fn):
        for e in range(N_EXPERTS):
            lo = plo_ref[e]
            hi = phi_ref[e]
            lo_al = jnp.minimum((lo + SUBLANES - 1) // SUBLANES * SUBLANES, hi)
            n_big = (hi - lo_al) // zr

            def big(k, c, lo_al=lo_al):
                fn(pl.multiple_of(lo_al + k * zr, SUBLANES), zr)
                return c

            def small(r, c):
                fn(r, 1)
                return c

            lax.fori_loop(lo, lo_al, small, 0)
            lax.fori_loop(0, n_big, big, 0)
            lax.fori_loop(lo_al + n_big * zr, hi, small, 0)

    @pl.when(i == 0)
    def _():
        zrow[...] = jnp.zeros_like(zrow)
        for_each_unrouted_span(lambda r, n: zero_copy(r, n).start())

    hbuf[slot] = _pack_bf16_pairs(_rms_bf16(x_ref[...], g_ref[...]))

    def start(r, c):
        _row_copy(hbuf.at[slot], xs_hbm, r, d0_ref[base + r], sem.at[slot]).start()
        _row_copy(hbuf.at[slot], xs_hbm, r, d1_ref[base + r], sem.at[slot]).start()
        return c

    lax.fori_loop(0, tm, start, 0, unroll=DMA_ISSUE_UNROLL)

    def wait_slot(s):
        for _ in range(TOP_K):
            pltpu.make_async_copy(hbuf.at[s], xs_hbm.at[pl.ds(0, tm)], sem.at[s]).wait()

    @pl.when(i >= 1)
    def _():
        wait_slot(1 - slot)

    @pl.when(i == last)
    def _():
        wait_slot(slot)
        for_each_unrouted_span(lambda r, n: zero_copy(r, n).wait())


def moe_dispatch(x, g, d0, d1, pad_lo, pad_hi, n_rows, *, tm=ROW_TILE):
    m, d = x.shape
    return pl.pallas_call(
        functools.partial(_dispatch_kernel, tm=tm),
        out_shape=jax.ShapeDtypeStruct((n_rows, d // 2), jnp.uint32),
        grid_spec=pltpu.PrefetchScalarGridSpec(
            num_scalar_prefetch=4,
            grid=(m // tm,),
            in_specs=[
                pl.BlockSpec((tm, d), lambda i, *_: (i, 0)),
                pl.BlockSpec((1, d), lambda i, *_: (0, 0)),
            ],
            out_specs=pl.BlockSpec(memory_space=pl.ANY),
            scratch_shapes=[pltpu.VMEM((2, tm, d // 2), jnp.uint32), pltpu.VMEM((ZERO_FILL_ROWS, d // 2), jnp.uint32),
                            pltpu.SemaphoreType.DMA((2,)), pltpu.SemaphoreType.DMA],
        ),
        compiler_params=_params("arbitrary"),
        name="moe_dispatch",
    )(d0, d1, pad_lo, pad_hi, x, g)


def _combine_kernel(d0_ref, d1_ref, x_ref, r_ref, g_ref, y_hbm, o_ref, buf, sem, *, tm, row0):
    i = pl.program_id(0)
    slot = i % 2

    def issue(block, slot_):
        base = row0 + block * tm

        def start(r, c):
            _row_copy(y_hbm, buf.at[slot_], d0_ref[base + r], r, sem.at[slot_]).start()
            _row_copy(y_hbm, buf.at[slot_], d1_ref[base + r], tm + r, sem.at[slot_]).start()
            return c

        lax.fori_loop(0, tm, start, 0, unroll=DMA_ISSUE_UNROLL)

    @pl.when(i == 0)
    def _():
        issue(0, 0)

    @pl.when(i + 1 < pl.num_programs(0))
    def _():
        issue(i + 1, 1 - slot)

    pltpu.make_async_copy(y_hbm.at[pl.ds(0, 2 * tm)], buf.at[slot], sem.at[slot]).wait()
    rt = r_ref[...]
    g0 = rt[:, 2:3]
    g1 = rt[:, 3:4]
    x = x_ref[...] + (buf[slot, 0:tm, :] * g0 + buf[slot, tm:2 * tm, :] * g1)
    ms = jnp.mean(x * x, axis=-1, keepdims=True)
    o_ref[...] = x * lax.rsqrt(ms + EPS) * g_ref[...]


def moe_combine_norm(x, route, g, y_rows, d0, d1, *, row0, nrows, tm=COMBINE_TILE):
    d = x.shape[1]
    rb0 = row0 // tm
    return pl.pallas_call(
        functools.partial(_combine_kernel, tm=tm, row0=row0),
        out_shape=jax.ShapeDtypeStruct((nrows, d), F32),
        grid_spec=pltpu.PrefetchScalarGridSpec(
            num_scalar_prefetch=2,
            grid=(nrows // tm,),
            in_specs=[
                pl.BlockSpec((tm, d), lambda i, a, b: (rb0 + i, 0)),
                pl.BlockSpec((tm, LANES), lambda i, a, b: (rb0 + i, 0)),
                pl.BlockSpec((1, d), lambda i, a, b: (0, 0)),
                pl.BlockSpec(memory_space=pl.ANY),
            ],
            out_specs=pl.BlockSpec((tm, d), lambda i, a, b: (i, 0)),
            scratch_shapes=[pltpu.VMEM((2, 2 * tm, d), F32), pltpu.SemaphoreType.DMA((2,))],
        ),
        compiler_params=_params("arbitrary"),
        name="moe_combine",
    )(d0, d1, x, route, g, y_rows)


def _gate_kernel(x_ref, g_ref, wlr_ref, wa2_ref, b_ref, o_ref):
    h = _rms_bf16(x_ref[...], g_ref[...])
    a_lr = _dot(h, wlr_ref[...])
    z = _dot(a_lr.astype(BF16), wa2_ref[...]) + b_ref[...]
    o_ref[...] = (jnp.minimum(z, 0.0) - jnp.log(1.0 + jnp.exp(-jnp.abs(z)))) * (1.0 / GATE_TAU)


def gla_gate(x, g, w_lr_pad, w_a2_pad, b_a, *, tm=ROW_TILE):
    m, d = x.shape
    n = w_a2_pad.shape[1]
    return pl.pallas_call(
        _gate_kernel,
        out_shape=jax.ShapeDtypeStruct((m, n), F32),
        grid=(m // tm,),
        in_specs=[
            pl.BlockSpec((tm, d), lambda i: (i, 0)),
            pl.BlockSpec((1, d), lambda i: (0, 0)),
            pl.BlockSpec((d, LANES), lambda i: (0, 0)),
            pl.BlockSpec((LANES, n), lambda i: (0, 0)),
            pl.BlockSpec((1, n), lambda i: (0, 0)),
        ],
        out_specs=pl.BlockSpec((tm, n), lambda i: (i, 0)),
        compiler_params=_params("parallel"),
        name="gla_gate",
    )(x, g, w_lr_pad, w_a2_pad, b_a)


def _rotary(x, cos, sin):
    half = x.shape[-1] // 2
    x1, x2 = x[:, :half], x[:, half:]
    return jnp.concatenate([x1 * cos - x2 * sin, x1 * sin + x2 * cos], axis=-1)


def _retention_kernel(lg_ref, *refs, L, has_state):
    if has_state:
        q_ref, k_ref, v_ref, g_ref, cos_ref, sin_ref, s0_ref, o_ref, sout_ref, s_ref = refs
    else:
        q_ref, k_ref, v_ref, g_ref, cos_ref, sin_ref, o_ref, sout_ref, s_ref = refs
    c = pl.program_id(1)

    @pl.when(c == 0)
    def _():
        if has_state:
            s_ref[...] = s0_ref[0]
        else:
            s_ref[...] = jnp.zeros_like(s_ref)

    cos = cos_ref[...]
    sin = sin_ref[...]
    n_col = lax.broadcasted_iota(jnp.int32, (L, 1), 0).astype(F32)
    n_row = lax.broadcasted_iota(jnp.int32, (1, L), 1).astype(F32)
    diff = n_col - n_row
    for h in range(HA):
        lg = lg_ref[h]
        qs = slice(h * DA, (h + 1) * DA)
        vs = slice(h * DVA, (h + 1) * DVA)
        qr = _rotary(q_ref[:, qs].astype(F32), cos, sin)
        kr = _rotary(k_ref[:, qs].astype(F32), cos, sin) * (DA ** -0.5)
        vb = v_ref[:, vs].astype(BF16)
        decay = jnp.where(diff >= 0.0, jnp.exp(jnp.maximum(diff, 0.0) * lg), 0.0)
        qb = qr.astype(BF16)
        scores = _dot_nt(qb, kr.astype(BF16)) * decay
        inner = _dot(scores.astype(BF16), vb)
        state = s_ref[h]
        cross = _dot(qb, state.astype(BF16)) * jnp.exp((n_col + 1.0) * lg)
        k_dec = (kr * jnp.exp((L - 1.0 - n_col) * lg)).astype(BF16)
        s_ref[h] = state * jnp.exp(jnp.zeros((1, 1), F32) + L * lg) + _dot_tn(k_dec, vb)
        ret = inner + cross
        ret = ret - jnp.mean(ret, axis=-1, keepdims=True)
        ret = ret * lax.rsqrt(jnp.mean(ret * ret, axis=-1, keepdims=True) + EPS)
        o_ref[:, vs] = (_silu(g_ref[:, vs].astype(F32)) * ret).astype(o_ref.dtype)

    @pl.when(c == pl.num_programs(1) - 1)
    def _():
        sout_ref[0] = s_ref[...]


def retention(proj, cos, sin, log_gamma, *, nbatch, seq, row0, pos0, L, state0=None):
    nc = seq // L
    rb0 = row0 // L
    pb0 = pos0 // L
    has_state = state0 is not None
    wq = HA * DA
    assert HA * DVA == wq

    def col(k):
        return lambda b, c, lg: (rb0 + b * nc + c, k)

    in_specs = [
        pl.BlockSpec((L, wq), col(0)),
        pl.BlockSpec((L, wq), col(1)),
        pl.BlockSpec((L, wq), col(2)),
        pl.BlockSpec((L, wq), col(3)),
        pl.BlockSpec((L, DA // 2), lambda b, c, lg: (pb0 + c, 0)),
        pl.BlockSpec((L, DA // 2), lambda b, c, lg: (pb0 + c, 0)),
    ]
    args = [proj, proj, proj, proj, cos, sin]
    if has_state:
        in_specs.append(pl.BlockSpec((1, HA, DA, DVA), lambda b, c, lg: (b, 0, 0, 0)))
        args.append(state0)
    return pl.pallas_call(
        functools.partial(_retention_kernel, L=L, has_state=has_state),
        out_shape=(jax.ShapeDtypeStruct((nbatch * seq, HA * DVA), BF16),
                   jax.ShapeDtypeStruct((nbatch, HA, DA, DVA), F32)),
        grid_spec=pltpu.PrefetchScalarGridSpec(
            num_scalar_prefetch=1,
            grid=(nbatch, nc),
            in_specs=in_specs,
            out_specs=(pl.BlockSpec((L, wq), lambda b, c, lg: (b * nc + c, 0)),
                       pl.BlockSpec((1, HA, DA, DVA), lambda b, c, lg: (b, 0, 0, 0))),
            scratch_shapes=[pltpu.VMEM((HA, DA, DVA), F32)],
        ),
        compiler_params=_params("parallel", "arbitrary"),
        name="retention",
    )(log_gamma, *args)


def _lambda_value(l_ref, lam_init):
    lv = l_ref[...]
    a = jnp.sum(lv[0:1] * lv[1:2], axis=-1, keepdims=True)
    b = jnp.sum(lv[2:3] * lv[3:4], axis=-1, keepdims=True)
    return jnp.exp(a) - jnp.exp(b) + lam_init


def _head_norm_scale(o, scale):
    return o * lax.rsqrt(jnp.mean(o * o, axis=-1, keepdims=True) + EPS) * scale


def _diff_softmax_pv(q, key_parts, val_parts, masks, lam):
    w = None
    for c in range(2):
        qc = q[:, c * DB:(c + 1) * DB].astype(BF16)
        s = [_dot_nt(qc, k[:, c * DB:(c + 1) * DB]) for k in key_parts]
        s = [x if m is None else jnp.where(m, x, NEG_BIG) for x, m in zip(s, masks)]
        mx = functools.reduce(jnp.maximum, [jnp.max(x, axis=-1, keepdims=True) for x in s])
        p = [jnp.exp(x - mx) for x in s]
        inv = 1.0 / functools.reduce(lambda a, b: a + b, [jnp.sum(x, axis=-1, keepdims=True) for x in p])
        if c == 0:
            w = [x * inv for x in p]
        else:
            w = [a - lam * (x * inv) for a, x in zip(w, p)]
    outs = [_dot(a.astype(BF16), v) for a, v in zip(w, val_parts)]
    return functools.reduce(lambda a, b: a + b, outs)


def _dattn_prompt_kernel(q_ref, k_ref, v_ref, l_ref, o_ref, *, tq, nq, lam_init):
    i = pl.program_id(2)
    lam = _lambda_value(l_ref, lam_init)
    r_chunk = lax.broadcasted_iota(jnp.int32, (tq, tq), 0) // CHUNK
    c_chunk = lax.broadcasted_iota(jnp.int32, (tq, tq), 1) // CHUNK
    diag_mask = c_chunk <= r_chunk

    for n in range(nq):
        @pl.when(i == n)
        def _(n=n):
            q = q_ref[...].astype(F32) * (DB ** -0.5)
            lo = n * tq
            keys = [k_ref[lo:lo + tq, :]]
            vals = [v_ref[lo:lo + tq, :]]
            masks = [diag_mask]
            if n > 0:
                keys.insert(0, k_ref[0:lo, :])
                vals.insert(0, v_ref[0:lo, :])
                masks.insert(0, None)
            o = _diff_softmax_pv(q, keys, vals, masks, lam)
            o_ref[...] = _head_norm_scale(o, 1.0 - lam_init).astype(o_ref.dtype)


def diff_attention_prompt(proj, k_bf, v_bf, lam_params, *, nbatch, seq, lam_init, tq=ATTN_TILE):
    nq = seq // tq
    assert tq % CHUNK == 0
    return pl.pallas_call(
        functools.partial(_dattn_prompt_kernel, tq=tq, nq=nq, lam_init=lam_init),
        out_shape=jax.ShapeDtypeStruct((nbatch * seq, HB * DVB), BF16),
        grid=(nbatch, HB, nq),
        in_specs=[
            pl.BlockSpec((tq, 2 * DB), lambda b, h, i: (b * nq + i, 4 * HA + h)),
            pl.BlockSpec((seq, 2 * DB), lambda b, h, i: (b, h)),
            pl.BlockSpec((seq, DVB), lambda b, h, i: (b, h)),
            pl.BlockSpec((4, DB), lambda b, h, i: (0, 0)),
        ],
        out_specs=pl.BlockSpec((tq, DVB), lambda b, h, i: (b * nq + i, h)),
        compiler_params=_params("parallel", "parallel", "arbitrary"),
        name="diff_attention_prompt",
    )(proj, k_bf, v_bf, lam_params)


def _dattn_sample_kernel(q_ref, kn_ref, vn_ref, kc_ref, vc_ref, l_ref, o_ref, *, lam_init):
    q = q_ref[...].astype(F32) * (DB ** -0.5)
    lam = _lambda_value(l_ref, lam_init)
    keys = [kc_ref[0].astype(BF16), kn_ref[...]]
    vals = [vc_ref[0].astype(BF16), vn_ref[...]]
    o = _diff_softmax_pv(q, keys, vals, [None, None], lam)
    o_ref[...] = _head_norm_scale(o, 1.0 - lam_init).astype(o_ref.dtype)


def diff_attention_sample(proj, k_new, v_new, cache_k, cache_v, lam_params, *, nbatch, seq, row0, lam_init):
    past = cache_k.shape[1]
    assert seq == CHUNK and past % CHUNK == 0 and row0 % seq == 0
    rb0 = row0 // seq
    return pl.pallas_call(
        functools.partial(_dattn_sample_kernel, lam_init=lam_init),
        out_shape=jax.ShapeDtypeStruct((nbatch * seq, HB * DVB), BF16),
        grid=(nbatch, HB),
        in_specs=[
            pl.BlockSpec((seq, 2 * DB), lambda b, h: (rb0 + b, 4 * HA + h)),
            pl.BlockSpec((seq, 2 * DB), lambda b, h: (b, h)),
            pl.BlockSpec((seq, DVB), lambda b, h: (b, h)),
            pl.BlockSpec((1, past, 2 * DB), lambda b, h: (b, 0, h)),
            pl.BlockSpec((1, past, DVB), lambda b, h: (b, 0, h)),
            pl.BlockSpec((4, DB), lambda b, h: (0, 0)),
        ],
        out_specs=pl.BlockSpec((seq, DVB), lambda b, h: (b, h)),
        compiler_params=_params("parallel", "parallel"),
        name="diff_attention_sample",
    )(proj, k_new, v_new, cache_k, cache_v, lam_params)


def _split3_bf16(x):
    hi = x.astype(BF16)
    r1 = x - hi.astype(F32)
    mid = r1.astype(BF16)
    lo = (r1 - mid.astype(F32)).astype(BF16)
    return hi, mid, lo


def _gla_kernel(*refs, nsub, has_state):
    if has_state:
        q_ref, k_ref, v_ref, g_ref, a_ref, s0_ref, o_ref, sout_ref, s_ref = refs
    else:
        q_ref, k_ref, v_ref, g_ref, a_ref, o_ref, sout_ref, s_ref = refs
    c = pl.program_id(1)
    L = CHUNK

    @pl.when(c == 0)
    def _():
        for h in range(HC):
            if has_state:
                s_ref[h] = s0_ref[0, h].T
            else:
                s_ref[h] = jnp.zeros((DVC, DKC), F32)

    lb = nsub * L
    row = lax.broadcasted_iota(jnp.int32, (lb, lb), 0)
    colm = lax.broadcasted_iota(jnp.int32, (lb, lb), 1)
    causal = jnp.logical_and(row // L == colm // L, colm <= row)
    tril = jnp.where(causal, 1.0, 0.0).astype(BF16)

    hi, mid, lo = _split3_bf16(a_ref[...])
    b = _dot(tril, hi) + _dot(tril, mid) + _dot(tril, lo)

    def chunk_row(r):
        return jnp.concatenate(
            [jnp.broadcast_to(b[j * L + r:j * L + r + 1, :], (L, b.shape[1])) for j in range(nsub)], axis=0)

    ref = chunk_row(L // 2)
    b_last = chunk_row(L - 1)
    q = q_ref[...].astype(F32) * (DKC ** -0.5)
    k = k_ref[...].astype(F32)
    q_in = (q * jnp.exp(b - ref)).astype(BF16)
    k_in = (k * jnp.exp(ref - b)).astype(BF16)
    q_dec = (q * jnp.exp(b)).astype(BF16)
    k_out = (k * jnp.exp(b_last - b)).astype(BF16)
    vbs, inners = [], []
    for h in range(HC):
        ks = slice(h * DKC, (h + 1) * DKC)
        vb = v_ref[:, h * DVC:(h + 1) * DVC].astype(BF16)
        att = jnp.where(causal, _dot_nt(q_in[:, ks], k_in[:, ks]), 0.0)
        vbs.append(vb)
        inners.append(_dot(att.astype(BF16), vb))

    for j in range(nsub):
        rows = slice(j * L, (j + 1) * L)
        e_last = jnp.exp(b[j * L + L - 1:j * L + L, :])
        for h in range(HC):
            ks = slice(h * DKC, (h + 1) * DKC)
            vs = slice(h * DVC, (h + 1) * DVC)
            state_t = s_ref[h]
            cross = _dot_nt(q_dec[rows, ks], state_t.astype(BF16))
            s_ref[h] = e_last[:, ks] * state_t + _dot_tn(vbs[h][rows], k_out[rows, ks])
            o = inners[h][rows] + cross
            o = o * lax.rsqrt(jnp.mean(o * o, axis=-1, keepdims=True) + EPS)
            o_ref[rows, vs] = (_silu(g_ref[rows, vs].astype(F32)) * o).astype(o_ref.dtype)

    @pl.when(c == pl.num_programs(1) - 1)
    def _():
        for h in range(HC):
            sout_ref[0, h] = s_ref[h].T


def gla(proj, log_a, *, nbatch, seq, row0, rows_per_step, state0=None):
    lb = rows_per_step
    nc = seq // lb
    rb0 = row0 // lb
    has_state = state0 is not None
    wk = HC * DKC
    wv = HC * DVC
    assert wv == 2 * wk

    def rows(b, c):
        return rb0 + b * nc + c

    in_specs = [
        pl.BlockSpec((lb, wk), lambda b, c: (rows(b, c), 0)),
        pl.BlockSpec((lb, wk), lambda b, c: (rows(b, c), 1)),
        pl.BlockSpec((lb, wv), lambda b, c: (rows(b, c), 1)),
        pl.BlockSpec((lb, wv), lambda b, c: (rows(b, c), 2)),
        pl.BlockSpec((lb, wk), lambda b, c: (rows(b, c), 0)),
    ]
    args = [proj, proj, proj, proj, log_a]
    if has_state:
        in_specs.append(pl.BlockSpec((1, HC, DKC, DVC), lambda b, c: (b, 0, 0, 0)))
        args.append(state0)
    return pl.pallas_call(
        functools.partial(_gla_kernel, nsub=lb // CHUNK, has_state=has_state),
        out_shape=(jax.ShapeDtypeStruct((nbatch * seq, wv), BF16),
                   jax.ShapeDtypeStruct((nbatch, HC, DKC, DVC), F32)),
        grid=(nbatch, nc),
        in_specs=in_specs,
        out_specs=(pl.BlockSpec((lb, wv), lambda b, c: (b * nc + c, 0)),
                   pl.BlockSpec((1, HC, DKC, DVC), lambda b, c: (b, 0, 0, 0))),
        scratch_shapes=[pltpu.VMEM((HC, DVC, DKC), F32)],
        compiler_params=_params("parallel", "arbitrary"),
        name="gla",
    )(*args)


def _xattn_kernel(q_ref, k_ref, v_ref, o_ref):
    for h in range(HX):
        sl = slice(h * DX, (h + 1) * DX)
        q = q_ref[:, sl]
        k = k_ref[0, :, sl].astype(BF16)
        v = v_ref[0, :, sl].astype(BF16)
        s = _dot_nt(q, k) * (DX ** -0.5)
        m = jnp.max(s, axis=-1, keepdims=True)
        p = jnp.exp(s - m)
        p = p / jnp.sum(p, axis=-1, keepdims=True)
        o_ref[:, sl] = _dot(p.astype(BF16), v).astype(o_ref.dtype)


def cross_attention(q, mem_k, mem_v, *, nbatch, seq, row0, tq):
    d = q.shape[1]
    nq = seq // tq
    rb0 = row0 // tq
    return pl.pallas_call(
        _xattn_kernel,
        out_shape=jax.ShapeDtypeStruct((nbatch * seq, d), BF16),
        grid=(nbatch, nq),
        in_specs=[
            pl.BlockSpec((tq, d), lambda b, i: (rb0 + b * nq + i, 0)),
            pl.BlockSpec((1, N_MEM, d), lambda b, i: (b, 0, 0)),
            pl.BlockSpec((1, N_MEM, d), lambda b, i: (b, 0, 0)),
        ],
        out_specs=pl.BlockSpec((tq, d), lambda b, i: (b * nq + i, 0)),
        compiler_params=_params("parallel", "arbitrary"),
        name="cross_attention",
    )(q, mem_k, mem_v)


def _routing_tables(route, tm):
    n_tok = route.shape[0]
    n_rows = n_tok * TOP_K
    nb = (n_rows + N_EXPERTS * (tm - 1) + tm - 1) // tm
    flat_e = route[:, :TOP_K].astype(jnp.int32).reshape(-1)
    onehot = (flat_e[:, None] == jnp.arange(N_EXPERTS, dtype=jnp.int32)[None, :]).astype(jnp.int32)
    csum = jnp.cumsum(onehot, axis=0)
    rank = jnp.sum(csum * onehot, axis=1) - 1
    counts = csum[-1]
    padded = (counts + tm - 1) // tm * tm
    pad_end = jnp.cumsum(padded)
    pad_start = pad_end - padded
    dest = jnp.sum(onehot * pad_start[None, :], axis=1) + rank
    n_valid = (pad_end[-1] // tm).astype(jnp.int32)
    blk = jnp.minimum(jnp.arange(nb, dtype=jnp.int32), n_valid - 1) * tm
    block_expert = jnp.minimum(jnp.searchsorted(pad_end, blk, side='right'), N_EXPERTS - 1).astype(jnp.int32)
    dest2 = dest.reshape(n_tok, TOP_K).astype(jnp.int32)
    pad_lo = (pad_start + counts).astype(jnp.int32)
    pad_hi = pad_end.astype(jnp.int32).at[N_EXPERTS - 1].set(nb * tm)
    blk_all = jnp.arange(nb, dtype=jnp.int32) * tm
    rows_used = jnp.clip(pad_lo[block_expert] - blk_all, 0, tm).astype(jnp.int32)
    block_info = jnp.concatenate([n_valid.reshape(1), rows_used])
    return block_expert, block_info, dest2[:, 0], dest2[:, 1], pad_lo, pad_hi, nb * tm


def kernel(x_prompt, x_sample, mem_prompt, cache_diff_k, cache_diff_v, state_ret, state_gla, cache_mem_k, cache_mem_v, g_mix, g_xattn, g_mem, g_ffn, g_final, w_in_even, w_out_even, lambda_q1, lambda_k1, lambda_q2, lambda_k2, w_in_odd, w_gate_lr, b_gate_lr, w_out_odd, w_xq, w_xkv, w_xo, w_ffn_gu, w_ffn_dn, w_router, w_moe_gu, w_moe_dn):
    d = D_MODEL
    bp, tp, _ = x_prompt.shape
    bs, ts, _ = x_sample.shape
    past = cache_diff_k.shape[2]
    np_tok = bp * tp
    ns_tok = bs * ts
    n_tok = np_tok + ns_tok
    depth = g_mix.shape[0]

    x = (x_prompt.reshape(np_tok, d), x_sample.reshape(ns_tok, d))

    half = DA // 2
    inv = 1.0 / (ROPE_BASE ** jnp.linspace(0.0, 1.0, half, dtype=F32))
    pos = jnp.arange(max(tp, past + ts), dtype=jnp.int32).astype(F32)
    ang = pos[:, None] * inv[None, :]
    cos_t, sin_t = jnp.cos(ang), jnp.sin(ang)
    log_gamma = jnp.log1p(-jnp.power(2.0, -5.0 - jnp.arange(HA, dtype=F32)))


    ret_p, ret_s, gla_p, gla_s = [], [], [], []
    dk_p, dv_p, dk_s, dv_s = [], [], [], []
    mk_p, mv_p = [], []
    y_rows = route = d0 = d1 = None

    for i in range(depth):
        j = i // 2
        g_i = g_mix[i].reshape(1, d)
        if i % 2 == 0:
            w_in = w_in_even[j].astype(BF16)
            c_dk = 4 * HA * DA + HB * 2 * DB
            w_kv = HB * 2 * DB
            assert HB * DVB == w_kv
            if isinstance(x, tuple):
                (x_p, x_s), off_s = x, 0
                proj_p = norm_matmul(x_p, g_i, w_in, col0=0, ncols=c_dk, tn=1024, out_dtype=PROJ_DTYPE,
                                     tm=PROMPT_PROJ_ROW_TILE)
                proj_s = norm_matmul(x_s, g_i, w_in, col0=0, ncols=c_dk, tn=1024, out_dtype=PROJ_DTYPE)
            else:
                x_p, x_s, off_s = x, x, np_tok
                proj_p = proj_s = norm_matmul(x, g_i, w_in, col0=0, ncols=c_dk, tn=1024, out_dtype=PROJ_DTYPE,
                                              tm=PROJ_ROW_TILE)
            dk_f_p, dk_b_p, dv_f_p, dv_b_p = norm_matmul_groups(
                x_p, g_i, w_in, row0=0, nrows=np_tok, col0=c_dk, group_cols=w_kv, n_groups=2,
                out_dtypes=(F32, BF16))
            dk_f_s, dk_b_s, dv_f_s, dv_b_s = norm_matmul_groups(
                x_s, g_i, w_in, row0=off_s, nrows=ns_tok, col0=c_dk, group_cols=w_kv, n_groups=2,
                out_dtypes=(F32, BF16))
            lam_init = 0.8 - 0.6 * math.exp(-0.3 * i)
            lam_params = jnp.stack([lambda_q1[j], lambda_k1[j], lambda_q2[j], lambda_k2[j]]).astype(F32)
            a_p, s_p = retention(proj_p, cos_t, sin_t, log_gamma, nbatch=bp, seq=tp, row0=0, pos0=0, L=RET_CHUNK)
            a_s, s_s = retention(proj_s, cos_t, sin_t, log_gamma, nbatch=bs, seq=ts, row0=off_s, pos0=past,
                                 L=CHUNK, state0=state_ret[j])
            ret_p.append(s_p)
            ret_s.append(s_s)
            b_p = diff_attention_prompt(proj_p, dk_b_p, dv_b_p, lam_params, nbatch=bp, seq=tp, lam_init=lam_init)
            b_s = diff_attention_sample(
                proj_s, dk_b_s, dv_b_s, cache_diff_k[j].reshape(bs, past, HB * 2 * DB),
                cache_diff_v[j].reshape(bs, past, HB * DVB), lam_params, nbatch=bs, seq=ts, row0=off_s,
                lam_init=lam_init)
            dk_p.append(dk_f_p.reshape(bp, tp, HB, 2, DB))
            dk_s.append(dk_f_s.reshape(bs, ts, HB, 2, DB))
            dv_p.append(dv_f_p.reshape(bp, tp, HB, DVB))
            dv_s.append(dv_f_s.reshape(bs, ts, HB, DVB))
            x = matmul_residual([(a_p, a_s), (b_p, b_s)], w_out_even[j].astype(BF16), x)
        else:
            n_main = 2 * HC * DKC + 2 * HC * DVC
            w_in = w_in_odd[j]
            proj = norm_matmul(x, g_i, w_in[:, :n_main].astype(BF16), col0=0, ncols=n_main, tn=1024, out_dtype=PROJ_DTYPE,
                               tm=PROJ_ROW_TILE)
            w_lr = jnp.zeros((d, LANES), BF16).at[:, :GATE_RANK].set(w_in[:, n_main:].astype(BF16))
            w_a2 = jnp.zeros((LANES, HC * DKC), BF16).at[:GATE_RANK].set(w_gate_lr[j].astype(BF16))
            log_a = gla_gate(x, g_i, w_lr, w_a2, b_gate_lr[j].reshape(1, -1).astype(F32))
            o_p, s_p = gla(proj, log_a, nbatch=bp, seq=tp, row0=0, rows_per_step=GLA_BLOCK)
            o_s, s_s = gla(proj, log_a, nbatch=bs, seq=ts, row0=np_tok, rows_per_step=ts, state0=state_gla[j])
            gla_p.append(s_p)
            gla_s.append(s_s)
            x = matmul_residual([(o_p, o_s)], w_out_odd[j].astype(BF16), x)

        (q,) = norm_matmul_groups(x, g_xattn[i].reshape(1, d), w_xq[i].astype(BF16), row0=0, nrows=n_tok, col0=0,
                                  group_cols=HX * DX, n_groups=1, out_dtypes=(BF16,))
        mem2d = mem_prompt.reshape(bp * N_MEM, d)
        mk, mv = norm_matmul_groups(mem2d, g_mem[i].reshape(1, d), w_xkv[i].astype(BF16), row0=0,
                                    nrows=bp * N_MEM, col0=0, group_cols=HX * DX, n_groups=2, out_dtypes=(F32,),
                                    tm=MEM_ROW_TILE)
        mk_p.append(mk.reshape(bp, N_MEM, HX, DX))
        mv_p.append(mv.reshape(bp, N_MEM, HX, DX))
        o_p = cross_attention(q, mk.reshape(bp, N_MEM, HX * DX), mv.reshape(bp, N_MEM, HX * DX),
                              nbatch=bp, seq=tp, row0=0, tq=512)
        o_s = cross_attention(q, cache_mem_k[i].reshape(bs, N_MEM, HX * DX),
                              cache_mem_v[i].reshape(bs, N_MEM, HX * DX), nbatch=bs, seq=ts, row0=np_tok, tq=ts)
        x = matmul_residual([(o_p, o_s)], w_xo[i].astype(BF16), x)

        g_f = g_ffn[i].reshape(1, d)
        if i % 2 == 0:
            dense_blocks = n_tok // DENSE_FFN_ROW_TILE
            x = swiglu_blocks(x, g_f, w_ffn_gu[j][None], w_ffn_dn[j][None],
                              jnp.zeros((dense_blocks,), jnp.int32),
                              jnp.full((1 + dense_blocks,), dense_blocks, jnp.int32),
                              residual=True, tm=DENSE_FFN_ROW_TILE)
        else:
            w_r = jnp.zeros((d, LANES), BF16).at[:, :N_EXPERTS].set(w_router[j].astype(BF16))
            route = router(x, g_f, w_r)
            block_expert, n_valid, d0, d1, pad_lo, pad_hi, n_rows = _routing_tables(route, MOE_ROW_TILE)
            xs = moe_dispatch(x, g_f, d0, d1, pad_lo, pad_hi, n_rows)
            y_rows = swiglu_blocks(xs, g_f, w_moe_gu[j], w_moe_dn[j], block_expert, n_valid, residual=False,
                                   tm=MOE_ROW_TILE)
            if i != depth - 1:
                raise NotImplementedError("MoE layer must be the last layer")

    g_fin = g_final.reshape(1, d)
    y_p = moe_combine_norm(x, route, g_fin, y_rows, d0, d1, row0=0, nrows=np_tok)
    y_s = moe_combine_norm(x, route, g_fin, y_rows, d0, d1, row0=np_tok, nrows=ns_tok)

    return (y_p.reshape(bp, tp, d), y_s.reshape(bs, ts, d),
            jnp.stack(dk_p), jnp.stack(dv_p), jnp.stack(ret_p), jnp.stack(gla_p),
            jnp.stack(mk_p), jnp.stack(mv_p),
            jnp.stack(dk_s), jnp.stack(dv_s), jnp.stack(ret_s), jnp.stack(gla_s))
```

```python
import functools
import math

import jax
import jax.numpy as jnp
import numpy as np
from jax import lax
from jax.experimental import pallas as pl
from jax.experimental.pallas import tpu as pltpu

F32 = jnp.float32
BF16 = jnp.bfloat16
PROJ_DTYPE = BF16

D_MODEL = 2048
CHUNK = 64
N_MEM = 256
EPS = 1e-6
HA, DA, DVA = 4, 256, 256
ROPE_BASE = 10000.0
HB, DB, DVB = 4, 128, 256
HC, DKC, DVC = 4, 256, 512
GATE_RANK = 16
GATE_TAU = 16.0
HX, DX = 4, 512
D_FF = 5632
N_EXPERTS = 8
TOP_K = 2
NEG_BIG = -1e30

VMEM_LIMIT_BYTES = 56 * 1024 * 1024
LANES = 128
SUBLANES = 8

ROW_TILE = 512
PROJ_ROW_TILE = 1408
PROMPT_PROJ_ROW_TILE = 1024
MEM_ROW_TILE = 256
FF_TILE = 256
DENSE_FFN_ROW_TILE = 1056
PARTIAL_BLOCK_PARTS = 4
MOE_ROW_TILE = 1024
COMBINE_TILE = 256
DMA_ISSUE_UNROLL = 8
ZERO_FILL_ROWS = 64
ATTN_TILE = 256
RET_CHUNK = 256
GLA_BLOCK = 256


def _params(*sem):
    return pltpu.CompilerParams(dimension_semantics=sem, vmem_limit_bytes=VMEM_LIMIT_BYTES)


def _resident_spec(block_shape, index_map):
    return pl.BlockSpec(block_shape, index_map, pipeline_mode=pl.Buffered(1))


def _rms_bf16(x, g):
    ms = jnp.mean(x * x, axis=-1, keepdims=True)
    return (x * lax.rsqrt(ms + EPS) * g).astype(BF16)


def _silu(x):
    return x / (1.0 + jnp.exp(-x))


def _dot(a, b):
    return jnp.dot(a, b, preferred_element_type=F32)


def _dot_nt(a, b):
    return lax.dot_general(a, b, (((1,), (1,)), ((), ())), preferred_element_type=F32)


def _dot_tn(a, b):
    return lax.dot_general(a, b, (((0,), (0,)), ((), ())), preferred_element_type=F32)


def _norm_mm_kernel(x_ref, g_ref, w_ref, o_ref, h_ref):
    @pl.when(pl.program_id(1) == 0)
    def _():
        h_ref[...] = _rms_bf16(x_ref[...], g_ref[...])

    o_ref[...] = _dot(h_ref[...], w_ref[...]).astype(o_ref.dtype)


def norm_matmul(x, g, w, *, col0, ncols, tn, out_dtype, tm=ROW_TILE):
    m, d = x.shape
    assert m % tm == 0 and ncols % tn == 0 and col0 % tn == 0
    cb0 = col0 // tn
    return pl.pallas_call(
        _norm_mm_kernel,
        out_shape=jax.ShapeDtypeStruct((m, ncols), out_dtype),
        grid=(m // tm, ncols // tn),
        in_specs=[
            pl.BlockSpec((tm, d), lambda i, j: (i, 0)),
            pl.BlockSpec((1, d), lambda i, j: (0, 0)),
            pl.BlockSpec((d, tn), lambda i, j: (0, cb0 + j)),
        ],
        out_specs=pl.BlockSpec((tm, tn), lambda i, j: (i, j)),
        scratch_shapes=[pltpu.VMEM((tm, d), BF16)],
        compiler_params=_params("parallel", "arbitrary"),
        name="norm_matmul",
    )(x, g, w)


def _norm_mm_groups_kernel(*refs, n_groups):
    x_ref, g_ref = refs[:2]
    w_refs = refs[2:2 + n_groups]
    o_refs = refs[2 + n_groups:]
    h = _rms_bf16(x_ref[...], g_ref[...])
    outs_per_group = len(o_refs) // n_groups
    for k, w_ref in enumerate(w_refs):
        y = _dot(h, w_ref[...])
        for o_ref in o_refs[k * outs_per_group:(k + 1) * outs_per_group]:
            o_ref[...] = y.astype(o_ref.dtype)


def norm_matmul_groups(x, g, w, *, row0, nrows, col0, group_cols, n_groups, out_dtypes, tm=ROW_TILE):
    d = x.shape[1]
    assert nrows % tm == 0 and row0 % tm == 0 and col0 % group_cols == 0
    rb0 = row0 // tm
    cb0 = col0 // group_cols
    in_specs = [pl.BlockSpec((tm, d), lambda i: (rb0 + i, 0)), pl.BlockSpec((1, d), lambda i: (0, 0))]
    for k in range(n_groups):
        in_specs.append(_resident_spec((d, group_cols), lambda i, k=k: (0, cb0 + k)))
    out_shape, out_specs = [], []
    for k in range(n_groups):
        for dt in out_dtypes:
            out_shape.append(jax.ShapeDtypeStruct((nrows, group_cols), dt))
            out_specs.append(pl.BlockSpec((tm, group_cols), lambda i: (i, 0)))
    return pl.pallas_call(
        functools.partial(_norm_mm_groups_kernel, n_groups=n_groups),
        out_shape=out_shape,
        grid=(nrows // tm,),
        in_specs=in_specs,
        out_specs=out_specs,
        compiler_params=_params("parallel"),
        name="norm_matmul_groups",
    )(x, g, *([w] * n_groups))


def _mm_res_kernel(*refs, n_lhs, n_head_blocks):
    head_refs = refs[:n_lhs]
    tail_refs = refs[n_lhs:2 * n_lhs]
    w_refs = refs[2 * n_lhs:3 * n_lhs]
    r_refs = refs[3 * n_lhs:-1]
    o_ref = refs[-1]

    def compute(a_refs, r_ref):
        acc = r_ref[...]
        for a_ref, w_ref in zip(a_refs, w_refs):
            acc = acc + _dot(a_ref[...], w_ref[...])
        o_ref[...] = acc

    is_head = pl.program_id(0) < n_head_blocks

    @pl.when(is_head)
    def _():
        compute(head_refs, r_refs[0])

    @pl.when(jnp.logical_not(is_head))
    def _():
        compute(tail_refs, r_refs[-1])


def matmul_residual(lhs_pairs, w, res, *, tm=ROW_TILE):
    res_parts = tuple(res) if isinstance(res, (tuple, list)) else (res,)
    m = sum(r.shape[0] for r in res_parts)
    n = res_parts[0].shape[1]
    n_lhs = len(lhs_pairs)
    nhb = lhs_pairs[0][0].shape[0] // tm
    in_specs = []
    for head, tail in lhs_pairs:
        assert head.shape[0] == nhb * tm and tail.shape[0] == tm and (nhb + 1) * tm == m
        in_specs.append(pl.BlockSpec((tm, head.shape[1]), lambda i: (jnp.minimum(i, nhb - 1), 0)))
    for head, tail in lhs_pairs:
        in_specs.append(pl.BlockSpec((tm, tail.shape[1]), lambda i: (0, 0)))
    row = 0
    for head, _ in lhs_pairs:
        kk = head.shape[1]
        assert row % kk == 0
        rb = row // kk
        in_specs.append(_resident_spec((kk, n), lambda i, rb=rb: (rb, 0)))
        row += kk
    if len(res_parts) == 1:
        in_specs.append(pl.BlockSpec((tm, n), lambda i: (i, 0)))
    else:
        assert res_parts[0].shape[0] == nhb * tm and res_parts[1].shape[0] == tm
        in_specs.append(pl.BlockSpec((tm, n), lambda i: (jnp.minimum(i, nhb - 1), 0)))
        in_specs.append(pl.BlockSpec((tm, n), lambda i: (0, 0)))
    heads = [p[0] for p in lhs_pairs]
    tails = [p[1] for p in lhs_pairs]
    return pl.pallas_call(
        functools.partial(_mm_res_kernel, n_lhs=n_lhs, n_head_blocks=nhb),
        out_shape=jax.ShapeDtypeStruct((m, n), F32),
        grid=(m // tm,),
        in_specs=in_specs,
        out_specs=pl.BlockSpec((tm, n), lambda i: (i, 0)),
        compiler_params=_params("parallel"),
        name="matmul_residual",
    )(*heads, *tails, *([w] * n_lhs), *res_parts)


def _ffn_kernel(be_ref, nv_ref, x_ref, g_ref, wg_ref, wu_ref, wd_ref, o_ref, h_ref, *, residual, packed,
                half_blocks):
    i = pl.program_id(0)
    f = pl.program_id(1)
    valid = i < nv_ref[0]

    @pl.when(f == 0)
    def _():
        if packed:
            h_ref[...] = _unpack_bf16_pairs(x_ref[...])
            o_ref[...] = jnp.zeros_like(o_ref)
        else:
            x = x_ref[...]
            h_ref[...] = _rms_bf16(x, g_ref[...])
            o_ref[...] = x if residual else jnp.zeros_like(x)

    def accumulate(n_rows):
        h = h_ref[0:n_rows, :]
        a = _dot(h, wg_ref[0].astype(BF16))
        u = _dot(h, wu_ref[0].astype(BF16))
        act = (_silu(a) * u).astype(BF16)
        o_ref[0:n_rows, :] += _dot(act, wd_ref[0].astype(BF16))

    tm = h_ref.shape[0]
    if half_blocks:
        quarter = tm // PARTIAL_BLOCK_PARTS
        parts_used = (nv_ref[1 + i] + quarter - 1) // quarter
        for parts in range(1, PARTIAL_BLOCK_PARTS + 1):
            @pl.when(jnp.logical_and(valid, parts_used == parts))
            def _(parts=parts):
                accumulate(parts * quarter)
    else:
        @pl.when(valid)
        def _():
            accumulate(tm)


def swiglu_blocks(x, g, w_gu, w_dn, block_expert, n_valid, *, residual, tm=ROW_TILE, tf=FF_TILE):
    m = x.shape[0]
    d = w_gu.shape[1]
    packed = x.dtype == jnp.uint32
    assert x.shape[1] == (d // 2 if packed else d) and not (packed and residual)
    ff = w_dn.shape[1]
    nf = ff // tf
    nb = m // tm

    def _x_map(i, f, be, nv):
        return (jnp.minimum(i, nv[0] - 1), 0)

    def _f_eff(i, f, nv):
        return jnp.where(i < nv[0], f, nf - 1)

    return pl.pallas_call(
        functools.partial(_ffn_kernel, residual=residual, packed=packed, half_blocks=packed),
        out_shape=jax.ShapeDtypeStruct((m, d), F32),
        grid_spec=pltpu.PrefetchScalarGridSpec(
            num_scalar_prefetch=2,
            grid=(nb, nf),
            in_specs=[
                pl.BlockSpec((tm, x.shape[1]), _x_map),
                pl.BlockSpec((1, d), lambda i, f, be, nv: (0, 0)),
                pl.BlockSpec((1, d, tf), lambda i, f, be, nv: (be[i], 0, _f_eff(i, f, nv))),
                pl.BlockSpec((1, d, tf), lambda i, f, be, nv: (be[i], 0, nf + _f_eff(i, f, nv))),
                pl.BlockSpec((1, tf, d), lambda i, f, be, nv: (be[i], _f_eff(i, f, nv), 0)),
            ],
            out_specs=pl.BlockSpec((tm, d), lambda i, f, be, nv: (i, 0)),
            scratch_shapes=[pltpu.VMEM((tm, d), BF16)],
        ),
        compiler_params=_params("parallel", "arbitrary"),
        name="swiglu_blocks",
    )(block_expert, n_valid, x, g, w_gu, w_gu, w_dn)


def _router_kernel(x_ref, g_ref, w_ref, o_ref):
    h = _rms_bf16(x_ref[...], g_ref[...])
    logits = _dot(h, w_ref[...])
    lane = lax.broadcasted_iota(jnp.int32, logits.shape, 1).astype(F32)
    l1 = jnp.where(lane < N_EXPERTS, logits, NEG_BIG)
    m1 = jnp.max(l1, axis=-1, keepdims=True)
    i1 = jnp.min(jnp.where(l1 == m1, lane, float(LANES)), axis=-1, keepdims=True)
    l2 = jnp.where(lane == i1, NEG_BIG, l1)
    m2 = jnp.max(l2, axis=-1, keepdims=True)
    i2 = jnp.min(jnp.where(l2 == m2, lane, float(LANES)), axis=-1, keepdims=True)
    e = jnp.exp(m2 - m1)
    g1 = 1.0 / (1.0 + e)
    g2 = e / (1.0 + e)
    out = jnp.where(lane == 0.0, i1,
                    jnp.where(lane == 1.0, i2,
                              jnp.where(lane == 2.0, g1, jnp.where(lane == 3.0, g2, 0.0))))
    o_ref[...] = out


def router(x, g, w_pad, *, tm=ROW_TILE):
    m, d = x.shape
    return pl.pallas_call(
        _router_kernel,
        out_shape=jax.ShapeDtypeStruct((m, LANES), F32),
        grid=(m // tm,),
        in_specs=[
            pl.BlockSpec((tm, d), lambda i: (i, 0)),
            pl.BlockSpec((1, d), lambda i: (0, 0)),
            pl.BlockSpec((d, LANES), lambda i: (0, 0)),
        ],
        out_specs=pl.BlockSpec((tm, LANES), lambda i: (i, 0)),
        compiler_params=_params("parallel"),
        name="router",
    )(x, g, w_pad)


def _row_copy(src, dst, src_row, dst_row, sem):
    return pltpu.make_async_copy(src.at[pl.ds(src_row, 1)], dst.at[pl.ds(dst_row, 1)], sem)


def _pack_bf16_pairs(h):
    half = h.shape[-1] // 2
    lo = lax.bitcast_convert_type(h[:, :half].astype(F32), jnp.uint32) >> 16
    hi = lax.bitcast_convert_type(h[:, half:].astype(F32), jnp.uint32) & jnp.uint32(0xFFFF0000)
    return lo | hi


def _unpack_bf16_pairs(u):
    lo = lax.bitcast_convert_type(u << 16, F32)
    hi = lax.bitcast_convert_type(u & jnp.uint32(0xFFFF0000), F32)
    return jnp.concatenate([lo, hi], axis=-1).astype(BF16)


def _dispatch_kernel(d0_ref, d1_ref, plo_ref, phi_ref, x_ref, g_ref, xs_hbm, hbuf, zrow, sem, zsem, *, tm):
    i = pl.program_id(0)
    last = pl.num_programs(0) - 1
    slot = i % 2
    base = i * tm

    zr = zrow.shape[0]

    def zero_copy(r, n):
        return pltpu.make_async_copy(zrow.at[pl.ds(0, n)], xs_hbm.at[pl.ds(r, n)], zsem)

    def for_each_unrouted_span(fn):
        for e in range(N_EXPERTS):
            lo = plo_ref[e]
            hi = phi_ref[e]
            lo_al = jnp.minimum((lo + SUBLANES - 1) // SUBLANES * SUBLANES, hi)
            n_big = (hi - lo_al) // zr

            def big(k, c, lo_al=lo_al):
                fn(pl.multiple_of(lo_al + k * zr, SUBLANES), zr)
                return c

            def small(r, c):
                fn(r, 1)
                return c

            lax.fori_loop(lo, lo_al, small, 0)
            lax.fori_loop(0, n_big, big, 0)
            lax.fori_loop(lo_al + n_big * zr, hi, small, 0)

    @pl.when(i == 0)
    def _():
        zrow[...] = jnp.zeros_like(zrow)
        for_each_unrouted_span(lambda r, n: zero_copy(r, n).start())

    hbuf[slot] = _pack_bf16_pairs(_rms_bf16(x_ref[...], g_ref[...]))

    def start(r, c):
        _row_copy(hbuf.at[slot], xs_hbm, r, d0_ref[base + r], sem.at[slot]).start(priority=0)
        _row_copy(hbuf.at[slot], xs_hbm, r, d1_ref[base + r], sem.at[slot]).start(priority=1)
        return c

    lax.fori_loop(0, tm, start, 0, unroll=DMA_ISSUE_UNROLL)

    def wait_slot(s):
        for _ in range(TOP_K):
            pltpu.make_async_copy(hbuf.at[s], xs_hbm.at[pl.ds(0, tm)], sem.at[s]).wait()

    @pl.when(i >= 1)
    def _():
        wait_slot(1 - slot)

    @pl.when(i == last)
    def _():
        wait_slot(slot)
        for_each_unrouted_span(lambda r, n: zero_copy(r, n).wait())


def moe_dispatch(x, g, d0, d1, pad_lo, pad_hi, n_rows, *, tm=ROW_TILE):
    m, d = x.shape
    return pl.pallas_call(
        functools.partial(_dispatch_kernel, tm=tm),
        out_shape=jax.ShapeDtypeStruct((n_rows, d // 2), jnp.uint32),
        grid_spec=pltpu.PrefetchScalarGridSpec(
            num_scalar_prefetch=4,
            grid=(m // tm,),
            in_specs=[
                pl.BlockSpec((tm, d), lambda i, *_: (i, 0)),
                pl.BlockSpec((1, d), lambda i, *_: (0, 0)),
            ],
            out_specs=pl.BlockSpec(memory_space=pl.ANY),
            scratch_shapes=[pltpu.VMEM((2, tm, d // 2), jnp.uint32), pltpu.VMEM((ZERO_FILL_ROWS, d // 2), jnp.uint32),
                            pltpu.SemaphoreType.DMA((2,)), pltpu.SemaphoreType.DMA],
        ),
        compiler_params=_params("arbitrary"),
        name="moe_dispatch",
    )(d0, d1, pad_lo, pad_hi, x, g)


def _combine_kernel(d0_ref, d1_ref, x_ref, r_ref, g_ref, y_hbm, o_ref, buf, sem, *, tm, row0):
    i = pl.program_id(0)
    slot = i % 2

    def issue(block, slot_):
        base = row0 + block * tm

        def start(r, c):
            _row_copy(y_hbm, buf.at[slot_], d0_ref[base + r], r, sem.at[slot_]).start(priority=0)
            _row_copy(y_hbm, buf.at[slot_], d1_ref[base + r], tm + r, sem.at[slot_]).start(priority=1)
            return c

        lax.fori_loop(0, tm, start, 0, unroll=DMA_ISSUE_UNROLL)

    @pl.when(i == 0)
    def _():
        issue(0, 0)

    @pl.when(i + 1 < pl.num_programs(0))
    def _():
        issue(i + 1, 1 - slot)

    pltpu.make_async_copy(y_hbm.at[pl.ds(0, 2 * tm)], buf.at[slot], sem.at[slot]).wait()
    rt = r_ref[...]
    g0 = rt[:, 2:3]
    g1 = rt[:, 3:4]
    x = x_ref[...] + (buf[slot, 0:tm, :] * g0 + buf[slot, tm:2 * tm, :] * g1)
    ms = jnp.mean(x * x, axis=-1, keepdims=True)
    o_ref[...] = x * lax.rsqrt(ms + EPS) * g_ref[...]


def moe_combine_norm(x, route, g, y_rows, d0, d1, *, row0, nrows, tm=COMBINE_TILE):
    d = x.shape[1]
    rb0 = row0 // tm
    return pl.pallas_call(
        functools.partial(_combine_kernel, tm=tm, row0=row0),
        out_shape=jax.ShapeDtypeStruct((nrows, d), F32),
        grid_spec=pltpu.PrefetchScalarGridSpec(
            num_scalar_prefetch=2,
            grid=(nrows // tm,),
            in_specs=[
                pl.BlockSpec((tm, d), lambda i, a, b: (rb0 + i, 0)),
                pl.BlockSpec((tm, LANES), lambda i, a, b: (rb0 + i, 0)),
                pl.BlockSpec((1, d), lambda i, a, b: (0, 0)),
                pl.BlockSpec(memory_space=pl.ANY),
            ],
            out_specs=pl.BlockSpec((tm, d), lambda i, a, b: (i, 0)),
            scratch_shapes=[pltpu.VMEM((2, 2 * tm, d), F32), pltpu.SemaphoreType.DMA((2,))],
        ),
        compiler_params=_params("arbitrary"),
        name="moe_combine",
    )(d0, d1, x, route, g, y_rows)


def _gate_kernel(x_ref, g_ref, wlr_ref, wa2_ref, b_ref, o_ref):
    h = _rms_bf16(x_ref[...], g_ref[...])
    a_lr = _dot(h, wlr_ref[...])
    z = _dot(a_lr.astype(BF16), wa2_ref[...]) + b_ref[...]
    o_ref[...] = (jnp.minimum(z, 0.0) - jnp.log(1.0 + jnp.exp(-jnp.abs(z)))) * (1.0 / GATE_TAU)


def gla_gate(x, g, w_lr_pad, w_a2_pad, b_a, *, tm=ROW_TILE):
    m, d = x.shape
    n = w_a2_pad.shape[1]
    return pl.pallas_call(
        _gate_kernel,
        out_shape=jax.ShapeDtypeStruct((m, n), F32),
        grid=(m // tm,),
        in_specs=[
            pl.BlockSpec((tm, d), lambda i: (i, 0)),
            pl.BlockSpec((1, d), lambda i: (0, 0)),
            pl.BlockSpec((d, LANES), lambda i: (0, 0)),
            pl.BlockSpec((LANES, n), lambda i: (0, 0)),
            pl.BlockSpec((1, n), lambda i: (0, 0)),
        ],
        out_specs=pl.BlockSpec((tm, n), lambda i: (i, 0)),
        compiler_params=_params("parallel"),
        name="gla_gate",
    )(x, g, w_lr_pad, w_a2_pad, b_a)


def _rotary(x, cos, sin):
    half = x.shape[-1] // 2
    x1, x2 = x[:, :half], x[:, half:]
    return jnp.concatenate([x1 * cos - x2 * sin, x1 * sin + x2 * cos], axis=-1)


def _retention_kernel(lg_ref, *refs, L, has_state):
    if has_state:
        q_ref, k_ref, v_ref, g_ref, cos_ref, sin_ref, s0_ref, o_ref, sout_ref, s_ref = refs
    else:
        q_ref, k_ref, v_ref, g_ref, cos_ref, sin_ref, o_ref, sout_ref, s_ref = refs
    c = pl.program_id(1)

    @pl.when(c == 0)
    def _():
        if has_state:
            s_ref[...] = s0_ref[0]
        else:
            s_ref[...] = jnp.zeros_like(s_ref)

    cos = cos_ref[...]
    sin = sin_ref[...]
    n_col = lax.broadcasted_iota(jnp.int32, (L, 1), 0).astype(F32)
    n_row = lax.broadcasted_iota(jnp.int32, (1, L), 1).astype(F32)
    diff = n_col - n_row
    for h in range(HA):
        lg = lg_ref[h]
        qs = slice(h * DA, (h + 1) * DA)
        vs = slice(h * DVA, (h + 1) * DVA)
        qr = _rotary(q_ref[:, qs].astype(F32), cos, sin)
        kr = _rotary(k_ref[:, qs].astype(F32), cos, sin) * (DA ** -0.5)
        vb = v_ref[:, vs].astype(BF16)
        decay = jnp.where(diff >= 0.0, jnp.exp(jnp.maximum(diff, 0.0) * lg), 0.0)
        qb = qr.astype(BF16)
        scores = _dot_nt(qb, kr.astype(BF16)) * decay
        inner = _dot(scores.astype(BF16), vb)
        state = s_ref[h]
        cross = _dot(qb, state.astype(BF16)) * jnp.exp((n_col + 1.0) * lg)
        k_dec = (kr * jnp.exp((L - 1.0 - n_col) * lg)).astype(BF16)
        s_ref[h] = state * jnp.exp(jnp.zeros((1, 1), F32) + L * lg) + _dot_tn(k_dec, vb)
        ret = inner + cross
        ret = ret - jnp.mean(ret, axis=-1, keepdims=True)
        ret = ret * lax.rsqrt(jnp.mean(ret * ret, axis=-1, keepdims=True) + EPS)
        o_ref[:, vs] = (_silu(g_ref[:, vs].astype(F32)) * ret).astype(o_ref.dtype)

    @pl.when(c == pl.num_programs(1) - 1)
    def _():
        sout_ref[0] = s_ref[...]


def retention(proj, cos, sin, log_gamma, *, nbatch, seq, row0, pos0, L, state0=None):
    nc = seq // L
    rb0 = row0 // L
    pb0 = pos0 // L
    has_state = state0 is not None
    wq = HA * DA
    assert HA * DVA == wq

    def col(k):
        return lambda b, c, lg: (rb0 + b * nc + c, k)

    in_specs = [
        pl.BlockSpec((L, wq), col(0)),
        pl.BlockSpec((L, wq), col(1)),
        pl.BlockSpec((L, wq), col(2)),
        pl.BlockSpec((L, wq), col(3)),
        pl.BlockSpec((L, DA // 2), lambda b, c, lg: (pb0 + c, 0)),
        pl.BlockSpec((L, DA // 2), lambda b, c, lg: (pb0 + c, 0)),
    ]
    args = [proj, proj, proj, proj, cos, sin]
    if has_state:
        in_specs.append(pl.BlockSpec((1, HA, DA, DVA), lambda b, c, lg: (b, 0, 0, 0)))
        args.append(state0)
    return pl.pallas_call(
        functools.partial(_retention_kernel, L=L, has_state=has_state),
        out_shape=(jax.ShapeDtypeStruct((nbatch * seq, HA * DVA), BF16),
                   jax.ShapeDtypeStruct((nbatch, HA, DA, DVA), F32)),
        grid_spec=pltpu.PrefetchScalarGridSpec(
            num_scalar_prefetch=1,
            grid=(nbatch, nc),
            in_specs=in_specs,
            out_specs=(pl.BlockSpec((L, wq), lambda b, c, lg: (b * nc + c, 0)),
                       pl.BlockSpec((1, HA, DA, DVA), lambda b, c, lg: (b, 0, 0, 0))),
            scratch_shapes=[pltpu.VMEM((HA, DA, DVA), F32)],
        ),
        compiler_params=_params("parallel", "arbitrary"),
        name="retention",
    )(log_gamma, *args)


def _lambda_value(l_ref, lam_init):
    lv = l_ref[...]
    a = jnp.sum(lv[0:1] * lv[1:2], axis=-1, keepdims=True)
    b = jnp.sum(lv[2:3] * lv[3:4], axis=-1, keepdims=True)
    return jnp.exp(a) - jnp.exp(b) + lam_init


def _head_norm_scale(o, scale):
    return o * lax.rsqrt(jnp.mean(o * o, axis=-1, keepdims=True) + EPS) * scale


def _diff_softmax_pv(q, key_parts, val_parts, masks, lam):
    w = None
    for c in range(2):
        qc = q[:, c * DB:(c + 1) * DB].astype(BF16)
        s = [_dot_nt(qc, k[:, c * DB:(c + 1) * DB]) for k in key_parts]
        s = [x if m is None else jnp.where(m, x, NEG_BIG) for x, m in zip(s, masks)]
        mx = functools.reduce(jnp.maximum, [jnp.max(x, axis=-1, keepdims=True) for x in s])
        p = [jnp.exp(x - mx) for x in s]
        inv = 1.0 / functools.reduce(lambda a, b: a + b, [jnp.sum(x, axis=-1, keepdims=True) for x in p])
        if c == 0:
            w = [x * inv for x in p]
        else:
            w = [a - lam * (x * inv) for a, x in zip(w, p)]
    outs = [_dot(a.astype(BF16), v) for a, v in zip(w, val_parts)]
    return functools.reduce(lambda a, b: a + b, outs)


def _dattn_prompt_kernel(q_ref, k_ref, v_ref, l_ref, o_ref, *, tq, nq, lam_init):
    i = pl.program_id(2)
    lam = _lambda_value(l_ref, lam_init)
    r_chunk = lax.broadcasted_iota(jnp.int32, (tq, tq), 0) // CHUNK
    c_chunk = lax.broadcasted_iota(jnp.int32, (tq, tq), 1) // CHUNK
    diag_mask = c_chunk <= r_chunk

    for n in range(nq):
        @pl.when(i == n)
        def _(n=n):
            q = q_ref[...].astype(F32) * (DB ** -0.5)
            lo = n * tq
            keys = [k_ref[lo:lo + tq, :]]
            vals = [v_ref[lo:lo + tq, :]]
            masks = [diag_mask]
            if n > 0:
                keys.insert(0, k_ref[0:lo, :])
                vals.insert(0, v_ref[0:lo, :])
                masks.insert(0, None)
            o = _diff_softmax_pv(q, keys, vals, masks, lam)
            o_ref[...] = _head_norm_scale(o, 1.0 - lam_init).astype(o_ref.dtype)


def diff_attention_prompt(proj, k_bf, v_bf, lam_params, *, nbatch, seq, lam_init, tq=ATTN_TILE):
    nq = seq // tq
    assert tq % CHUNK == 0
    return pl.pallas_call(
        functools.partial(_dattn_prompt_kernel, tq=tq, nq=nq, lam_init=lam_init),
        out_shape=jax.ShapeDtypeStruct((nbatch * seq, HB * DVB), BF16),
        grid=(nbatch, HB, nq),
        in_specs=[
            pl.BlockSpec((tq, 2 * DB), lambda b, h, i: (b * nq + i, 4 * HA + h)),
            pl.BlockSpec((seq, 2 * DB), lambda b, h, i: (b, h)),
            pl.BlockSpec((seq, DVB), lambda b, h, i: (b, h)),
            pl.BlockSpec((4, DB), lambda b, h, i: (0, 0)),
        ],
        out_specs=pl.BlockSpec((tq, DVB), lambda b, h, i: (b * nq + i, h)),
        compiler_params=_params("parallel", "parallel", "arbitrary"),
        name="diff_attention_prompt",
    )(proj, k_bf, v_bf, lam_params)


def _dattn_sample_kernel(q_ref, kn_ref, vn_ref, kc_ref, vc_ref, l_ref, o_ref, *, lam_init):
    q = q_ref[...].astype(F32) * (DB ** -0.5)
    lam = _lambda_value(l_ref, lam_init)
    keys = [kc_ref[0].astype(BF16), kn_ref[...]]
    vals = [vc_ref[0].astype(BF16), vn_ref[...]]
    o = _diff_softmax_pv(q, keys, vals, [None, None], lam)
    o_ref[...] = _head_norm_scale(o, 1.0 - lam_init).astype(o_ref.dtype)


def diff_attention_sample(proj, k_new, v_new, cache_k, cache_v, lam_params, *, nbatch, seq, row0, lam_init):
    past = cache_k.shape[1]
    assert seq == CHUNK and past % CHUNK == 0 and row0 % seq == 0
    rb0 = row0 // seq
    return pl.pallas_call(
        functools.partial(_dattn_sample_kernel, lam_init=lam_init),
        out_shape=jax.ShapeDtypeStruct((nbatch * seq, HB * DVB), BF16),
        grid=(nbatch, HB),
        in_specs=[
            pl.BlockSpec((seq, 2 * DB), lambda b, h: (rb0 + b, 4 * HA + h)),
            pl.BlockSpec((seq, 2 * DB), lambda b, h: (b, h)),
            pl.BlockSpec((seq, DVB), lambda b, h: (b, h)),
            pl.BlockSpec((1, past, 2 * DB), lambda b, h: (b, 0, h)),
            pl.BlockSpec((1, past, DVB), lambda b, h: (b, 0, h)),
            pl.BlockSpec((4, DB), lambda b, h: (0, 0)),
        ],
        out_specs=pl.BlockSpec((seq, DVB), lambda b, h: (b, h)),
        compiler_params=_params("parallel", "parallel"),
        name="diff_attention_sample",
    )(proj, k_new, v_new, cache_k, cache_v, lam_params)


def _split3_bf16(x):
    hi = x.astype(BF16)
    r1 = x - hi.astype(F32)
    mid = r1.astype(BF16)
    lo = (r1 - mid.astype(F32)).astype(BF16)
    return hi, mid, lo


def _gla_kernel(*refs, nsub, has_state):
    if has_state:
        q_ref, k_ref, v_ref, g_ref, a_ref, s0_ref, o_ref, sout_ref, s_ref = refs
    else:
        q_ref, k_ref, v_ref, g_ref, a_ref, o_ref, sout_ref, s_ref = refs
    c = pl.program_id(1)
    L = CHUNK

    @pl.when(c == 0)
    def _():
        for h in range(HC):
            if has_state:
                s_ref[h] = s0_ref[0, h].T
            else:
                s_ref[h] = jnp.zeros((DVC, DKC), F32)

    lb = nsub * L
    row = lax.broadcasted_iota(jnp.int32, (lb, lb), 0)
    colm = lax.broadcasted_iota(jnp.int32, (lb, lb), 1)
    causal = jnp.logical_and(row // L == colm // L, colm <= row)
    tril = jnp.where(causal, 1.0, 0.0).astype(BF16)

    hi, mid, lo = _split3_bf16(a_ref[...])
    b = _dot(tril, hi) + _dot(tril, mid) + _dot(tril, lo)

    def chunk_row(r):
        return jnp.concatenate(
            [jnp.broadcast_to(b[j * L + r:j * L + r + 1, :], (L, b.shape[1])) for j in range(nsub)], axis=0)

    ref = chunk_row(L // 2)
    b_last = chunk_row(L - 1)
    q = q_ref[...].astype(F32) * (DKC ** -0.5)
    k = k_ref[...].astype(F32)
    q_in = (q * jnp.exp(b - ref)).astype(BF16)
    k_in = (k * jnp.exp(ref - b)).astype(BF16)
    q_dec = (q * jnp.exp(b)).astype(BF16)
    k_out = (k * jnp.exp(b_last - b)).astype(BF16)
    vbs, inners = [], []
    for h in range(HC):
        ks = slice(h * DKC, (h + 1) * DKC)
        vb = v_ref[:, h * DVC:(h + 1) * DVC].astype(BF16)
        att = jnp.where(causal, _dot_nt(q_in[:, ks], k_in[:, ks]), 0.0)
        vbs.append(vb)
        inners.append(_dot(att.astype(BF16), vb))

    for j in range(nsub):
        rows = slice(j * L, (j + 1) * L)
        e_last = jnp.exp(b[j * L + L - 1:j * L + L, :])
        for h in range(HC):
            ks = slice(h * DKC, (h + 1) * DKC)
            vs = slice(h * DVC, (h + 1) * DVC)
            state_t = s_ref[h]
            cross = _dot_nt(q_dec[rows, ks], state_t.astype(BF16))
            s_ref[h] = e_last[:, ks] * state_t + _dot_tn(vbs[h][rows], k_out[rows, ks])
            o = inners[h][rows] + cross
            o = o * lax.rsqrt(jnp.mean(o * o, axis=-1, keepdims=True) + EPS)
            o_ref[rows, vs] = (_silu(g_ref[rows, vs].astype(F32)) * o).astype(o_ref.dtype)

    @pl.when(c == pl.num_programs(1) - 1)
    def _():
        for h in range(HC):
            sout_ref[0, h] = s_ref[h].T


def gla(proj, log_a, *, nbatch, seq, row0, rows_per_step, state0=None):
    lb = rows_per_step
    nc = seq // lb
    rb0 = row0 // lb
    has_state = state0 is not None
    wk = HC * DKC
    wv = HC * DVC
    assert wv == 2 * wk

    def rows(b, c):
        return rb0 + b * nc + c

    in_specs = [
        pl.BlockSpec((lb, wk), lambda b, c: (rows(b, c), 0)),
        pl.BlockSpec((lb, wk), lambda b, c: (rows(b, c), 1)),
        pl.BlockSpec((lb, wv), lambda b, c: (rows(b, c), 1)),
        pl.BlockSpec((lb, wv), lambda b, c: (rows(b, c), 2)),
        pl.BlockSpec((lb, wk), lambda b, c: (rows(b, c), 0)),
    ]
    args = [proj, proj, proj, proj, log_a]
    if has_state:
        in_specs.append(pl.BlockSpec((1, HC, DKC, DVC), lambda b, c: (b, 0, 0, 0)))
        args.append(state0)
    return pl.pallas_call(
        functools.partial(_gla_kernel, nsub=lb // CHUNK, has_state=has_state),
        out_shape=(jax.ShapeDtypeStruct((nbatch * seq, wv), BF16),
                   jax.ShapeDtypeStruct((nbatch, HC, DKC, DVC), F32)),
        grid=(nbatch, nc),
        in_specs=in_specs,
        out_specs=(pl.BlockSpec((lb, wv), lambda b, c: (b * nc + c, 0)),
                   pl.BlockSpec((1, HC, DKC, DVC), lambda b, c: (b, 0, 0, 0))),
        scratch_shapes=[pltpu.VMEM((HC, DVC, DKC), F32)],
        compiler_params=_params("parallel", "arbitrary"),
        name="gla",
    )(*args)


def _xattn_kernel(q_ref, k_ref, v_ref, o_ref):
    for h in range(HX):
        sl = slice(h * DX, (h + 1) * DX)
        q = q_ref[:, sl]
        k = k_ref[0, :, sl].astype(BF16)
        v = v_ref[0, :, sl].astype(BF16)
        s = _dot_nt(q, k) * (DX ** -0.5)
        m = jnp.max(s, axis=-1, keepdims=True)
        p = jnp.exp(s - m)
        p = p / jnp.sum(p, axis=-1, keepdims=True)
        o_ref[:, sl] = _dot(p.astype(BF16), v).astype(o_ref.dtype)


def cross_attention(q, mem_k, mem_v, *, nbatch, seq, row0, tq):
    d = q.shape[1]
    nq = seq // tq
    rb0 = row0 // tq
    return pl.pallas_call(
        _xattn_kernel,
        out_shape=jax.ShapeDtypeStruct((nbatch * seq, d), BF16),
        grid=(nbatch, nq),
        in_specs=[
            pl.BlockSpec((tq, d), lambda b, i: (rb0 + b * nq + i, 0)),
            pl.BlockSpec((1, N_MEM, d), lambda b, i: (b, 0, 0)),
            pl.BlockSpec((1, N_MEM, d), lambda b, i: (b, 0, 0)),
        ],
        out_specs=pl.BlockSpec((tq, d), lambda b, i: (b * nq + i, 0)),
        compiler_params=_params("parallel", "arbitrary"),
        name="cross_attention",
    )(q, mem_k, mem_v)


def _routing_tables(route, tm):
    n_tok = route.shape[0]
    n_rows = n_tok * TOP_K
    nb = (n_rows + N_EXPERTS * (tm - 1) + tm - 1) // tm
    flat_e = route[:, :TOP_K].astype(jnp.int32).reshape(-1)
    onehot = (flat_e[:, None] == jnp.arange(N_EXPERTS, dtype=jnp.int32)[None, :]).astype(jnp.int32)
    csum = jnp.cumsum(onehot, axis=0)
    rank = jnp.sum(csum * onehot, axis=1) - 1
    counts = csum[-1]
    padded = (counts + tm - 1) // tm * tm
    pad_end = jnp.cumsum(padded)
    pad_start = pad_end - padded
    dest = jnp.sum(onehot * pad_start[None, :], axis=1) + rank
    n_valid = (pad_end[-1] // tm).astype(jnp.int32)
    blk = jnp.minimum(jnp.arange(nb, dtype=jnp.int32), n_valid - 1) * tm
    block_expert = jnp.minimum(jnp.searchsorted(pad_end, blk, side='right'), N_EXPERTS - 1).astype(jnp.int32)
    dest2 = dest.reshape(n_tok, TOP_K).astype(jnp.int32)
    pad_lo = (pad_start + counts).astype(jnp.int32)
    pad_hi = pad_end.astype(jnp.int32).at[N_EXPERTS - 1].set(nb * tm)
    blk_all = jnp.arange(nb, dtype=jnp.int32) * tm
    rows_used = jnp.clip(pad_lo[block_expert] - blk_all, 0, tm).astype(jnp.int32)
    block_info = jnp.concatenate([n_valid.reshape(1), rows_used])
    return block_expert, block_info, dest2[:, 0], dest2[:, 1], pad_lo, pad_hi, nb * tm


def kernel(x_prompt, x_sample, mem_prompt, cache_diff_k, cache_diff_v, state_ret, state_gla, cache_mem_k, cache_mem_v, g_mix, g_xattn, g_mem, g_ffn, g_final, w_in_even, w_out_even, lambda_q1, lambda_k1, lambda_q2, lambda_k2, w_in_odd, w_gate_lr, b_gate_lr, w_out_odd, w_xq, w_xkv, w_xo, w_ffn_gu, w_ffn_dn, w_router, w_moe_gu, w_moe_dn):
    d = D_MODEL
    bp, tp, _ = x_prompt.shape
    bs, ts, _ = x_sample.shape
    past = cache_diff_k.shape[2]
    np_tok = bp * tp
    ns_tok = bs * ts
    n_tok = np_tok + ns_tok
    depth = g_mix.shape[0]

    x = (x_prompt.reshape(np_tok, d), x_sample.reshape(ns_tok, d))

    half = DA // 2
    inv = 1.0 / (ROPE_BASE ** jnp.linspace(0.0, 1.0, half, dtype=F32))
    pos = jnp.arange(max(tp, past + ts), dtype=jnp.int32).astype(F32)
    ang = pos[:, None] * inv[None, :]
    cos_t, sin_t = jnp.cos(ang), jnp.sin(ang)
    log_gamma = jnp.log1p(-jnp.power(2.0, -5.0 - jnp.arange(HA, dtype=F32)))


    ret_p, ret_s, gla_p, gla_s = [], [], [], []
    dk_p, dv_p, dk_s, dv_s = [], [], [], []
    mk_p, mv_p = [], []
    y_rows = route = d0 = d1 = None

    for i in range(depth):
        j = i // 2
        g_i = g_mix[i].reshape(1, d)
        if i % 2 == 0:
            w_in = w_in_even[j].astype(BF16)
            c_dk = 4 * HA * DA + HB * 2 * DB
            w_kv = HB * 2 * DB
            assert HB * DVB == w_kv
            if isinstance(x, tuple):
                (x_p, x_s), off_s = x, 0
                proj_p = norm_matmul(x_p, g_i, w_in, col0=0, ncols=c_dk, tn=1024, out_dtype=PROJ_DTYPE,
                                     tm=PROMPT_PROJ_ROW_TILE)
                proj_s = norm_matmul(x_s, g_i, w_in, col0=0, ncols=c_dk, tn=1024, out_dtype=PROJ_DTYPE)
            else:
                x_p, x_s, off_s = x, x, np_tok
                proj_p = proj_s = norm_matmul(x, g_i, w_in, col0=0, ncols=c_dk, tn=1024, out_dtype=PROJ_DTYPE,
                                              tm=PROJ_ROW_TILE)
            dk_f_p, dk_b_p, dv_f_p, dv_b_p = norm_matmul_groups(
                x_p, g_i, w_in, row0=0, nrows=np_tok, col0=c_dk, group_cols=w_kv, n_groups=2,
                out_dtypes=(F32, BF16))
            dk_f_s, dk_b_s, dv_f_s, dv_b_s = norm_matmul_groups(
                x_s, g_i, w_in, row0=off_s, nrows=ns_tok, col0=c_dk, group_cols=w_kv, n_groups=2,
                out_dtypes=(F32, BF16))
            lam_init = 0.8 - 0.6 * math.exp(-0.3 * i)
            lam_params = jnp.stack([lambda_q1[j], lambda_k1[j], lambda_q2[j], lambda_k2[j]]).astype(F32)
            a_p, s_p = retention(proj_p, cos_t, sin_t, log_gamma, nbatch=bp, seq=tp, row0=0, pos0=0, L=RET_CHUNK)
            a_s, s_s = retention(proj_s, cos_t, sin_t, log_gamma, nbatch=bs, seq=ts, row0=off_s, pos0=past,
                                 L=CHUNK, state0=state_ret[j])
            ret_p.append(s_p)
            ret_s.append(s_s)
            b_p = diff_attention_prompt(proj_p, dk_b_p, dv_b_p, lam_params, nbatch=bp, seq=tp, lam_init=lam_init)
            b_s = diff_attention_sample(
                proj_s, dk_b_s, dv_b_s, cache_diff_k[j].reshape(bs, past, HB * 2 * DB),
                cache_diff_v[j].reshape(bs, past, HB * DVB), lam_params, nbatch=bs, seq=ts, row0=off_s,
                lam_init=lam_init)
            dk_p.append(dk_f_p.reshape(bp, tp, HB, 2, DB))
            dk_s.append(dk_f_s.reshape(bs, ts, HB, 2, DB))
            dv_p.append(dv_f_p.reshape(bp, tp, HB, DVB))
            dv_s.append(dv_f_s.reshape(bs, ts, HB, DVB))
            x = matmul_residual([(a_p, a_s), (b_p, b_s)], w_out_even[j].astype(BF16), x)
        else:
            n_main = 2 * HC * DKC + 2 * HC * DVC
            w_in = w_in_odd[j]
            proj = norm_matmul(x, g_i, w_in[:, :n_main].astype(BF16), col0=0, ncols=n_main, tn=1024, out_dtype=PROJ_DTYPE,
                               tm=PROJ_ROW_TILE)
            w_lr = jnp.zeros((d, LANES), BF16).at[:, :GATE_RANK].set(w_in[:, n_main:].astype(BF16))
            w_a2 = jnp.zeros((LANES, HC * DKC), BF16).at[:GATE_RANK].set(w_gate_lr[j].astype(BF16))
            log_a = gla_gate(x, g_i, w_lr, w_a2, b_gate_lr[j].reshape(1, -1).astype(F32))
            o_p, s_p = gla(proj, log_a, nbatch=bp, seq=tp, row0=0, rows_per_step=GLA_BLOCK)
            o_s, s_s = gla(proj, log_a, nbatch=bs, seq=ts, row0=np_tok, rows_per_step=ts, state0=state_gla[j])
            gla_p.append(s_p)
            gla_s.append(s_s)
            x = matmul_residual([(o_p, o_s)], w_out_odd[j].astype(BF16), x)

        (q,) = norm_matmul_groups(x, g_xattn[i].reshape(1, d), w_xq[i].astype(BF16), row0=0, nrows=n_tok, col0=0,
                                  group_cols=HX * DX, n_groups=1, out_dtypes=(BF16,))
        mem2d = mem_prompt.reshape(bp * N_MEM, d)
        mk, mv = norm_matmul_groups(mem2d, g_mem[i].reshape(1, d), w_xkv[i].astype(BF16), row0=0,
                                    nrows=bp * N_MEM, col0=0, group_cols=HX * DX, n_groups=2, out_dtypes=(F32,),
                                    tm=MEM_ROW_TILE)
        mk_p.append(mk.reshape(bp, N_MEM, HX, DX))
        mv_p.append(mv.reshape(bp, N_MEM, HX, DX))
        o_p = cross_attention(q, mk.reshape(bp, N_MEM, HX * DX), mv.reshape(bp, N_MEM, HX * DX),
                              nbatch=bp, seq=tp, row0=0, tq=512)
        o_s = cross_attention(q, cache_mem_k[i].reshape(bs, N_MEM, HX * DX),
                              cache_mem_v[i].reshape(bs, N_MEM, HX * DX), nbatch=bs, seq=ts, row0=np_tok, tq=ts)
        x = matmul_residual([(o_p, o_s)], w_xo[i].astype(BF16), x)

        g_f = g_ffn[i].reshape(1, d)
        if i % 2 == 0:
            dense_blocks = n_tok // DENSE_FFN_ROW_TILE
            x = swiglu_blocks(x, g_f, w_ffn_gu[j][None], w_ffn_dn[j][None],
                              jnp.zeros((dense_blocks,), jnp.int32),
                              jnp.full((1 + dense_blocks,), dense_blocks, jnp.int32),
                              residual=True, tm=DENSE_FFN_ROW_TILE)
        else:
            w_r = jnp.zeros((d, LANES), BF16).at[:, :N_EXPERTS].set(w_router[j].astype(BF16))
            route = router(x, g_f, w_r)
            block_expert, n_valid, d0, d1, pad_lo, pad_hi, n_rows = _routing_tables(route, MOE_ROW_TILE)
            xs = moe_dispatch(x, g_f, d0, d1, pad_lo, pad_hi, n_rows)
            y_rows = swiglu_blocks(xs, g_f, w_moe_gu[j], w_moe_dn[j], block_expert, n_valid, residual=False,
                                   tm=MOE_ROW_TILE)
            if i != depth - 1:
                raise NotImplementedError("MoE layer must be the last layer")

    g_fin = g_final.reshape(1, d)
    y_p = moe_combine_norm(x, route, g_fin, y_rows, d0, d1, row0=0, nrows=np_tok)
    y_s = moe_combine_norm(x, route, g_fin, y_rows, d0, d1, row0=np_tok, nrows=ns_tok)

    return (y_p.reshape(bp, tp, d), y_s.reshape(bs, ts, d),
            jnp.stack(dk_p), jnp.stack(dv_p), jnp.stack(ret_p), jnp.stack(gla_p),
            jnp.stack(mk_p), jnp.stack(mv_p),
            jnp.stack(dk_s), jnp.stack(dv_s), jnp.stack(ret_s), jnp.stack(gla_s))
```
